```python
import jax
import jax.numpy as jnp
from jax import lax
import numpy as np

D_MODEL = 1024
BATCH = 2
SEQ = 16384
DEPTH = 4

GRID_W = 64
CTX_LEN = 256
HEAD_DIM = 64
SCALE = HEAD_DIM ** -0.5
ROPE_HALF = HEAD_DIM // 2
ROPE_FREQS = HEAD_DIM // 4
ROPE_THETA = 10000.0
Q_BLOCK = 128
A_HEADS = 8
A_KV_HEADS = 2
B_HEADS = 8
B_KV_HEADS = 2
WINDOW = 128
C_HEADS = 8
NA_ROWS = 8
NA_COLS = 16
D_HEADS = 8
DECAY_LORA = 64
ICLR_LORA = 64
GATE_LORA = 128
GN_EPS = 64e-5
RMS_EPS = 1e-6
NEG_INF = -1e30
FFN_HIDDEN = -(-8 * D_MODEL // (3 * 256)) * 256

A_Q = A_HEADS * HEAD_DIM
A_KV = A_KV_HEADS * HEAD_DIM
B_Q = B_HEADS * HEAD_DIM
B_KV = B_KV_HEADS * HEAD_DIM
C_W = C_HEADS * HEAD_DIM
D_W = D_HEADS * HEAD_DIM
EVEN_WIDTHS = (A_Q, A_KV, A_KV, B_Q, B_KV, B_KV)
EVEN_IN = A_Q + 2 * A_KV + B_Q + 2 * B_KV
EVEN_MIX = A_Q + B_Q
D_SHIFT_W = 3 * D_W + DECAY_LORA + ICLR_LORA + GATE_LORA
ODD_IN = 3 * C_W + D_SHIFT_W
ODD_MIX = C_W + D_W
N_EVEN = (DEPTH + 1) // 2
N_ODD = DEPTH // 2

kernel_name = "hybrid_axialgqa_swa_natten_rwkv7_dit"


def _split(t, widths):
    return jnp.split(t, [int(o) for o in np.cumsum(widths)[:-1]], axis=-1)


def rms_norm(t, gain):
    tf = t.astype(jnp.float32)
    tf = tf * lax.rsqrt(jnp.mean(tf * tf, axis=-1, keepdims=True) + RMS_EPS)
    return (tf * gain.astype(jnp.float32)).astype(t.dtype)


def modulate(t, shift, scale):
    return t * (1 + scale) + shift


def heads(t, n):
    return t.reshape(t.shape[:-1] + (n, HEAD_DIM))


def group(q, n_kv):
    return q.reshape(q.shape[:2] + (n_kv, q.shape[2] // n_kv, HEAD_DIM))


def axial_rope_tables(n_tokens):
    t = jnp.arange(n_tokens, dtype=jnp.int32)
    row = (t // GRID_W).astype(jnp.float32)
    col = (t % GRID_W).astype(jnp.float32)
    inv = ROPE_THETA ** (-jnp.arange(ROPE_FREQS, dtype=jnp.float32) / ROPE_FREQS)
    ang = jnp.concatenate([row[:, None] * inv, col[:, None] * inv], axis=-1)
    return jnp.cos(ang), jnp.sin(ang)


def apply_rope(t, cos, sin):
    c = cos[None, :, None, :].astype(t.dtype)
    s = sin[None, :, None, :].astype(t.dtype)
    t1, t2 = t[..., :ROPE_HALF], t[..., ROPE_HALF:]
    return jnp.concatenate([t1 * c - t2 * s, t1 * s + t2 * c], axis=-1)


def _sink_column(sink, shape):
    col = sink.astype(jnp.float32).reshape(1, shape[1], shape[2], 1, 1)
    return jnp.broadcast_to(col, shape[:-1] + (1,))


def dense_attention(q, k, v, sink=None):
    B, L, HKV, G, _ = q.shape
    n = k.shape[1]
    s = jnp.einsum('bqkgd,bnkd->bkgqn', q, k, preferred_element_type=jnp.float32) * SCALE
    if sink is not None:
        s = jnp.concatenate([s, _sink_column(sink, s.shape)], axis=-1)
    p = jax.nn.softmax(s, axis=-1)[..., :n].astype(v.dtype)
    return jnp.einsum('bkgqn,bnkd->bqkgd', p, v).reshape(B, L, HKV * G * HEAD_DIM)


def global_attention(q, k, v, kc, vc):
    B, S, HKV, G, _ = q.shape
    nblk = S // Q_BLOCK
    keys = jnp.concatenate([k, kc], axis=1)
    vals = jnp.concatenate([v, vc], axis=1)
    qb = jnp.moveaxis(q.reshape(B, nblk, Q_BLOCK, HKV, G, HEAD_DIM), 1, 0)

    def block(qblk):
        s = jnp.einsum('bqkgd,bnkd->bkgqn', qblk, keys, preferred_element_type=jnp.float32) * SCALE
        p = jax.nn.softmax(s, axis=-1).astype(vals.dtype)
        return jnp.einsum('bkgqn,bnkd->bqkgd', p, vals)

    o = lax.map(block, qb)
    return jnp.moveaxis(o, 0, 1).reshape(B, S, HKV * G * HEAD_DIM)


def window_attention(q, k, v, kc, vc, sink):
    B, S, HKV, G, _ = q.shape
    nblk = S // Q_BLOCK
    span = Q_BLOCK + 2 * WINDOW
    n_ctx = kc.shape[1]
    pad = ((0, 0), (WINDOW, WINDOW), (0, 0), (0, 0))
    kp, vp = jnp.pad(k, pad), jnp.pad(v, pad)
    qb = jnp.moveaxis(q.reshape(B, nblk, Q_BLOCK, HKV, G, HEAD_DIM), 1, 0)
    qi = jnp.arange(Q_BLOCK)[:, None]
    kj = jnp.arange(span)[None, :]
    band = jnp.abs(kj - WINDOW - qi) <= WINDOW

    def block(args):
        qblk, start = args
        ks = lax.dynamic_slice_in_dim(kp, start, span, axis=1)
        vs = lax.dynamic_slice_in_dim(vp, start, span, axis=1)
        pos = start - WINDOW + kj
        valid = band & (pos >= 0) & (pos < S)
        s_loc = jnp.einsum('bqkgd,bnkd->bkgqn', qblk, ks, preferred_element_type=jnp.float32) * SCALE
        s_loc = jnp.where(valid, s_loc, NEG_INF)
        s_ctx = jnp.einsum('bqkgd,bnkd->bkgqn', qblk, kc, preferred_element_type=jnp.float32) * SCALE
        s = jnp.concatenate([s_loc, s_ctx, _sink_column(sink, s_loc.shape)], axis=-1)
        p = jax.nn.softmax(s, axis=-1).astype(v.dtype)
        return (jnp.einsum('bkgqn,bnkd->bqkgd', p[..., :span], vs)
                + jnp.einsum('bkgqn,bnkd->bqkgd', p[..., span:span + n_ctx], vc))

    o = lax.map(block, (qb, jnp.arange(nblk) * Q_BLOCK))
    return jnp.moveaxis(o, 0, 1).reshape(B, S, HKV * G * HEAD_DIM)


def neighborhood_attention(q, k, v, kc, vc, rpb):
    B, S, H, _ = q.shape
    rows = S // GRID_W
    kr = min(NA_ROWS, rows)
    n_loc = kr * NA_COLS
    qg = jnp.moveaxis(q.reshape(B, rows, GRID_W, H, HEAD_DIM), 1, 0)
    kg = k.reshape(B, rows, GRID_W, H, HEAD_DIM)
    vg = v.reshape(B, rows, GRID_W, H, HEAD_DIM)
    col = jnp.arange(GRID_W)
    col_idx = jnp.clip(col - NA_COLS // 2, 0, GRID_W - NA_COLS)[:, None] + jnp.arange(NA_COLS)[None, :]
    dcol = col_idx - col[:, None] + NA_COLS - 1
    row_start = jnp.clip(jnp.arange(rows) - kr // 2, 0, rows - kr)

    def block(args):
        q_row, r, rs = args
        ks = lax.dynamic_slice_in_dim(kg, rs, kr, axis=1)[:, :, col_idx]
        vs = lax.dynamic_slice_in_dim(vg, rs, kr, axis=1)[:, :, col_idx]
        drow = rs + jnp.arange(kr) - r + NA_ROWS - 1
        bias = rpb[:, drow[None, :, None], dcol[:, None, :]].astype(jnp.float32)
        s_loc = jnp.einsum('bchd,bicjhd->bhcij', q_row, ks, preferred_element_type=jnp.float32) * SCALE + bias
        s_ctx = jnp.einsum('bchd,bnhd->bhcn', q_row, kc, preferred_element_type=jnp.float32) * SCALE
        s = jnp.concatenate([s_loc.reshape(B, H, GRID_W, n_loc), s_ctx], axis=-1)
        p = jax.nn.softmax(s, axis=-1).astype(v.dtype)
        p_loc = p[..., :n_loc].reshape(B, H, GRID_W, kr, NA_COLS)
        return (jnp.einsum('bhcij,bicjhd->bchd', p_loc, vs)
                + jnp.einsum('bhcn,bnhd->bchd', p[..., n_loc:], vc))

    o = lax.map(block, (qg, jnp.arange(rows), row_start))
    return jnp.moveaxis(o, 0, 1).reshape(B, S, H * HEAD_DIM)


def centred_shift_mix(z, mu):
    zp = jnp.pad(z, ((0, 0), (1, 1), (0, 0)))
    return z + (0.5 * (zp[:, :-2] + zp[:, 2:]) - z) * mu


def rwkv7_inputs(zd, w0, w2, a0, a2, g2, k_k, k_a):
    r, k, v, wd, ad, gd = _split(zd, (D_W, D_W, D_W, DECAY_LORA, ICLR_LORA, GATE_LORA))
    g = jax.nn.sigmoid(gd) @ g2
    kk = heads(k * k_k, D_HEADS).astype(jnp.float32)
    kk = kk / jnp.maximum(jnp.sqrt(jnp.sum(kk * kk, axis=-1, keepdims=True)), 1e-12)
    decay, kd, av = [], [], []
    for d in range(2):
        w = -jax.nn.softplus(-(w0[d] + jnp.tanh(wd) @ w2[d])) - 0.5
        a = jax.nn.sigmoid(a0[d] + ad @ a2[d])
        decay.append(heads(jnp.exp(-jnp.exp(w.astype(jnp.float32))), D_HEADS))
        kd.append(heads(k * (1 + (a - 1) * k_a), D_HEADS))
        av.append(heads(a, D_HEADS))
    return {"r": heads(r, D_HEADS), "v": heads(v, D_HEADS), "kk": kk, "g": g,
            "decay": decay, "k": kd, "a": av}


def wkv7_scan(state0, p, d, reverse, emit):
    seq = [p["decay"][d], p["k"][d], p["v"], p["kk"], p["a"][d]] + ([p["r"]] if emit else [])
    xs = tuple(jnp.moveaxis(t.astype(jnp.float32), 1, 0) for t in seq)

    def step(S, inp):
        w_t, k_t, v_t, kk_t, a_t = inp[:5]
        sa = jnp.einsum('bhvk,bhk->bhv', S, kk_t)
        S = (S * w_t[:, :, None, :] - sa[..., None] * (kk_t * a_t)[:, :, None, :]
             + v_t[..., None] * k_t[:, :, None, :])
        return S, (jnp.einsum('bhvk,bhk->bhv', S, inp[5]) if emit else None)

    S, ys = lax.scan(step, state0, xs, reverse=reverse)
    return S, (jnp.moveaxis(ys, 0, 1) if emit else None)


def rwkv7_readout(y, p, r_k, ln_w, ln_b):
    B, L, H, N = y.shape
    mean = jnp.mean(y, axis=-1, keepdims=True)
    var = jnp.mean(jnp.square(y - mean), axis=-1, keepdims=True)
    yn = ((y - mean) * lax.rsqrt(var + GN_EPS)).reshape(B, L, H * N)
    yn = yn * ln_w.astype(jnp.float32) + ln_b.astype(jnp.float32)
    r = p["r"].astype(jnp.float32)
    ksum = (p["k"][0] + p["k"][1]).astype(jnp.float32)
    bonus = jnp.sum(r * ksum * r_k.astype(jnp.float32), axis=-1, keepdims=True) * p["v"].astype(jnp.float32)
    return (yn + bonus.reshape(B, L, H * N)) * p["g"].astype(jnp.float32)


def even_mixer(h, hc, w_in, w_out, q_gain, k_gain, sink, cos, sin, need_ctx):
    qa, ka, va, qb, kb, vb = _split(h @ w_in, EVEN_WIDTHS)
    qca, kca, vca, qcb, kcb, vcb = _split(hc @ w_in, EVEN_WIDTHS)
    qa = apply_rope(rms_norm(heads(qa, A_HEADS), q_gain), cos, sin)
    ka = apply_rope(rms_norm(heads(ka, A_KV_HEADS), k_gain), cos, sin)
    kca = rms_norm(heads(kca, A_KV_HEADS), k_gain)
    va, vca = heads(va, A_KV_HEADS), heads(vca, A_KV_HEADS)
    qb = apply_rope(heads(qb, B_HEADS), cos, sin)
    kb = apply_rope(heads(kb, B_KV_HEADS), cos, sin)
    kcb, vb, vcb = heads(kcb, B_KV_HEADS), heads(vb, B_KV_HEADS), heads(vcb, B_KV_HEADS)
    y_a = global_attention(group(qa, A_KV_HEADS), ka, va, kca, vca)
    y_b = window_attention(group(qb, B_KV_HEADS), kb, vb, kcb, vcb, sink)
    y = jnp.concatenate([y_a, y_b], axis=-1) @ w_out
    if not need_ctx:
        return y, None
    qca = rms_norm(heads(qca, A_HEADS), q_gain)
    yc_a = dense_attention(group(qca, A_KV_HEADS), kca, vca)
    yc_b = dense_attention(group(heads(qcb, B_HEADS), B_KV_HEADS), kcb, vcb, sink)
    return y, jnp.concatenate([yc_a, yc_b], axis=-1) @ w_out


def odd_mixer(h, hc, w_in, w_out, rpb, mu, w0, w2, a0, a2, g2, k_k, k_a, r_k, ln_w, ln_b, need_ctx):
    q, k, v, zd = _split(h @ w_in, (C_W, C_W, C_W, D_SHIFT_W))
    qc, kc, vc, zdc = _split(hc @ w_in, (C_W, C_W, C_W, D_SHIFT_W))
    kc, vc = heads(kc, C_HEADS), heads(vc, C_HEADS)
    y_c = neighborhood_attention(heads(q, C_HEADS), heads(k, C_HEADS), heads(v, C_HEADS), kc, vc, rpb)
    lat = rwkv7_inputs(centred_shift_mix(zd, mu), w0, w2, a0, a2, g2, k_k, k_a)
    cx = rwkv7_inputs(centred_shift_mix(zdc, mu), w0, w2, a0, a2, g2, k_k, k_a)
    zero = jnp.zeros((h.shape[0], D_HEADS, HEAD_DIM, HEAD_DIM), jnp.float32)
    s_fwd, yc_fwd = wkv7_scan(zero, cx, 0, False, need_ctx)
    s_bwd, yc_bwd = wkv7_scan(zero, cx, 1, True, need_ctx)
    _, y_fwd = wkv7_scan(s_fwd, lat, 0, False, True)
    _, y_bwd = wkv7_scan(s_bwd, lat, 1, True, True)
    y_d = rwkv7_readout(y_fwd + y_bwd, lat, r_k, ln_w, ln_b).astype(h.dtype)
    y = jnp.concatenate([y_c, y_d], axis=-1) @ w_out
    if not need_ctx:
        return y, None
    yc_c = dense_attention(heads(qc, C_HEADS)[:, :, :, None], kc, vc)
    yc_d = rwkv7_readout(yc_fwd + yc_bwd, cx, r_k, ln_w, ln_b).astype(hc.dtype)
    return y, jnp.concatenate([yc_c, yc_d], axis=-1) @ w_out


def swiglu(h, w_in, w_out):
    gate, up = jnp.split(h @ w_in, 2, axis=-1)
    return (jax.nn.silu(gate) * up) @ w_out


def setup_inputs(seed: int = 0) -> dict:
    key = jax.random.key(seed)
    ks = iter(jax.random.split(key, 40))

    def nrm(shape, scale):
        return scale * jax.random.normal(next(ks), shape, jnp.float32)

    D = D_MODEL
    return {
        "x": nrm((BATCH, SEQ, D), 1.0),
        "c": nrm((BATCH, D), 1.0),
        "ctx": nrm((BATCH, CTX_LEN, D), 1.0),
        "c_ctx": nrm((D,), 1.0),
        "w_mod": nrm((DEPTH, D, 6 * D), 0.5 * D ** -0.5),
        "b_mod": nrm((DEPTH, 6 * D), 0.02),
        "norm_mix": 1.0 + nrm((DEPTH, D), 0.02),
        "norm_ffn": 1.0 + nrm((DEPTH, D), 0.02),
        "w_in_even": nrm((N_EVEN, D, EVEN_IN), D ** -0.5),
        "w_out_even": nrm((N_EVEN, EVEN_MIX, D), EVEN_MIX ** -0.5),
        "q_norm_a": 1.0 + nrm((N_EVEN, HEAD_DIM), 0.02),
        "k_norm_a": 1.0 + nrm((N_EVEN, HEAD_DIM), 0.02),
        "sink_b": nrm((N_EVEN, B_HEADS), 0.5),
        "w_in_odd": nrm((N_ODD, D, ODD_IN), D ** -0.5),
        "w_out_odd": nrm((N_ODD, ODD_MIX, D), ODD_MIX ** -0.5),
        "rpb_c": nrm((N_ODD, C_HEADS, 2 * NA_ROWS - 1, 2 * NA_COLS - 1), 0.2),
        "shift_mu": jax.random.uniform(next(ks), (N_ODD, D_SHIFT_W), jnp.float32),
        "decay_w0": -1.0 + nrm((N_ODD, 2, D_W), 0.3),
        "decay_w2": nrm((N_ODD, 2, DECAY_LORA, D_W), 0.1),
        "iclr_a0": nrm((N_ODD, 2, D_W), 0.3),
        "iclr_a2": nrm((N_ODD, 2, ICLR_LORA, D_W), 0.1),
        "gate_g2": nrm((N_ODD, GATE_LORA, D_W), GATE_LORA ** -0.5),
        "k_k": 0.85 + nrm((N_ODD, D_W), 0.05),
        "k_a": 1.0 + nrm((N_ODD, D_W), 0.05),
        "r_k": nrm((N_ODD, D_HEADS, HEAD_DIM), 0.1),
        "ln_x_w": 1.0 + nrm((N_ODD, D_W), 0.02),
        "ln_x_b": nrm((N_ODD, D_W), 0.02),
        "w_ffn_in": nrm((DEPTH, D, 2 * FFN_HIDDEN), D ** -0.5),
        "w_ffn_out": nrm((DEPTH, FFN_HIDDEN, D), FFN_HIDDEN ** -0.5),
        "norm_out": 1.0 + nrm((D,), 0.02),
    }


def reference(x, c, ctx, c_ctx, w_mod, b_mod, norm_mix, norm_ffn, w_in_even, w_out_even, q_norm_a, k_norm_a,
              sink_b, w_in_odd, w_out_odd, rpb_c, shift_mu, decay_w0, decay_w2, iclr_a0, iclr_a2, gate_g2,
              k_k, k_a, r_k, ln_x_w, ln_x_b, w_ffn_in, w_ffn_out, norm_out):
    cos, sin = axial_rope_tables(x.shape[1])
    silu_c = jax.nn.silu(c)
    silu_cc = jax.nn.silu(c_ctx)
    for layer in range(DEPTH):
        need_ctx = layer < DEPTH - 1
        mod = (silu_c @ w_mod[layer] + b_mod[layer])[:, None, :]
        modc = silu_cc @ w_mod[layer] + b_mod[layer]
        sh1, sc1, g1, sh2, sc2, g2 = jnp.split(mod, 6, axis=-1)
        csh1, csc1, cg1, csh2, csc2, cg2 = jnp.split(modc, 6, axis=-1)
        h = modulate(rms_norm(x, norm_mix[layer]), sh1, sc1)
        hc = modulate(rms_norm(ctx, norm_mix[layer]), csh1, csc1)
        i = layer // 2
        if layer % 2 == 0:
            y, yc = even_mixer(h, hc, w_in_even[i], w_out_even[i], q_norm_a[i], k_norm_a[i], sink_b[i],
                               cos, sin, need_ctx)
        else:
            y, yc = odd_mixer(h, hc, w_in_odd[i], w_out_odd[i], rpb_c[i], shift_mu[i], decay_w0[i], decay_w2[i],
                              iclr_a0[i], iclr_a2[i], gate_g2[i], k_k[i], k_a[i], r_k[i], ln_x_w[i], ln_x_b[i],
                              need_ctx)
        x = x + g1 * y
        x = x + g2 * swiglu(modulate(rms_norm(x, norm_ffn[layer]), sh2, sc2), w_ffn_in[layer], w_ffn_out[layer])
        if need_ctx:
            ctx = ctx + cg1 * yc
            ctx = ctx + cg2 * swiglu(modulate(rms_norm(ctx, norm_ffn[layer]), csh2, csc2),
                                     w_ffn_in[layer], w_ffn_out[layer])
    return rms_norm(x, norm_out)
```

```python
import functools

import jax
import jax.numpy as jnp
import numpy as np
from jax import lax
from jax.experimental import pallas as pl
from jax.experimental.pallas import tpu as pltpu

F32 = jnp.float32
BF16 = jnp.bfloat16

D_MODEL = 1024
DEPTH = 4
GRID_W = 64
HEAD_DIM = 64
SCALE = HEAD_DIM ** -0.5
ROPE_HALF = HEAD_DIM // 2
ROPE_FREQS = HEAD_DIM // 4
ROPE_THETA = 10000.0
A_HEADS = 8
A_KV_HEADS = 2
B_HEADS = 8
B_KV_HEADS = 2
WINDOW = 128
C_HEADS = 8
NA_ROWS = 8
NA_COLS = 16
D_HEADS = 8
DECAY_LORA = 64
ICLR_LORA = 64
GATE_LORA = 128
GN_EPS = 64e-5
RMS_EPS = 1e-6
NEG_INF = -1e30
FFN_HIDDEN = -(-8 * D_MODEL // (3 * 256)) * 256

A_Q = A_HEADS * HEAD_DIM
A_KV = A_KV_HEADS * HEAD_DIM
B_Q = B_HEADS * HEAD_DIM
B_KV = B_KV_HEADS * HEAD_DIM
C_W = C_HEADS * HEAD_DIM
D_W = D_HEADS * HEAD_DIM
EVEN_WIDTHS = (A_Q, A_KV, A_KV, B_Q, B_KV, B_KV)
D_SHIFT_W = 3 * D_W + DECAY_LORA + ICLR_LORA + GATE_LORA

VMEM_LIMIT_BYTES = 56 * 1024 * 1024
RWKV_CHUNK = 64
RWKV_INV_BLOCK = 16
NA_TILE_ROWS = 8
NA_KEY_ROWS = NA_TILE_ROWS + NA_ROWS - 1


def _cparams(*sem):
    return pltpu.CompilerParams(dimension_semantics=sem, vmem_limit_bytes=VMEM_LIMIT_BYTES)


def _split(t, widths):
    return jnp.split(t, [int(o) for o in np.cumsum(widths)[:-1]], axis=-1)


def _pick_tile(n, target):
    t = min(n, target)
    while n % t:
        t //= 2
    return t


def _mod_kernel(a_ref, w_ref, b_ref, o_ref):
    w = w_ref[0].astype(BF16)
    o_ref[0] = jnp.dot(a_ref[...], w, preferred_element_type=F32) + b_ref[0]


def modulation(act, w_mod, b_mod):
    depth, d, n = w_mod.shape
    tn = 1536
    return pl.pallas_call(
        _mod_kernel,
        grid=(depth, n // tn),
        in_specs=[
            pl.BlockSpec((8, d), lambda l, j: (0, 0)),
            pl.BlockSpec((1, d, tn), lambda l, j: (l, 0, j)),
            pl.BlockSpec((1, 1, tn), lambda l, j: (l, 0, j)),
        ],
        out_specs=pl.BlockSpec((1, 8, tn), lambda l, j: (l, 0, j)),
        out_shape=jax.ShapeDtypeStruct((depth, 8, n), F32),
        compiler_params=_cparams("parallel", "parallel"),
        name="modulation",
    )(act, w_mod, b_mod.reshape(depth, 1, n))


def _norm_mod(x, gain, shift, scale):
    ms = jnp.mean(x * x, axis=-1, keepdims=True)
    h = x * lax.rsqrt(ms + RMS_EPS) * gain
    return h * (1.0 + scale) + shift


def _norm_proj_kernel(x_ref, gain_ref, sh_ref, sc_ref, w_ref, o_ref):
    h = _norm_mod(x_ref[0], gain_ref[...], sh_ref[0], sc_ref[0])
    o_ref[0] = jnp.dot(h.astype(BF16), w_ref[...], preferred_element_type=F32)


def norm_proj(x, gain, shift, scale, w):
    b, s, d = x.shape
    n = w.shape[1]
    tm = _pick_tile(s, 512)
    vec = pl.BlockSpec((1, 1, d), lambda bi, i: (bi, 0, 0))
    return pl.pallas_call(
        _norm_proj_kernel,
        grid=(b, s // tm),
        in_specs=[
            pl.BlockSpec((1, tm, d), lambda bi, i: (bi, i, 0)),
            pl.BlockSpec((1, d), lambda bi, i: (0, 0)),
            vec, vec,
            pl.BlockSpec((d, n), lambda bi, i: (0, 0)),
        ],
        out_specs=pl.BlockSpec((1, tm, n), lambda bi, i: (bi, i, 0)),
        out_shape=jax.ShapeDtypeStruct((b, s, n), F32),
        compiler_params=_cparams("parallel", "parallel"),
        name="norm_proj",
    )(x, gain, shift, scale, w)


def _out_proj_kernel(x_ref, g_ref, y1_ref, y2_ref, w1_ref, w2_ref, o_ref):
    y = jnp.dot(y1_ref[0], w1_ref[...], preferred_element_type=F32)
    y = y + jnp.dot(y2_ref[0], w2_ref[...], preferred_element_type=F32)
    o_ref[0] = x_ref[0] + g_ref[0] * y


def out_proj(x, gate, y1, y2, w1, w2):
    b, s, d = x.shape
    k1, k2 = y1.shape[-1], y2.shape[-1]
    tm = _pick_tile(s, 1024)
    return pl.pallas_call(
        _out_proj_kernel,
        grid=(b, s // tm),
        in_specs=[
            pl.BlockSpec((1, tm, d), lambda bi, i: (bi, i, 0)),
            pl.BlockSpec((1, 1, d), lambda bi, i: (bi, 0, 0)),
            pl.BlockSpec((1, tm, k1), lambda bi, i: (bi, i, 0)),
            pl.BlockSpec((1, tm, k2), lambda bi, i: (bi, i, 0)),
            pl.BlockSpec((k1, d), lambda bi, i: (0, 0)),
            pl.BlockSpec((k2, d), lambda bi, i: (0, 0)),
        ],
        out_specs=pl.BlockSpec((1, tm, d), lambda bi, i: (bi, i, 0)),
        out_shape=jax.ShapeDtypeStruct((b, s, d), F32),
        compiler_params=_cparams("parallel", "parallel"),
        name="out_proj",
    )(x, gate, y1, y2, w1, w2)


def _ffn_kernel(x_ref, gain_ref, sh_ref, sc_ref, g_ref, wg_ref, wu_ref, wo_ref, o_ref, h_ref, acc_ref):
    j = pl.program_id(2)

    @pl.when(j == 0)
    def _():
        h_ref[...] = _norm_mod(x_ref[0], gain_ref[...], sh_ref[0], sc_ref[0]).astype(BF16)
        acc_ref[...] = jnp.zeros_like(acc_ref)

    h = h_ref[...]
    gate = jnp.dot(h, wg_ref[...], preferred_element_type=F32)
    up = jnp.dot(h, wu_ref[...], preferred_element_type=F32)
    act = gate * (1.0 / (1.0 + jnp.exp(-gate))) * up
    acc_ref[...] += jnp.dot(act.astype(BF16), wo_ref[...], preferred_element_type=F32)

    @pl.when(j == pl.num_programs(2) - 1)
    def _():
        o_ref[0] = x_ref[0] + g_ref[0] * acc_ref[...]


def ffn(x, gain, shift, scale, gate, w_in, w_out):
    b, s, d = x.shape
    f = w_out.shape[0]
    tm = _pick_tile(s, 512)
    th = 1408 if f % 1408 == 0 else _pick_tile(f, 256)
    nj = f // th
    vec = pl.BlockSpec((1, 1, d), lambda bi, i, j: (bi, 0, 0))
    return pl.pallas_call(
        _ffn_kernel,
        grid=(b, s // tm, nj),
        in_specs=[
            pl.BlockSpec((1, tm, d), lambda bi, i, j: (bi, i, 0)),
            pl.BlockSpec((1, d), lambda bi, i, j: (0, 0)),
            vec, vec, vec,
            pl.BlockSpec((d, th), lambda bi, i, j: (0, j)),
            pl.BlockSpec((d, th), lambda bi, i, j: (0, j + nj)),
            pl.BlockSpec((th, d), lambda bi, i, j: (j, 0)),
        ],
        out_specs=pl.BlockSpec((1, tm, d), lambda bi, i, j: (bi, i, 0)),
        out_shape=jax.ShapeDtypeStruct((b, s, d), F32),
        scratch_shapes=[pltpu.VMEM((tm, d), BF16), pltpu.VMEM((tm, d), F32)],
        compiler_params=_cparams("parallel", "parallel", "arbitrary"),
        name="ffn",
    )(x, gain, shift, scale, gate, w_in, w_in, w_out)


def _rms_kernel(x_ref, gain_ref, o_ref):
    x = x_ref[0]
    ms = jnp.mean(x * x, axis=-1, keepdims=True)
    o_ref[0] = x * lax.rsqrt(ms + RMS_EPS) * gain_ref[...]


def final_norm(x, gain):
    b, s, d = x.shape
    tm = _pick_tile(s, 1024)
    return pl.pallas_call(
        _rms_kernel,
        grid=(b, s // tm),
        in_specs=[pl.BlockSpec((1, tm, d), lambda bi, i: (bi, i, 0)),
                  pl.BlockSpec((1, d), lambda bi, i: (0, 0))],
        out_specs=pl.BlockSpec((1, tm, d), lambda bi, i: (bi, i, 0)),
        out_shape=jax.ShapeDtypeStruct((b, s, d), F32),
        compiler_params=_cparams("parallel", "parallel"),
        name="final_norm",
    )(x, gain)


def _nt_dot(a, b):
    return lax.dot_general(a, b, (((1,), (1,)), ((), ())), preferred_element_type=F32)


def _flash_kernel(q_ref, k_ref, v_ref, o_ref, m_ref, l_ref, acc_ref):
    kv = pl.program_id(3)
    g, tq, hd = q_ref.shape[1:]

    @pl.when(kv == 0)
    def _():
        m_ref[...] = jnp.full_like(m_ref, -jnp.inf)
        l_ref[...] = jnp.zeros_like(l_ref)
        acc_ref[...] = jnp.zeros_like(acc_ref)

    q = q_ref[0].reshape(g * tq, hd)
    s = _nt_dot(q, k_ref[0, 0])
    m_prev = m_ref[...]
    m_new = jnp.maximum(m_prev, jnp.max(s, axis=-1, keepdims=True))
    alpha = jnp.exp(m_prev - m_new)
    p = jnp.exp(s - m_new)
    l_ref[...] = alpha * l_ref[...] + jnp.sum(p, axis=-1, keepdims=True)
    acc_ref[...] = alpha * acc_ref[...] + jnp.dot(p.astype(BF16), v_ref[0, 0], preferred_element_type=F32)
    m_ref[...] = m_new

    @pl.when(kv == pl.num_programs(3) - 1)
    def _():
        o_ref[0] = (acc_ref[...] / l_ref[...]).reshape(g, tq, hd).astype(o_ref.dtype)


def _key_tile(n):
    for t in (1280, 1024, 640, 512, 256, 128):
        if n % t == 0:
            return t
    return n


def global_attention(q, k, v):
    b, h, s, hd = q.shape
    hkv, n = k.shape[1], k.shape[2]
    g = h // hkv
    tq = _pick_tile(s, 256)
    tk = _key_tile(n)
    return pl.pallas_call(
        _flash_kernel,
        grid=(b, hkv, s // tq, n // tk),
        in_specs=[
            pl.BlockSpec((1, g, tq, hd), lambda bi, gi, i, j: (bi, gi, i, 0)),
            pl.BlockSpec((1, 1, tk, hd), lambda bi, gi, i, j: (bi, gi, j, 0)),
            pl.BlockSpec((1, 1, tk, hd), lambda bi, gi, i, j: (bi, gi, j, 0)),
        ],
        out_specs=pl.BlockSpec((1, g, tq, hd), lambda bi, gi, i, j: (bi, gi, i, 0)),
        out_shape=jax.ShapeDtypeStruct((b, h, s, hd), BF16),
        scratch_shapes=[pltpu.VMEM((g * tq, 1), F32), pltpu.VMEM((g * tq, 1), F32),
                        pltpu.VMEM((g * tq, hd), F32)],
        compiler_params=_cparams("parallel", "parallel", "parallel", "arbitrary"),
        name="global_attention",
    )(q, k, v)


def _window_kernel(sink_ref, q_ref, kp_ref, kc_ref, kn_ref, vp_ref, vc_ref, vn_ref, kx_ref, vx_ref, o_ref,
                   *, seq):
    gi = pl.program_id(1)
    i = pl.program_id(2)
    g, tq, hd = q_ref.shape[1:]
    w = kp_ref.shape[2]
    q = q_ref[0].reshape(g * tq, hd)
    k = jnp.concatenate([kp_ref[0, 0], kc_ref[0, 0], kn_ref[0, 0]], axis=0)
    v = jnp.concatenate([vp_ref[0, 0], vc_ref[0, 0], vn_ref[0, 0]], axis=0)
    span = tq + 2 * w
    s_loc = _nt_dot(q, k)
    row = lax.broadcasted_iota(jnp.int32, (g * tq, span), 0)
    col = lax.broadcasted_iota(jnp.int32, (g * tq, span), 1)
    qpos = i * tq + row % tq
    kpos = i * tq - w + col
    valid = (jnp.abs(kpos - qpos) <= w) & (kpos >= 0) & (kpos < seq)
    s_loc = jnp.where(valid, s_loc, NEG_INF)
    s_ctx = _nt_dot(q, kx_ref[0, 0])
    hrow = lax.broadcasted_iota(jnp.int32, (g * tq, 1), 0) // tq
    sink = jnp.zeros((g * tq, 1), F32)
    for hh in range(g):
        sink = jnp.where(hrow == hh, sink_ref[gi * g + hh], sink)
    m = jnp.maximum(jnp.maximum(jnp.max(s_loc, axis=-1, keepdims=True),
                                jnp.max(s_ctx, axis=-1, keepdims=True)), sink)
    p_loc = jnp.exp(s_loc - m)
    p_ctx = jnp.exp(s_ctx - m)
    l = jnp.sum(p_loc, axis=-1, keepdims=True) + jnp.sum(p_ctx, axis=-1, keepdims=True) + jnp.exp(sink - m)
    o = jnp.dot(p_loc.astype(BF16), v, preferred_element_type=F32)
    o = o + jnp.dot(p_ctx.astype(BF16), vx_ref[0, 0], preferred_element_type=F32)
    o_ref[0] = (o / l).reshape(g, tq, hd).astype(o_ref.dtype)


def window_attention(q, k, v, kx, vx, sink):
    b, h, s, hd = q.shape
    hkv = k.shape[1]
    g = h // hkv
    n_ctx = kx.shape[2]
    tq = _pick_tile(s, 256)
    r = tq // WINDOW
    nwb = s // WINDOW
    prev = pl.BlockSpec((1, 1, WINDOW, hd), lambda bi, gi, i: (bi, gi, jnp.maximum(i * r - 1, 0), 0))
    cur = pl.BlockSpec((1, 1, tq, hd), lambda bi, gi, i: (bi, gi, i, 0))
    nxt = pl.BlockSpec((1, 1, WINDOW, hd), lambda bi, gi, i: (bi, gi, jnp.minimum((i + 1) * r, nwb - 1), 0))
    cx = pl.BlockSpec((1, 1, n_ctx, hd), lambda bi, gi, i: (bi, gi, 0, 0))
    return pl.pallas_call(
        functools.partial(_window_kernel, seq=s),
        grid=(b, hkv, s // tq),
        in_specs=[
            pl.BlockSpec(memory_space=pltpu.SMEM),
            pl.BlockSpec((1, g, tq, hd), lambda bi, gi, i: (bi, gi, i, 0)),
            prev, cur, nxt, prev, cur, nxt, cx, cx,
        ],
        out_specs=pl.BlockSpec((1, g, tq, hd), lambda bi, gi, i: (bi, gi, i, 0)),
        out_shape=jax.ShapeDtypeStruct((b, h, s, hd), BF16),
        compiler_params=_cparams("parallel", "parallel", "parallel"),
        name="window_attention",
    )(sink, q, k, k, k, v, v, v, kx, vx)


def _ctx_attn_kernel(sink_ref, q_ref, k_ref, v_ref, o_ref):
    hi = pl.program_id(1)
    s = _nt_dot(q_ref[0, 0], k_ref[0, 0])
    sink = sink_ref[hi]
    m = jnp.maximum(jnp.max(s, axis=-1, keepdims=True), sink)
    p = jnp.exp(s - m)
    l = jnp.sum(p, axis=-1, keepdims=True) + jnp.exp(sink - m)
    o = jnp.dot(p.astype(BF16), v_ref[0, 0], preferred_element_type=F32)
    o_ref[0, 0] = (o / l).astype(o_ref.dtype)


def ctx_attention(q, k, v, sink):
    b, h, n, hd = q.shape
    g = h // k.shape[1]
    kv = pl.BlockSpec((1, 1, n, hd), lambda bi, hi: (bi, hi // g, 0, 0))
    qs = pl.BlockSpec((1, 1, n, hd), lambda bi, hi: (bi, hi, 0, 0))
    return pl.pallas_call(
        _ctx_attn_kernel,
        grid=(b, h),
        in_specs=[pl.BlockSpec(memory_space=pltpu.SMEM), qs, kv, kv],
        out_specs=qs,
        out_shape=jax.ShapeDtypeStruct((b, h, n, hd), BF16),
        compiler_params=_cparams("parallel", "parallel"),
        name="ctx_attention",
    )(sink, q, k, v)


def _na_kernel(q_ref, k_ref, v_ref, kx_ref, vx_ref, bias_ref, o_ref, *, rows):
    t = pl.program_id(2)
    nkeys = NA_KEY_ROWS * GRID_W
    kstart = jnp.clip(t * NA_TILE_ROWS - NA_ROWS // 2, 0, rows - NA_KEY_ROWS)
    off = pl.multiple_of(kstart * GRID_W, GRID_W)
    q = q_ref[0, 0]
    kw = k_ref[0, 0, pl.ds(off, nkeys), :]
    vw = v_ref[0, 0, pl.ds(off, nkeys), :]
    s_loc = _nt_dot(q, kw) + bias_ref[0, 0]
    s_ctx = _nt_dot(q, kx_ref[0, 0])
    m = jnp.maximum(jnp.max(s_loc, axis=-1, keepdims=True), jnp.max(s_ctx, axis=-1, keepdims=True))
    p_loc = jnp.exp(s_loc - m)
    p_ctx = jnp.exp(s_ctx - m)
    l = jnp.sum(p_loc, axis=-1, keepdims=True) + jnp.sum(p_ctx, axis=-1, keepdims=True)
    o = jnp.dot(p_loc.astype(BF16), vw, preferred_element_type=F32)
    o = o + jnp.dot(p_ctx.astype(BF16), vx_ref[0, 0], preferred_element_type=F32)
    o_ref[0, 0] = (o / l).astype(o_ref.dtype)


def na_bias_tables(rpb, rows):
    i = np.arange(NA_TILE_ROWS)
    j = np.arange(NA_KEY_ROWS)
    qc = np.arange(GRID_W)
    kc = np.arange(GRID_W)
    cs = np.clip(qc - NA_COLS // 2, 0, GRID_W - NA_COLS)
    col_valid = (kc[None, :] >= cs[:, None]) & (kc[None, :] < cs[:, None] + NA_COLS)
    dcol = np.clip(kc[None, :] - qc[:, None] + NA_COLS - 1, 0, 2 * NA_COLS - 2)
    out = []
    for r0 in (0, NA_TILE_ROWS, rows - NA_TILE_ROWS):
        r = r0 + i
        rs = np.clip(r - NA_ROWS // 2, 0, rows - NA_ROWS)
        kstart = int(np.clip(r0 - NA_ROWS // 2, 0, rows - NA_KEY_ROWS))
        krow = kstart + j
        row_valid = (krow[None, :] >= rs[:, None]) & (krow[None, :] < rs[:, None] + NA_ROWS)
        drow = np.clip(krow[None, :] - r[:, None] + NA_ROWS - 1, 0, 2 * NA_ROWS - 2)
        bias = rpb[:, drow[:, None, :, None], dcol[None, :, None, :]]
        valid = row_valid[:, None, :, None] & col_valid[None, :, None, :]
        bias = jnp.where(valid[None], bias.astype(F32), NEG_INF)
        out.append(bias.reshape(rpb.shape[0], NA_TILE_ROWS * GRID_W, NA_KEY_ROWS * GRID_W))
    return jnp.stack(out)


def neighborhood_attention(q, k, v, kx, vx, bias):
    b, h, s, hd = q.shape
    rows = s // GRID_W
    nt = rows // NA_TILE_ROWS
    tq = NA_TILE_ROWS * GRID_W
    nkeys = NA_KEY_ROWS * GRID_W
    n_ctx = kx.shape[2]
    full = pl.BlockSpec((1, 1, s, hd), lambda bi, hi, t: (bi, hi, 0, 0))
    cx = pl.BlockSpec((1, 1, n_ctx, hd), lambda bi, hi, t: (bi, hi, 0, 0))

    def case(t):
        return jnp.where(t == 0, 0, jnp.where(t == nt - 1, 2, 1))

    return pl.pallas_call(
        functools.partial(_na_kernel, rows=rows),
        grid=(b, h, nt),
        in_specs=[
            pl.BlockSpec((1, 1, tq, hd), lambda bi, hi, t: (bi, hi, t, 0)),
            full, full, cx, cx,
            pl.BlockSpec((1, 1, tq, nkeys), lambda bi, hi, t: (case(t), hi, 0, 0)),
        ],
        out_specs=pl.BlockSpec((1, 1, tq, hd), lambda bi, hi, t: (bi, hi, t, 0)),
        out_shape=jax.ShapeDtypeStruct((b, h, s, hd), BF16),
        compiler_params=_cparams("parallel", "parallel", "arbitrary"),
        name="neighborhood_attention",
    )(q, k, v, kx, vx, bias)


def _mm(a, b):
    return jnp.dot(a, b, preferred_element_type=F32, precision=lax.Precision.HIGHEST)


def _mm_nt(a, b):
    return lax.dot_general(a, b, (((1,), (1,)), ((), ())), preferred_element_type=F32,
                           precision=lax.Precision.HIGHEST)


def _mm_tn(a, b):
    return lax.dot_general(a, b, (((0,), (0,)), ((), ())), preferred_element_type=F32,
                           precision=lax.Precision.HIGHEST)


def _unit_lower_inverse(low, row, col):
    t = low.shape[0]
    eye = (row == col).astype(F32)
    blk = RWKV_INV_BLOCK
    x = -jnp.where(row // blk == col // blk, low, 0.0)
    inv = eye + x
    pw = x
    span = 2
    while span < blk:
        pw = _mm(pw, pw)
        inv = inv + _mm(inv, pw)
        span *= 2
    while blk < t:
        off = jnp.where((row // (2 * blk) == col // (2 * blk)) & (row // blk != col // blk), low, 0.0)
        inv = inv - _mm(_mm(inv, off), inv)
        blk *= 2
    return inv


def _rwkv_kernel(lw_ref, k_ref, a_ref, r_ref, v_ref, kk_ref, y_ref, s_ref):
    c = pl.program_id(2)

    @pl.when(c == 0)
    def _():
        s_ref[...] = jnp.zeros_like(s_ref)

    lw = lw_ref[0, 0]
    kd = k_ref[0, 0]
    r = r_ref[0, 0]
    v = v_ref[0, 0]
    kap = kk_ref[0, 0]
    bvec = kap * a_ref[0, 0]
    t = lw.shape[0]
    row = lax.broadcasted_iota(jnp.int32, (t, t), 0)
    col = lax.broadcasted_iota(jnp.int32, (t, t), 1)
    strict = row > col
    incl = row >= col

    c_incl = _mm(incl.astype(F32), lw)
    c_excl = c_incl - lw
    c_mid = c_incl[t // 2 - 1:t // 2, :]
    c_end = c_incl[t - 1:t, :]
    e_neg = jnp.exp(c_mid - c_incl)
    left = jnp.concatenate([kap * jnp.exp(c_excl - c_mid), r * jnp.exp(c_incl - c_mid)], axis=0)
    right = jnp.concatenate([bvec * e_neg, kd * e_neg], axis=0)
    m = _mm_nt(left, right)
    l_b = jnp.where(strict, m[:t, :t], 0.0)
    l_k = jnp.where(strict, m[:t, t:], 0.0)
    m_br = jnp.where(incl, m[t:, :t], 0.0)
    m_kr = jnp.where(incl, m[t:, t:], 0.0)

    ainv = _unit_lower_inverse(l_b, row, col)
    kt = kap * jnp.exp(c_excl)
    rt = r * jnp.exp(c_incl)
    wk = _mm(ainv, kt)
    u0 = _mm(ainv, _mm(l_k, v))
    rq = rt - _mm(m_br, wk)
    y0 = _mm(m_kr, v) - _mm(m_br, u0)
    e_end = jnp.exp(c_end - c_incl)
    b_hat = bvec * e_end
    k_hat = kd * e_end
    hd = lw.shape[1]
    eye_k = (lax.broadcasted_iota(jnp.int32, (hd, hd), 0) == lax.broadcasted_iota(jnp.int32, (hd, hd), 1))
    g = jnp.where(eye_k, jnp.exp(c_end), 0.0) - _mm_tn(wk, b_hat)
    c0 = _mm_tn(v, k_hat) - _mm_tn(u0, b_hat)

    s0 = s_ref[...]
    y_ref[0, 0] = _mm_nt(rq, s0) + y0
    s_ref[...] = _mm(s0, g) + c0


def rwkv_scan(lw, kd, a, r, v, kk):
    n, h, length, hd = lw.shape
    t = RWKV_CHUNK
    spec = pl.BlockSpec((1, 1, t, hd), lambda ni, hi, c: (ni, hi, c, 0))
    return pl.pallas_call(
        _rwkv_kernel,
        grid=(n, h, length // t),
        in_specs=[spec] * 6,
        out_specs=spec,
        out_shape=jax.ShapeDtypeStruct((n, h, length, hd), F32),
        scratch_shapes=[pltpu.VMEM((hd, hd), F32)],
        compiler_params=_cparams("parallel", "parallel", "arbitrary"),
        name="rwkv_scan",
    )(lw, kd, a, r, v, kk)


def _heads(t, n):
    return t.reshape(t.shape[:-1] + (n, HEAD_DIM))


def _to_hm(t, n, dtype=BF16):
    return jnp.swapaxes(_heads(t, n), 1, 2).astype(dtype)


def _from_hm(t):
    b, n, length, hd = t.shape
    return jnp.swapaxes(t, 1, 2).reshape(b, length, n * hd)


def _rms_heads(t, gain):
    t = t * lax.rsqrt(jnp.mean(t * t, axis=-1, keepdims=True) + RMS_EPS)
    return t * gain


def _rope(t, cos, sin):
    c = cos[None, :, None, :]
    s = sin[None, :, None, :]
    t1, t2 = t[..., :ROPE_HALF], t[..., ROPE_HALF:]
    return jnp.concatenate([t1 * c - t2 * s, t1 * s + t2 * c], axis=-1)


def _rope_tables(n_tokens):
    t = jnp.arange(n_tokens, dtype=jnp.int32)
    row = (t // GRID_W).astype(F32)
    col = (t % GRID_W).astype(F32)
    inv = ROPE_THETA ** (-jnp.arange(ROPE_FREQS, dtype=F32) / ROPE_FREQS)
    ang = jnp.concatenate([row[:, None] * inv, col[:, None] * inv], axis=-1)
    return jnp.cos(ang), jnp.sin(ang)


def _even_mixer(z, zc, q_gain, k_gain, sink, cos, sin, need_ctx):
    qa, ka, va, qb, kb, vb = _split(z, EVEN_WIDTHS)
    qca, kca, vca, qcb, kcb, vcb = _split(zc, EVEN_WIDTHS)
    qa = _rope(_rms_heads(_heads(qa, A_HEADS), q_gain), cos, sin) * SCALE
    ka = _rope(_rms_heads(_heads(ka, A_KV_HEADS), k_gain), cos, sin)
    kca = _rms_heads(_heads(kca, A_KV_HEADS), k_gain)
    qb = _rope(_heads(qb, B_HEADS), cos, sin) * SCALE
    kb = _rope(_heads(kb, B_KV_HEADS), cos, sin)

    def hm(t):
        return jnp.swapaxes(t, 1, 2).astype(BF16)

    kca_h, vca_h = hm(kca), _to_hm(vca, A_KV_HEADS)
    kcb_h, vcb_h = _to_hm(kcb, B_KV_HEADS), _to_hm(vcb, B_KV_HEADS)
    keys = jnp.concatenate([hm(ka), kca_h], axis=2)
    vals = jnp.concatenate([_to_hm(va, A_KV_HEADS), vca_h], axis=2)
    y_a = global_attention(hm(qa), keys, vals)
    y_b = window_attention(hm(qb), hm(kb), _to_hm(vb, B_KV_HEADS), kcb_h, vcb_h, sink)
    y = (_from_hm(y_a), _from_hm(y_b))
    if not need_ctx:
        return y, None
    qca = _rms_heads(_heads(qca, A_HEADS), q_gain) * SCALE
    yc_a = ctx_attention(hm(qca), kca_h, vca_h, jnp.full((A_HEADS,), NEG_INF, F32))
    yc_b = ctx_attention(_to_hm(qcb * SCALE, B_HEADS), kcb_h, vcb_h, sink)
    return y, (_from_hm(yc_a), _from_hm(yc_b))


def _shift_mix(z, mu):
    zp = jnp.pad(z, ((0, 0), (1, 1), (0, 0)))
    return z + (0.5 * (zp[:, :-2] + zp[:, 2:]) - z) * mu


def _rwkv_inputs(zd, w0, w2, a0, a2, g2, k_k, k_a):
    r, k, v, wd, ad, gd = _split(zd, (D_W, D_W, D_W, DECAY_LORA, ICLR_LORA, GATE_LORA))
    g = jax.nn.sigmoid(gd) @ g2
    kk = _heads(k * k_k, D_HEADS)
    kk = kk / jnp.maximum(jnp.sqrt(jnp.sum(kk * kk, axis=-1, keepdims=True)), 1e-12)
    logw, kd, av = [], [], []
    for d in range(2):
        w = -jax.nn.softplus(-(w0[d] + jnp.tanh(wd) @ w2[d])) - 0.5
        a = jax.nn.sigmoid(a0[d] + ad @ a2[d])
        logw.append(-jnp.exp(w))
        kd.append(k * (1 + (a - 1) * k_a))
        av.append(a)
    return {"r": r, "v": v, "kk": kk.reshape(k.shape), "g": g, "logw": logw, "k": kd, "a": av}


def _rwkv_readout(y, p, r_k, ln_w, ln_b):
    b, length, _ = y.shape
    yh = _heads(y, D_HEADS)
    mean = jnp.mean(yh, axis=-1, keepdims=True)
    var = jnp.mean(jnp.square(yh - mean), axis=-1, keepdims=True)
    yn = ((yh - mean) * lax.rsqrt(var + GN_EPS)).reshape(b, length, D_W)
    yn = yn * ln_w + ln_b
    ksum = p["k"][0] + p["k"][1]
    bonus = jnp.sum(_heads(p["r"] * ksum, D_HEADS) * r_k, axis=-1, keepdims=True) * _heads(p["v"], D_HEADS)
    return (yn + bonus.reshape(b, length, D_W)) * p["g"]


def _odd_mixer(z, zc, rpb, mu, w0, w2, a0, a2, g2, k_k, k_a, r_k, ln_w, ln_b, need_ctx):
    q, k, v, zd = _split(z, (C_W, C_W, C_W, D_SHIFT_W))
    qc, kc, vc, zdc = _split(zc, (C_W, C_W, C_W, D_SHIFT_W))
    s = z.shape[1]
    n_ctx = zc.shape[1]
    kc_h, vc_h = _to_hm(kc, C_HEADS), _to_hm(vc, C_HEADS)
    bias = na_bias_tables(rpb, s // GRID_W)
    y_c = neighborhood_attention(_to_hm(q * SCALE, C_HEADS), _to_hm(k, C_HEADS), _to_hm(v, C_HEADS),
                                 kc_h, vc_h, bias)

    lat = _rwkv_inputs(_shift_mix(zd, mu), w0, w2, a0, a2, g2, k_k, k_a)
    cx = _rwkv_inputs(_shift_mix(zdc, mu), w0, w2, a0, a2, g2, k_k, k_a)

    def seq(name, d=None):
        lt, ct = (lat[name], cx[name]) if d is None else (lat[name][d], cx[name][d])
        fwd = jnp.concatenate([ct, lt], axis=1)
        bwd = jnp.concatenate([ct[:, ::-1], lt[:, ::-1]], axis=1)
        return fwd, bwd

    def both(name):
        if name in ("logw", "k", "a"):
            f, _ = seq(name, 0)
            _, bw = seq(name, 1)
        else:
            f, bw = seq(name)
        return _to_hm(jnp.concatenate([f, bw], axis=0), D_HEADS, F32)

    y = rwkv_scan(both("logw"), both("k"), both("a"), both("r"), both("v"), both("kk"))
    y = _from_hm(y)
    bsz = z.shape[0]
    y_f, y_b = y[:bsz], y[bsz:]
    y_lat = y_f[:, n_ctx:] + y_b[:, n_ctx:][:, ::-1]
    y_d = _rwkv_readout(y_lat, lat, r_k, ln_w, ln_b)
    out = (_from_hm(y_c), y_d.astype(BF16))
    if not need_ctx:
        return out, None
    yc_c = ctx_attention(_to_hm(qc * SCALE, C_HEADS), kc_h, vc_h, jnp.full((C_HEADS,), NEG_INF, F32))
    y_cx = y_f[:, :n_ctx] + y_b[:, :n_ctx][:, ::-1]
    yc_d = _rwkv_readout(y_cx, cx, r_k, ln_w, ln_b)
    return out, (_from_hm(yc_c), yc_d.astype(BF16))


def kernel(x, c, ctx, c_ctx, w_mod, b_mod, norm_mix, norm_ffn, w_in_even, w_out_even, q_norm_a, k_norm_a,
           sink_b, w_in_odd, w_out_odd, rpb_c, shift_mu, decay_w0, decay_w2, iclr_a0, iclr_a2, gate_g2,
           k_k, k_a, r_k, ln_x_w, ln_x_b, w_ffn_in, w_ffn_out, norm_out):
    bsz, s, d = x.shape
    cos, sin = _rope_tables(s)
    act = jnp.zeros((8, d), F32).at[:bsz].set(jax.nn.silu(c)).at[bsz].set(jax.nn.silu(c_ctx))
    mods = modulation(act.astype(BF16), w_mod, b_mod)
    w_in_even = w_in_even.astype(BF16)
    w_out_even = w_out_even.astype(BF16)
    w_in_odd = w_in_odd.astype(BF16)
    w_out_odd = w_out_odd.astype(BF16)
    w_ffn_in = w_ffn_in.astype(BF16)
    w_ffn_out = w_ffn_out.astype(BF16)
    for layer in range(DEPTH):
        need_ctx = layer < DEPTH - 1
        mod = mods[layer, :bsz][:, None, :]
        modc = jnp.broadcast_to(mods[layer, bsz][None, None, :], (bsz, 1, 6 * d))
        sh1, sc1, g1, sh2, sc2, g2 = jnp.split(mod, 6, axis=-1)
        csh1, csc1, cg1, csh2, csc2, cg2 = jnp.split(modc, 6, axis=-1)
        gain_mix = norm_mix[layer][None, :]
        gain_ffn = norm_ffn[layer][None, :]
        i = layer // 2
        if layer % 2 == 0:
            w_in, w_out = w_in_even[i], w_out_even[i]
            z = norm_proj(x, gain_mix, sh1, sc1, w_in)
            zc = norm_proj(ctx, gain_mix, csh1, csc1, w_in)
            y, yc = _even_mixer(z, zc, q_norm_a[i], k_norm_a[i], sink_b[i], cos, sin, need_ctx)
        else:
            w_in, w_out = w_in_odd[i], w_out_odd[i]
            z = norm_proj(x, gain_mix, sh1, sc1, w_in)
            zc = norm_proj(ctx, gain_mix, csh1, csc1, w_in)
            y, yc = _odd_mixer(z, zc, rpb_c[i], shift_mu[i], decay_w0[i], decay_w2[i], iclr_a0[i],
                               iclr_a2[i], gate_g2[i], k_k[i], k_a[i], r_k[i], ln_x_w[i], ln_x_b[i],
                               need_ctx)
        half = y[0].shape[-1]
        x = out_proj(x, g1, y[0], y[1], w_out[:half], w_out[half:])
        x = ffn(x, gain_ffn, sh2, sc2, g2, w_ffn_in[layer], w_ffn_out[layer])
        if need_ctx:
            ctx = out_proj(ctx, cg1, yc[0], yc[1], w_out[:half], w_out[half:])
            ctx = ffn(ctx, gain_ffn, csh2, csc2, cg2, w_ffn_in[layer], w_ffn_out[layer])
    return final_norm(x, norm_out[None, :])
```

```python
import functools

import jax
import jax.numpy as jnp
import numpy as np
from jax import lax
from jax.experimental import pallas as pl
from jax.experimental.pallas import tpu as pltpu

F32 = jnp.float32
BF16 = jnp.bfloat16

D_MODEL = 1024
DEPTH = 4
GRID_W = 64
HEAD_DIM = 64
SCALE = HEAD_DIM ** -0.5
ROPE_HALF = HEAD_DIM // 2
ROPE_FREQS = HEAD_DIM // 4
ROPE_THETA = 10000.0
A_HEADS = 8
A_KV_HEADS = 2
B_HEADS = 8
B_KV_HEADS = 2
WINDOW = 128
C_HEADS = 8
NA_ROWS = 8
NA_COLS = 16
D_HEADS = 8
DECAY_LORA = 64
ICLR_LORA = 64
GATE_LORA = 128
GN_EPS = 64e-5
RMS_EPS = 1e-6
NEG_INF = -1e30
FFN_HIDDEN = -(-8 * D_MODEL // (3 * 256)) * 256

A_Q = A_HEADS * HEAD_DIM
A_KV = A_KV_HEADS * HEAD_DIM
B_Q = B_HEADS * HEAD_DIM
B_KV = B_KV_HEADS * HEAD_DIM
C_W = C_HEADS * HEAD_DIM
D_W = D_HEADS * HEAD_DIM
EVEN_WIDTHS = (A_Q, A_KV, A_KV, B_Q, B_KV, B_KV)
D_SHIFT_W = 3 * D_W + DECAY_LORA + ICLR_LORA + GATE_LORA

VMEM_LIMIT_BYTES = 56 * 1024 * 1024
RWKV_CHUNK = HEAD_DIM
RWKV_INV_BLOCK = 16
RWKV_LANES = 256
NA_TILE_ROWS = 8
NA_KEY_ROWS = NA_TILE_ROWS + NA_ROWS - 1


def _cparams(*sem):
    return pltpu.CompilerParams(dimension_semantics=sem, vmem_limit_bytes=VMEM_LIMIT_BYTES)


def _split(t, widths):
    return jnp.split(t, [int(o) for o in np.cumsum(widths)[:-1]], axis=-1)


def _pick_tile(n, target):
    t = min(n, target)
    while n % t:
        t //= 2
    return t


def _mod_kernel(a_ref, w_ref, b_ref, o_ref):
    w = w_ref[0].astype(BF16)
    o_ref[0] = jnp.dot(a_ref[...], w, preferred_element_type=F32) + b_ref[0]


def modulation(act, w_mod, b_mod):
    depth, d, n = w_mod.shape
    tn = 1536
    return pl.pallas_call(
        _mod_kernel,
        grid=(depth, n // tn),
        in_specs=[
            pl.BlockSpec((8, d), lambda l, j: (0, 0)),
            pl.BlockSpec((1, d, tn), lambda l, j: (l, 0, j)),
            pl.BlockSpec((1, 1, tn), lambda l, j: (l, 0, j)),
        ],
        out_specs=pl.BlockSpec((1, 8, tn), lambda l, j: (l, 0, j)),
        out_shape=jax.ShapeDtypeStruct((depth, 8, n), F32),
        compiler_params=_cparams("parallel", "parallel"),
        name="modulation",
    )(act, w_mod, b_mod.reshape(depth, 1, n))


def _norm_mod(x, gain, shift, scale):
    ms = jnp.mean(x * x, axis=-1, keepdims=True)
    h = x * lax.rsqrt(ms + RMS_EPS) * gain
    return h * (1.0 + scale) + shift


def _head_rms(z):
    return lax.rsqrt(jnp.mean(z * z, axis=-1, keepdims=True) + RMS_EPS)


_EV_QA, _EV_KA, _EV_VA, _EV_QB, _EV_KB, _EV_VB = (int(o) for o in np.cumsum((0,) + EVEN_WIDTHS)[:-1])
_EV_QAR = sum(EVEN_WIDTHS)
_EV_KAR = _EV_QAR + A_Q
_EV_QBR = _EV_KAR + A_KV
_EV_KBR = _EV_QBR + B_Q
EVEN_EXT = _EV_KBR + B_KV


def _even_proj_kernel(x_ref, gain_ref, sh_ref, sc_ref, w_ref, cos_ref, sin_ref, qg_ref, kg_ref,
                      qa_ref, ka_ref, va_ref, qb_ref, kb_ref, vb_ref):
    h = _norm_mod(x_ref[0], gain_ref[...], sh_ref[0], sc_ref[0])
    z = jnp.dot(h.astype(BF16), w_ref[...], preferred_element_type=F32)
    cos = cos_ref[...]
    sin = sin_ref[...]
    qg = qg_ref[...]
    kg = kg_ref[...]

    def sl(off, i):
        return z[:, off + i * HEAD_DIM: off + (i + 1) * HEAD_DIM]

    for i in range(A_HEADS):
        zq = sl(_EV_QA, i)
        qa = _head_rms(zq) * (zq * qg * cos + sl(_EV_QAR, i) * sin)
        qa_ref[0, i] = (qa * SCALE).astype(BF16)
    for i in range(B_HEADS):
        qb_ref[0, i] = ((sl(_EV_QB, i) * cos + sl(_EV_QBR, i) * sin) * SCALE).astype(BF16)
    for i in range(A_KV_HEADS):
        zk = sl(_EV_KA, i)
        ka_ref[0, i] = (_head_rms(zk) * (zk * kg * cos + sl(_EV_KAR, i) * sin)).astype(BF16)
        va_ref[0, i] = sl(_EV_VA, i).astype(BF16)
    for i in range(B_KV_HEADS):
        kb_ref[0, i] = (sl(_EV_KB, i) * cos + sl(_EV_KBR, i) * sin).astype(BF16)
        vb_ref[0, i] = sl(_EV_VB, i).astype(BF16)


def _rot_cols(w):
    d = w.shape[0]
    wh = w.reshape(d, -1, 2, ROPE_HALF)
    return jnp.stack([-wh[:, :, 1], wh[:, :, 0]], axis=2).reshape(w.shape)


def even_weight(w_in, q_gain, k_gain):
    qa, ka, va, qb, kb, vb = _split(w_in, EVEN_WIDTHS)
    qar = _rot_cols(qa * jnp.tile(q_gain, A_HEADS))
    kar = _rot_cols(ka * jnp.tile(k_gain, A_KV_HEADS))
    return jnp.concatenate([w_in, qar, kar, _rot_cols(qb), _rot_cols(kb)], axis=1).astype(BF16)


def even_proj(x, gain, shift, scale, w_ext, cos2, sin2, q_gain, k_gain):
    b, s, d = x.shape
    tm = _pick_tile(s, 512)
    vec = pl.BlockSpec((1, 1, d), lambda bi, i: (bi, 0, 0))
    tab = pl.BlockSpec((tm, HEAD_DIM), lambda bi, i: (i, 0))
    hv = pl.BlockSpec((1, HEAD_DIM), lambda bi, i: (0, 0))

    def hm(n):
        return pl.BlockSpec((1, n, tm, HEAD_DIM), lambda bi, i: (bi, 0, i, 0))

    def shape(n):
        return jax.ShapeDtypeStruct((b, n, s, HEAD_DIM), BF16)

    return pl.pallas_call(
        _even_proj_kernel,
        grid=(b, s // tm),
        in_specs=[
            pl.BlockSpec((1, tm, d), lambda bi, i: (bi, i, 0)),
            pl.BlockSpec((1, d), lambda bi, i: (0, 0)),
            vec, vec,
            pl.BlockSpec((d, EVEN_EXT), lambda bi, i: (0, 0)),
            tab, tab, hv, hv,
        ],
        out_specs=[hm(A_HEADS), hm(A_KV_HEADS), hm(A_KV_HEADS), hm(B_HEADS), hm(B_KV_HEADS), hm(B_KV_HEADS)],
        out_shape=[shape(A_HEADS), shape(A_KV_HEADS), shape(A_KV_HEADS),
                   shape(B_HEADS), shape(B_KV_HEADS), shape(B_KV_HEADS)],
        compiler_params=_cparams("parallel", "parallel"),
        name="even_proj",
    )(x, gain, shift, scale, w_ext, cos2, sin2, q_gain, k_gain)


def _odd_proj_kernel(x_ref, gain_ref, sh_ref, sc_ref, w_ref, q_ref, k_ref, v_ref, zd_ref):
    h = _norm_mod(x_ref[0], gain_ref[...], sh_ref[0], sc_ref[0])
    z = jnp.dot(h.astype(BF16), w_ref[...], preferred_element_type=F32)
    for i in range(C_HEADS):
        lo = i * HEAD_DIM
        q_ref[0, i] = (z[:, lo:lo + HEAD_DIM] * SCALE).astype(BF16)
        k_ref[0, i] = z[:, C_W + lo:C_W + lo + HEAD_DIM].astype(BF16)
        v_ref[0, i] = z[:, 2 * C_W + lo:2 * C_W + lo + HEAD_DIM].astype(BF16)
    zd_ref[0] = z[:, 3 * C_W:]


def odd_proj(x, gain, shift, scale, w):
    b, s, d = x.shape
    n = w.shape[1]
    tm = _pick_tile(s, 512)
    vec = pl.BlockSpec((1, 1, d), lambda bi, i: (bi, 0, 0))
    hm = pl.BlockSpec((1, C_HEADS, tm, HEAD_DIM), lambda bi, i: (bi, 0, i, 0))
    hshape = jax.ShapeDtypeStruct((b, C_HEADS, s, HEAD_DIM), BF16)
    return pl.pallas_call(
        _odd_proj_kernel,
        grid=(b, s // tm),
        in_specs=[
            pl.BlockSpec((1, tm, d), lambda bi, i: (bi, i, 0)),
            pl.BlockSpec((1, d), lambda bi, i: (0, 0)),
            vec, vec,
            pl.BlockSpec((d, n), lambda bi, i: (0, 0)),
        ],
        out_specs=[hm, hm, hm, pl.BlockSpec((1, tm, D_SHIFT_W), lambda bi, i: (bi, i, 0))],
        out_shape=[hshape, hshape, hshape, jax.ShapeDtypeStruct((b, s, D_SHIFT_W), F32)],
        compiler_params=_cparams("parallel", "parallel"),
        name="odd_proj",
    )(x, gain, shift, scale, w)


def _out_proj_kernel(x_ref, g_ref, y1_ref, y2_ref, w1_ref, w2_ref, o_ref, *, y2_heads):
    acc = jnp.dot(y1_ref[0, 0], w1_ref[0], preferred_element_type=F32)
    for h in range(1, y1_ref.shape[1]):
        acc = acc + jnp.dot(y1_ref[0, h], w1_ref[h], preferred_element_type=F32)
    if y2_heads:
        for h in range(y2_ref.shape[1]):
            acc = acc + jnp.dot(y2_ref[0, h], w2_ref[h], preferred_element_type=F32)
    else:
        acc = acc + jnp.dot(y2_ref[0], w2_ref[...], preferred_element_type=F32)
    o_ref[0] = x_ref[0] + g_ref[0] * acc


def out_proj(x, gate, y1, y2, w):
    b, s, d = x.shape
    nh = y1.shape[1]
    half = nh * HEAD_DIM
    tm = _pick_tile(s, 1024)
    y2_heads = y2.ndim == 4
    hm = pl.BlockSpec((1, nh, tm, HEAD_DIM), lambda bi, i: (bi, 0, i, 0))
    wh = pl.BlockSpec((nh, HEAD_DIM, d), lambda bi, i: (0, 0, 0))
    w1 = w[:half].reshape(nh, HEAD_DIM, d)
    if y2_heads:
        y2_spec, w2_spec, w2 = hm, wh, w[half:].reshape(nh, HEAD_DIM, d)
    else:
        y2_spec = pl.BlockSpec((1, tm, half), lambda bi, i: (bi, i, 0))
        w2_spec, w2 = pl.BlockSpec((half, d), lambda bi, i: (0, 0)), w[half:]
    return pl.pallas_call(
        functools.partial(_out_proj_kernel, y2_heads=y2_heads),
        grid=(b, s // tm),
        in_specs=[
            pl.BlockSpec((1, tm, d), lambda bi, i: (bi, i, 0)),
            pl.BlockSpec((1, 1, d), lambda bi, i: (bi, 0, 0)),
            hm, y2_spec, wh, w2_spec,
        ],
        out_specs=pl.BlockSpec((1, tm, d), lambda bi, i: (bi, i, 0)),
        out_shape=jax.ShapeDtypeStruct((b, s, d), F32),
        compiler_params=_cparams("parallel", "parallel"),
        name="out_proj",
    )(x, gate, y1, y2, w1, w2)


def _ffn_kernel(x_ref, gain_ref, sh_ref, sc_ref, g_ref, wg_ref, wu_ref, wo_ref, o_ref, h_ref, acc_ref):
    j = pl.program_id(2)

    @pl.when(j == 0)
    def _():
        h_ref[...] = _norm_mod(x_ref[0], gain_ref[...], sh_ref[0], sc_ref[0]).astype(BF16)
        acc_ref[...] = jnp.zeros_like(acc_ref)

    h = h_ref[...]
    gate = jnp.dot(h, wg_ref[...], preferred_element_type=F32)
    up = jnp.dot(h, wu_ref[...], preferred_element_type=F32)
    act = gate * (1.0 / (1.0 + jnp.exp(-gate))) * up
    acc_ref[...] += jnp.dot(act.astype(BF16), wo_ref[...], preferred_element_type=F32)

    @pl.when(j == pl.num_programs(2) - 1)
    def _():
        o_ref[0] = x_ref[0] + g_ref[0] * acc_ref[...]


def ffn(x, gain, shift, scale, gate, w_in, w_out):
    b, s, d = x.shape
    f = w_out.shape[0]
    tm = _pick_tile(s, 512)
    th = 1408 if f % 1408 == 0 else _pick_tile(f, 256)
    nj = f // th
    vec = pl.BlockSpec((1, 1, d), lambda bi, i, j: (bi, 0, 0))
    return pl.pallas_call(
        _ffn_kernel,
        grid=(b, s // tm, nj),
        in_specs=[
            pl.BlockSpec((1, tm, d), lambda bi, i, j: (bi, i, 0)),
            pl.BlockSpec((1, d), lambda bi, i, j: (0, 0)),
            vec, vec, vec,
            pl.BlockSpec((d, th), lambda bi, i, j: (0, j)),
            pl.BlockSpec((d, th), lambda bi, i, j: (0, j + nj)),
            pl.BlockSpec((th, d), lambda bi, i, j: (j, 0)),
        ],
        out_specs=pl.BlockSpec((1, tm, d), lambda bi, i, j: (bi, i, 0)),
        out_shape=jax.ShapeDtypeStruct((b, s, d), F32),
        scratch_shapes=[pltpu.VMEM((tm, d), BF16), pltpu.VMEM((tm, d), F32)],
        compiler_params=_cparams("parallel", "parallel", "arbitrary"),
        name="ffn",
    )(x, gain, shift, scale, gate, w_in, w_in, w_out)


def _rms_kernel(x_ref, gain_ref, o_ref):
    x = x_ref[0]
    ms = jnp.mean(x * x, axis=-1, keepdims=True)
    o_ref[0] = x * lax.rsqrt(ms + RMS_EPS) * gain_ref[...]


def final_norm(x, gain):
    b, s, d = x.shape
    tm = _pick_tile(s, 1024)
    return pl.pallas_call(
        _rms_kernel,
        grid=(b, s // tm),
        in_specs=[pl.BlockSpec((1, tm, d), lambda bi, i: (bi, i, 0)),
                  pl.BlockSpec((1, d), lambda bi, i: (0, 0))],
        out_specs=pl.BlockSpec((1, tm, d), lambda bi, i: (bi, i, 0)),
        out_shape=jax.ShapeDtypeStruct((b, s, d), F32),
        compiler_params=_cparams("parallel", "parallel"),
        name="final_norm",
    )(x, gain)


def _nt_dot(a, b):
    return lax.dot_general(a, b, (((1,), (1,)), ((), ())), preferred_element_type=F32)


def _flash_kernel(q_ref, k_ref, v_ref, kx_ref, vx_ref, o_ref, m_ref, l_ref, acc_ref):
    kv = pl.program_id(3)
    g, tq, hd = q_ref.shape[1:]

    @pl.when(kv == 0)
    def _():
        m_ref[...] = jnp.full_like(m_ref, -jnp.inf)
        l_ref[...] = jnp.zeros_like(l_ref)
        acc_ref[...] = jnp.zeros_like(acc_ref)

    q = q_ref[0].reshape(g * tq, hd)

    def update(k, v):
        s = _nt_dot(q, k)
        m_prev = m_ref[...]
        m_new = jnp.maximum(m_prev, jnp.max(s, axis=-1, keepdims=True))
        alpha = jnp.exp(m_prev - m_new)
        p = jnp.exp(s - m_new)
        l_ref[...] = alpha * l_ref[...] + jnp.sum(p, axis=-1, keepdims=True)
        acc_ref[...] = alpha * acc_ref[...] + jnp.dot(p.astype(BF16), v, preferred_element_type=F32)
        m_ref[...] = m_new

    update(k_ref[0, 0], v_ref[0, 0])

    @pl.when(kv == pl.num_programs(3) - 1)
    def _():
        update(kx_ref[0, 0], vx_ref[0, 0])
        o_ref[0] = (acc_ref[...] / l_ref[...]).reshape(g, tq, hd).astype(o_ref.dtype)


def global_attention(q, k, v, kx, vx):
    b, h, s, hd = q.shape
    hkv = k.shape[1]
    n_ctx = kx.shape[2]
    g = h // hkv
    tq = _pick_tile(s, 256)
    tk = _pick_tile(s, 1024)
    kv_spec = pl.BlockSpec((1, 1, tk, hd), lambda bi, gi, i, j: (bi, gi, j, 0))
    cx_spec = pl.BlockSpec((1, 1, n_ctx, hd), lambda bi, gi, i, j: (bi, gi, 0, 0))
    q_spec = pl.BlockSpec((1, g, tq, hd), lambda bi, gi, i, j: (bi, gi, i, 0))
    return pl.pallas_call(
        _flash_kernel,
        grid=(b, hkv, s // tq, s // tk),
        in_specs=[q_spec, kv_spec, kv_spec, cx_spec, cx_spec],
        out_specs=q_spec,
        out_shape=jax.ShapeDtypeStruct((b, h, s, hd), BF16),
        scratch_shapes=[pltpu.VMEM((g * tq, 1), F32), pltpu.VMEM((g * tq, 1), F32),
                        pltpu.VMEM((g * tq, hd), F32)],
        compiler_params=_cparams("parallel", "parallel", "parallel", "arbitrary"),
        name="global_attention",
    )(q, k, v, kx, vx)


def _window_kernel(sink_ref, q_ref, kp_ref, kc_ref, kn_ref, vp_ref, vc_ref, vn_ref, kx_ref, vx_ref, o_ref,
                   *, seq):
    gi = pl.program_id(1)
    i = pl.program_id(2)
    g, tq, hd = q_ref.shape[1:]
    w = kp_ref.shape[2]
    q = q_ref[0].reshape(g * tq, hd)
    k = jnp.concatenate([kp_ref[0, 0], kc_ref[0, 0], kn_ref[0, 0]], axis=0)
    v = jnp.concatenate([vp_ref[0, 0], vc_ref[0, 0], vn_ref[0, 0]], axis=0)
    span = tq + 2 * w
    s_loc = _nt_dot(q, k)
    row = lax.broadcasted_iota(jnp.int32, (g * tq, span), 0)
    col = lax.broadcasted_iota(jnp.int32, (g * tq, span), 1)
    qpos = i * tq + row % tq
    kpos = i * tq - w + col
    valid = (jnp.abs(kpos - qpos) <= w) & (kpos >= 0) & (kpos < seq)
    s_loc = jnp.where(valid, s_loc, NEG_INF)
    s_ctx = _nt_dot(q, kx_ref[0, 0])
    hrow = lax.broadcasted_iota(jnp.int32, (g * tq, 1), 0) // tq
    sink = jnp.zeros((g * tq, 1), F32)
    for hh in range(g):
        sink = jnp.where(hrow == hh, sink_ref[gi * g + hh], sink)
    m = jnp.maximum(jnp.maximum(jnp.max(s_loc, axis=-1, keepdims=True),
                                jnp.max(s_ctx, axis=-1, keepdims=True)), sink)
    p_loc = jnp.exp(s_loc - m)
    p_ctx = jnp.exp(s_ctx - m)
    l = jnp.sum(p_loc, axis=-1, keepdims=True) + jnp.sum(p_ctx, axis=-1, keepdims=True) + jnp.exp(sink - m)
    o = jnp.dot(p_loc.astype(BF16), v, preferred_element_type=F32)
    o = o + jnp.dot(p_ctx.astype(BF16), vx_ref[0, 0], preferred_element_type=F32)
    o_ref[0] = (o / l).reshape(g, tq, hd).astype(o_ref.dtype)


def window_attention(q, k, v, kx, vx, sink):
    b, h, s, hd = q.shape
    hkv = k.shape[1]
    g = h // hkv
    n_ctx = kx.shape[2]
    tq = _pick_tile(s, 256)
    r = tq // WINDOW
    nwb = s // WINDOW
    prev = pl.BlockSpec((1, 1, WINDOW, hd), lambda bi, gi, i: (bi, gi, jnp.maximum(i * r - 1, 0), 0))
    cur = pl.BlockSpec((1, 1, tq, hd), lambda bi, gi, i: (bi, gi, i, 0))
    nxt = pl.BlockSpec((1, 1, WINDOW, hd), lambda bi, gi, i: (bi, gi, jnp.minimum((i + 1) * r, nwb - 1), 0))
    cx = pl.BlockSpec((1, 1, n_ctx, hd), lambda bi, gi, i: (bi, gi, 0, 0))
    return pl.pallas_call(
        functools.partial(_window_kernel, seq=s),
        grid=(b, hkv, s // tq),
        in_specs=[
            pl.BlockSpec(memory_space=pltpu.SMEM),
            pl.BlockSpec((1, g, tq, hd), lambda bi, gi, i: (bi, gi, i, 0)),
            prev, cur, nxt, prev, cur, nxt, cx, cx,
        ],
        out_specs=pl.BlockSpec((1, g, tq, hd), lambda bi, gi, i: (bi, gi, i, 0)),
        out_shape=jax.ShapeDtypeStruct((b, h, s, hd), BF16),
        compiler_params=_cparams("parallel", "parallel", "parallel"),
        name="window_attention",
    )(sink, q, k, k, k, v, v, v, kx, vx)


def _ctx_attn_kernel(sink_ref, q_ref, k_ref, v_ref, o_ref):
    hi = pl.program_id(1)
    s = _nt_dot(q_ref[0, 0], k_ref[0, 0])
    sink = sink_ref[hi]
    m = jnp.maximum(jnp.max(s, axis=-1, keepdims=True), sink)
    p = jnp.exp(s - m)
    l = jnp.sum(p, axis=-1, keepdims=True) + jnp.exp(sink - m)
    o = jnp.dot(p.astype(BF16), v_ref[0, 0], preferred_element_type=F32)
    o_ref[0, 0] = (o / l).astype(o_ref.dtype)


def ctx_attention(q, k, v, sink):
    b, h, n, hd = q.shape
    g = h // k.shape[1]
    kv = pl.BlockSpec((1, 1, n, hd), lambda bi, hi: (bi, hi // g, 0, 0))
    qs = pl.BlockSpec((1, 1, n, hd), lambda bi, hi: (bi, hi, 0, 0))
    return pl.pallas_call(
        _ctx_attn_kernel,
        grid=(b, h),
        in_specs=[pl.BlockSpec(memory_space=pltpu.SMEM), qs, kv, kv],
        out_specs=qs,
        out_shape=jax.ShapeDtypeStruct((b, h, n, hd), BF16),
        compiler_params=_cparams("parallel", "parallel"),
        name="ctx_attention",
    )(sink, q, k, v)


def _na_kernel(q_ref, k_ref, v_ref, kx_ref, vx_ref, bias_ref, o_ref, *, rows):
    t = pl.program_id(2)
    nkeys = NA_KEY_ROWS * GRID_W
    kstart = jnp.clip(t * NA_TILE_ROWS - NA_ROWS // 2, 0, rows - NA_KEY_ROWS)
    off = pl.multiple_of(kstart * GRID_W, GRID_W)
    q = q_ref[0, 0]
    kw = k_ref[0, 0, pl.ds(off, nkeys), :]
    vw = v_ref[0, 0, pl.ds(off, nkeys), :]
    s_loc = _nt_dot(q, kw) + bias_ref[0, 0]
    s_ctx = _nt_dot(q, kx_ref[0, 0])
    m = jnp.maximum(jnp.max(s_loc, axis=-1, keepdims=True), jnp.max(s_ctx, axis=-1, keepdims=True))
    p_loc = jnp.exp(s_loc - m)
    p_ctx = jnp.exp(s_ctx - m)
    l = jnp.sum(p_loc, axis=-1, keepdims=True) + jnp.sum(p_ctx, axis=-1, keepdims=True)
    o = jnp.dot(p_loc.astype(BF16), vw, preferred_element_type=F32)
    o = o + jnp.dot(p_ctx.astype(BF16), vx_ref[0, 0], preferred_element_type=F32)
    o_ref[0, 0] = (o / l).astype(o_ref.dtype)


def na_bias_tables(rpb, rows):
    nh = rpb.shape[0]
    i = np.arange(NA_TILE_ROWS)
    j = np.arange(NA_KEY_ROWS)
    qc = np.arange(GRID_W)
    kc = np.arange(GRID_W)
    cs = np.clip(qc - NA_COLS // 2, 0, GRID_W - NA_COLS)
    col_valid = (kc[None, :] >= cs[:, None]) & (kc[None, :] < cs[:, None] + NA_COLS)
    dcol = np.clip(kc[None, :] - qc[:, None] + NA_COLS - 1, 0, 2 * NA_COLS - 2)
    per_drow = jnp.where(col_valid[None, None], rpb.astype(F32)[:, :, dcol], NEG_INF)
    masked = jnp.full((nh, 1, GRID_W, GRID_W), NEG_INF, F32)
    per_drow = jnp.concatenate([per_drow, masked], axis=1)
    out = []
    for r0 in (0, NA_TILE_ROWS, rows - NA_TILE_ROWS):
        r = r0 + i
        rs = np.clip(r - NA_ROWS // 2, 0, rows - NA_ROWS)
        kstart = int(np.clip(r0 - NA_ROWS // 2, 0, rows - NA_KEY_ROWS))
        krow = kstart + j
        row_valid = (krow[None, :] >= rs[:, None]) & (krow[None, :] < rs[:, None] + NA_ROWS)
        drow = np.where(row_valid, krow[None, :] - r[:, None] + NA_ROWS - 1, 2 * NA_ROWS - 1)
        cols = [per_drow[:, drow[:, jj]] for jj in range(NA_KEY_ROWS)]
        dense = jnp.stack(cols, axis=3)
        out.append(dense.reshape(nh, NA_TILE_ROWS * GRID_W, NA_KEY_ROWS * GRID_W))
    return jnp.stack(out)


def neighborhood_attention(q, k, v, kx, vx, bias):
    b, h, s, hd = q.shape
    rows = s // GRID_W
    nt = rows // NA_TILE_ROWS
    tq = NA_TILE_ROWS * GRID_W
    nkeys = NA_KEY_ROWS * GRID_W
    n_ctx = kx.shape[2]
    full = pl.BlockSpec((1, 1, s, hd), lambda bi, hi, t: (bi, hi, 0, 0))
    cx = pl.BlockSpec((1, 1, n_ctx, hd), lambda bi, hi, t: (bi, hi, 0, 0))

    def case(t):
        return jnp.where(t == 0, 0, jnp.where(t == nt - 1, 2, 1))

    return pl.pallas_call(
        functools.partial(_na_kernel, rows=rows),
        grid=(b, h, nt),
        in_specs=[
            pl.BlockSpec((1, 1, tq, hd), lambda bi, hi, t: (bi, hi, t, 0)),
            full, full, cx, cx,
            pl.BlockSpec((1, 1, tq, nkeys), lambda bi, hi, t: (case(t), hi, 0, 0)),
        ],
        out_specs=pl.BlockSpec((1, 1, tq, hd), lambda bi, hi, t: (bi, hi, t, 0)),
        out_shape=jax.ShapeDtypeStruct((b, h, s, hd), BF16),
        compiler_params=_cparams("parallel", "parallel", "arbitrary"),
        name="neighborhood_attention",
    )(q, k, v, kx, vx, bias)


def _mm(a, b, exact=False):
    if exact:
        return jnp.dot(a, b, preferred_element_type=F32, precision=lax.Precision.HIGHEST)
    return jnp.dot(a.astype(BF16), b.astype(BF16), preferred_element_type=F32)


def _mm_nt(a, b):
    return lax.dot_general(a.astype(BF16), b.astype(BF16), (((1,), (1,)), ((), ())),
                           preferred_element_type=F32)


def _mm_tn(a, b):
    return lax.dot_general(a.astype(BF16), b.astype(BF16), (((0,), (0,)), ((), ())),
                           preferred_element_type=F32)


def _rwkv_chain(lw, kd, a, r, v, kap, st, reverse):
    t, w = lw.shape
    nh = w // HEAD_DIM
    row = lax.broadcasted_iota(jnp.int32, (t, w), 0)
    lane = lax.broadcasted_iota(jnp.int32, (t, w), 1)
    colj = lane % t
    lane_head = lax.broadcasted_iota(jnp.int32, (1, w), 1) // HEAD_DIM
    if reverse:
        strict, incl = colj > row, colj >= row
    else:
        strict, incl = colj < row, colj <= row

    def bd(y):
        return jnp.concatenate([jnp.where(lane_head == h, y, 0.0) for h in range(nh)], axis=0)

    def bdmm(x, y):
        return _mm(x, bd(y))

    tri = incl[:, :t].astype(F32)
    c_incl = _mm(tri, lw, exact=True)
    c_excl = c_incl - lw
    c_mid = c_incl[t // 2:t // 2 + 1, :]
    c_end = c_incl[0:1, :] if reverse else c_incl[t - 1:t, :]
    bvec = kap * a
    e_neg = jnp.exp(c_mid - c_incl)
    left = jnp.concatenate([kap * jnp.exp(c_excl - c_mid), r * jnp.exp(c_incl - c_mid)], axis=0)
    wt = jnp.concatenate([bd(bvec * e_neg), bd(kd * e_neg)], axis=0)
    m = _mm_nt(left, wt)
    l_b = jnp.where(strict, m[:t, :w], 0.0)
    l_k = jnp.where(strict, m[:t, w:], 0.0)
    m_br = jnp.where(incl, m[t:, :w], 0.0)
    m_kr = jnp.where(incl, m[t:, w:], 0.0)

    blk = RWKV_INV_BLOCK
    x = -jnp.where(row // blk == colj // blk, l_b, 0.0)
    inv = jnp.where(row == colj, 1.0, 0.0) + x
    pw = x
    span = 2
    while span < blk:
        pw = bdmm(pw, pw)
        inv = inv + bdmm(inv, pw)
        span *= 2
    while blk < t:
        off = jnp.where((row // (2 * blk) == colj // (2 * blk)) & (row // blk != colj // blk), l_b, 0.0)
        inv = inv - bdmm(bdmm(inv, off), inv)
        blk *= 2

    kt = kap * jnp.exp(c_excl)
    rt = r * jnp.exp(c_incl)
    wk = bdmm(inv, kt)
    u0 = bdmm(inv, bdmm(l_k, v))
    rq = rt - bdmm(m_br, wk)
    y0 = bdmm(m_kr, v) - bdmm(m_br, u0)
    e_end = jnp.exp(c_end - c_incl)
    b_hat = bvec * e_end
    k_hat = kd * e_end
    ri = lax.broadcasted_iota(jnp.int32, (w, w), 0)
    ci = lax.broadcasted_iota(jnp.int32, (w, w), 1)
    same_head = ri // HEAD_DIM == ci // HEAD_DIM
    gt = jnp.where(ri == ci, jnp.exp(c_end), 0.0) - jnp.where(same_head, _mm_tn(b_hat, wk), 0.0)
    c0t = jnp.where(same_head, _mm_tn(k_hat, v) - _mm_tn(b_hat, u0), 0.0)
    y = _mm(rq, st) + y0
    return y, _mm(gt, st) + c0t


def _rwkv_kernel(lwf_ref, kdf_ref, af_ref, lwb_ref, kdb_ref, ab_ref, rf_ref, vf_ref, kkf_ref,
                 rb_ref, vb_ref, kkb_ref, sin_ref, yf_ref, yb_ref, sout_ref, st_ref):
    c = pl.program_id(1)

    @pl.when(c == 0)
    def _():
        st_ref[...] = sin_ref[0]

    groups = lwf_ref.shape[2] // RWKV_LANES
    for d, (lw_ref, kd_ref, a_ref, r_ref, v_ref, kk_ref, y_ref) in enumerate((
            (lwf_ref, kdf_ref, af_ref, rf_ref, vf_ref, kkf_ref, yf_ref),
            (lwb_ref, kdb_ref, ab_ref, rb_ref, vb_ref, kkb_ref, yb_ref))):
        for hg in range(groups):
            lanes = slice(hg * RWKV_LANES, (hg + 1) * RWKV_LANES)
            chain = d * groups + hg
            y, st = _rwkv_chain(lw_ref[0, :, lanes], kd_ref[0, :, lanes], a_ref[0, :, lanes],
                                r_ref[0, :, lanes], v_ref[0, :, lanes], kk_ref[0, :, lanes],
                                st_ref[chain], reverse=bool(d))
            y_ref[0, :, lanes] = y
            st_ref[chain] = st

    @pl.when(c == pl.num_programs(1) - 1)
    def _():
        sout_ref[0] = st_ref[...]


def rwkv_scan(lw, kd, a, r, v, kk, state):
    b, length, w = r.shape
    t = RWKV_CHUNK
    nc = length // t
    chains = state.shape[1]
    fwd = pl.BlockSpec((1, t, w), lambda bi, c: (bi, c, 0))
    bwd = pl.BlockSpec((1, t, w), lambda bi, c: (bi, nc - 1 - c, 0))
    st_spec = pl.BlockSpec((1, chains, RWKV_LANES, RWKV_LANES), lambda bi, c: (bi, 0, 0, 0))
    y_shape = jax.ShapeDtypeStruct((b, length, w), F32)
    return pl.pallas_call(
        _rwkv_kernel,
        grid=(b, nc),
        in_specs=[fwd, fwd, fwd, bwd, bwd, bwd, fwd, fwd, fwd, bwd, bwd, bwd, st_spec],
        out_specs=[fwd, bwd, st_spec],
        out_shape=[y_shape, y_shape, jax.ShapeDtypeStruct(state.shape, F32)],
        scratch_shapes=[pltpu.VMEM((chains, RWKV_LANES, RWKV_LANES), F32)],
        compiler_params=_cparams("parallel", "arbitrary"),
        name="rwkv_scan",
    )(lw[0], kd[0], a[0], lw[1], kd[1], a[1], r, v, kk, r, v, kk, state)


def _heads(t, n):
    return t.reshape(t.shape[:-1] + (n, HEAD_DIM))


def _rope_tables(n_tokens):
    t = jnp.arange(n_tokens, dtype=jnp.int32)
    row = (t // GRID_W).astype(F32)
    col = (t % GRID_W).astype(F32)
    inv = ROPE_THETA ** (-jnp.arange(ROPE_FREQS, dtype=F32) / ROPE_FREQS)
    ang = jnp.concatenate([row[:, None] * inv, col[:, None] * inv], axis=-1)
    cos, sin = jnp.cos(ang), jnp.sin(ang)
    return jnp.concatenate([cos, cos], axis=-1), jnp.concatenate([sin, sin], axis=-1)


def _shift_mix(z, mu):
    zp = jnp.pad(z, ((0, 0), (1, 1), (0, 0)))
    return z + (0.5 * (zp[:, :-2] + zp[:, 2:]) - z) * mu


def _rwkv_inputs(zd, w0, w2, a0, a2, g2, k_k, k_a):
    r, k, v, wd, ad, gd = _split(zd, (D_W, D_W, D_W, DECAY_LORA, ICLR_LORA, GATE_LORA))
    g = jax.nn.sigmoid(gd) @ g2
    kk = _heads(k * k_k, D_HEADS)
    kk = kk / jnp.maximum(jnp.sqrt(jnp.sum(kk * kk, axis=-1, keepdims=True)), 1e-12)
    logw, kd, av = [], [], []
    for d in range(2):
        w = -jax.nn.softplus(-(w0[d] + jnp.tanh(wd) @ w2[d])) - 0.5
        a = jax.nn.sigmoid(a0[d] + ad @ a2[d])
        logw.append(-jnp.exp(w))
        kd.append(k * (1 + (a - 1) * k_a))
        av.append(a)
    return {"r": r, "v": v, "kk": kk.reshape(k.shape), "g": g, "logw": logw, "k": kd, "a": av}


def _rwkv_readout(y, p, r_k, ln_w, ln_b):
    b, length, _ = y.shape
    yh = _heads(y, D_HEADS)
    mean = jnp.mean(yh, axis=-1, keepdims=True)
    var = jnp.mean(jnp.square(yh - mean), axis=-1, keepdims=True)
    yn = ((yh - mean) * lax.rsqrt(var + GN_EPS)).reshape(b, length, D_W)
    yn = yn * ln_w + ln_b
    ksum = p["k"][0] + p["k"][1]
    bonus = jnp.sum(_heads(p["r"] * ksum, D_HEADS) * r_k, axis=-1, keepdims=True) * _heads(p["v"], D_HEADS)
    return ((yn + bonus.reshape(b, length, D_W)) * p["g"]).astype(BF16)


def _rwkv_mixer(zd, zdc, mu, w0, w2, a0, a2, g2, k_k, k_a, r_k, ln_w, ln_b, need_ctx):
    lat = _rwkv_inputs(_shift_mix(zd, mu), w0, w2, a0, a2, g2, k_k, k_a)
    cx = _rwkv_inputs(_shift_mix(zdc, mu), w0, w2, a0, a2, g2, k_k, k_a)
    bsz = zd.shape[0]
    chains = 2 * (D_W // RWKV_LANES)
    zero = jnp.zeros((bsz, chains, RWKV_LANES, RWKV_LANES), F32)
    ycf, ycb, state = rwkv_scan(cx["logw"], cx["k"], cx["a"], cx["r"], cx["v"], cx["kk"], zero)
    yf, yb, _ = rwkv_scan(lat["logw"], lat["k"], lat["a"], lat["r"], lat["v"], lat["kk"], state)
    y_d = _rwkv_readout(yf + yb, lat, r_k, ln_w, ln_b)
    yc_d = _rwkv_readout(ycf + ycb, cx, r_k, ln_w, ln_b) if need_ctx else None
    return y_d, yc_d


def kernel(x, c, ctx, c_ctx, w_mod, b_mod, norm_mix, norm_ffn, w_in_even, w_out_even, q_norm_a, k_norm_a,
           sink_b, w_in_odd, w_out_odd, rpb_c, shift_mu, decay_w0, decay_w2, iclr_a0, iclr_a2, gate_g2,
           k_k, k_a, r_k, ln_x_w, ln_x_b, w_ffn_in, w_ffn_out, norm_out):
    bsz, s, d = x.shape
    n_ctx = ctx.shape[1]
    cos2, sin2 = _rope_tables(s)
    cos_ctx = jnp.ones((n_ctx, HEAD_DIM), F32)
    sin_ctx = jnp.zeros((n_ctx, HEAD_DIM), F32)
    act = jnp.zeros((8, d), F32).at[:bsz].set(jax.nn.silu(c)).at[bsz].set(jax.nn.silu(c_ctx))
    mods = modulation(act.astype(BF16), w_mod, b_mod)
    w_out_even = w_out_even.astype(BF16)
    w_in_odd = w_in_odd.astype(BF16)
    w_out_odd = w_out_odd.astype(BF16)
    w_ffn_in = w_ffn_in.astype(BF16)
    w_ffn_out = w_ffn_out.astype(BF16)
    no_sink = jnp.full((A_HEADS,), NEG_INF, F32)
    for layer in range(DEPTH):
        need_ctx = layer < DEPTH - 1
        mod = mods[layer, :bsz][:, None, :]
        modc = jnp.broadcast_to(mods[layer, bsz][None, None, :], (bsz, 1, 6 * d))
        sh1, sc1, g1, sh2, sc2, g2 = jnp.split(mod, 6, axis=-1)
        csh1, csc1, cg1, csh2, csc2, cg2 = jnp.split(modc, 6, axis=-1)
        gain_mix = norm_mix[layer][None, :]
        gain_ffn = norm_ffn[layer][None, :]
        i = layer // 2
        if layer % 2 == 0:
            w_out = w_out_even[i]
            qg, kg = q_norm_a[i][None, :], k_norm_a[i][None, :]
            w_ext = even_weight(w_in_even[i], q_norm_a[i], k_norm_a[i])
            qa, ka, va, qb, kb, vb = even_proj(x, gain_mix, sh1, sc1, w_ext, cos2, sin2, qg, kg)
            qca, kca, vca, qcb, kcb, vcb = even_proj(ctx, gain_mix, csh1, csc1, w_ext, cos_ctx, sin_ctx, qg, kg)
            y1 = global_attention(qa, ka, va, kca, vca)
            y2 = window_attention(qb, kb, vb, kcb, vcb, sink_b[i])
            if need_ctx:
                yc1 = ctx_attention(qca, kca, vca, no_sink)
                yc2 = ctx_attention(qcb, kcb, vcb, sink_b[i])
        else:
            w_out = w_out_odd[i]
            q, k, v, zd = odd_proj(x, gain_mix, sh1, sc1, w_in_odd[i])
            qc, kc, vc, zdc = odd_proj(ctx, gain_mix, csh1, csc1, w_in_odd[i])
            bias = na_bias_tables(rpb_c[i], s // GRID_W)
            y1 = neighborhood_attention(q, k, v, kc, vc, bias)
            y2, yc2 = _rwkv_mixer(zd, zdc, shift_mu[i], decay_w0[i], decay_w2[i], iclr_a0[i], iclr_a2[i],
                                  gate_g2[i], k_k[i], k_a[i], r_k[i], ln_x_w[i], ln_x_b[i], need_ctx)
            if need_ctx:
                yc1 = ctx_attention(qc, kc, vc, no_sink)
        x = out_proj(x, g1, y1, y2, w_out)
        x = ffn(x, gain_ffn, sh2, sc2, g2, w_ffn_in[layer], w_ffn_out[layer])
        if need_ctx:
            ctx = out_proj(ctx, cg1, yc1, yc2, w_out)
            ctx = ffn(ctx, gain_ffn, csh2, csc2, cg2, w_ffn_in[layer], w_ffn_out[layer])
    return final_norm(x, norm_out[None, :])
```

```python
import functools

import jax
import jax.numpy as jnp
import numpy as np
from jax import lax
from jax.experimental import pallas as pl
from jax.experimental.pallas import tpu as pltpu

F32 = jnp.float32
BF16 = jnp.bfloat16

D_MODEL = 1024
DEPTH = 4
GRID_W = 64
HEAD_DIM = 64
SCALE = HEAD_DIM ** -0.5
ROPE_HALF = HEAD_DIM // 2
ROPE_FREQS = HEAD_DIM // 4
ROPE_THETA = 10000.0
A_HEADS = 8
A_KV_HEADS = 2
B_HEADS = 8
B_KV_HEADS = 2
WINDOW = 128
C_HEADS = 8
NA_ROWS = 8
NA_COLS = 16
D_HEADS = 8
DECAY_LORA = 64
ICLR_LORA = 64
GATE_LORA = 128
GN_EPS = 64e-5
RMS_EPS = 1e-6
NEG_INF = -1e30
FFN_HIDDEN = -(-8 * D_MODEL // (3 * 256)) * 256

A_Q = A_HEADS * HEAD_DIM
A_KV = A_KV_HEADS * HEAD_DIM
B_Q = B_HEADS * HEAD_DIM
B_KV = B_KV_HEADS * HEAD_DIM
C_W = C_HEADS * HEAD_DIM
D_W = D_HEADS * HEAD_DIM
EVEN_WIDTHS = (A_Q, A_KV, A_KV, B_Q, B_KV, B_KV)
D_SHIFT_W = 3 * D_W + DECAY_LORA + ICLR_LORA + GATE_LORA

VMEM_LIMIT_BYTES = 56 * 1024 * 1024
RWKV_CHUNK = HEAD_DIM
RWKV_INV_BLOCK = 16
RWKV_LANES = 256
NA_TILE_ROWS = 8
NA_KEY_ROWS = NA_TILE_ROWS + NA_ROWS - 1


def _cparams(*sem):
    return pltpu.CompilerParams(dimension_semantics=sem, vmem_limit_bytes=VMEM_LIMIT_BYTES)


def _split(t, widths):
    return jnp.split(t, [int(o) for o in np.cumsum(widths)[:-1]], axis=-1)


def _pick_tile(n, target):
    t = min(n, target)
    while n % t:
        t //= 2
    return t


def _mod_kernel(a_ref, w_ref, b_ref, o_ref):
    w = w_ref[0].astype(BF16)
    o_ref[0] = jnp.dot(a_ref[...], w, preferred_element_type=F32) + b_ref[0]


def modulation(act, w_mod, b_mod):
    depth, d, n = w_mod.shape
    tn = 1536
    return pl.pallas_call(
        _mod_kernel,
        grid=(depth, n // tn),
        in_specs=[
            pl.BlockSpec((8, d), lambda l, j: (0, 0)),
            pl.BlockSpec((1, d, tn), lambda l, j: (l, 0, j)),
            pl.BlockSpec((1, 1, tn), lambda l, j: (l, 0, j)),
        ],
        out_specs=pl.BlockSpec((1, 8, tn), lambda l, j: (l, 0, j)),
        out_shape=jax.ShapeDtypeStruct((depth, 8, n), F32),
        compiler_params=_cparams("parallel", "parallel"),
        name="modulation",
    )(act, w_mod, b_mod.reshape(depth, 1, n))


def _norm_mod(x, gain, shift, scale):
    ms = jnp.mean(x * x, axis=-1, keepdims=True)
    h = x * lax.rsqrt(ms + RMS_EPS) * gain
    return h * (1.0 + scale) + shift


def _head_rms(z):
    return lax.rsqrt(jnp.mean(z * z, axis=-1, keepdims=True) + RMS_EPS)


def _nt_dot(a, b):
    return lax.dot_general(a, b, (((1,), (1,)), ((), ())), preferred_element_type=F32)


_EV_QA, _EV_KA, _EV_VA, _EV_QB, _EV_KB, _EV_VB = (int(o) for o in np.cumsum((0,) + EVEN_WIDTHS)[:-1])
_EV_QAR = sum(EVEN_WIDTHS)
_EV_KAR = _EV_QAR + A_Q
_EV_QBR = _EV_KAR + A_KV
_EV_KBR = _EV_QBR + B_Q
EVEN_EXT = _EV_KBR + B_KV


def _even_proj_kernel(x_ref, gain_ref, sh_ref, sc_ref, w_ref, wvt_ref, cos_ref, sin_ref, qg_ref, kg_ref,
                      qa_ref, ka_ref, vat_ref, qb_ref, kb_ref, vb_ref):
    h = _norm_mod(x_ref[0], gain_ref[...], sh_ref[0], sc_ref[0]).astype(BF16)
    z = jnp.dot(h, w_ref[...], preferred_element_type=F32)
    vat = _nt_dot(wvt_ref[...], h)
    cos = cos_ref[...]
    sin = sin_ref[...]
    qg = qg_ref[...]
    kg = kg_ref[...]

    def sl(off, i):
        return z[:, off + i * HEAD_DIM: off + (i + 1) * HEAD_DIM]

    for i in range(A_HEADS):
        zq = sl(_EV_QA, i)
        qa = _head_rms(zq) * (zq * qg * cos + sl(_EV_QAR, i) * sin)
        qa_ref[0, i] = (qa * SCALE).astype(BF16)
    for i in range(B_HEADS):
        qb_ref[0, i] = ((sl(_EV_QB, i) * cos + sl(_EV_QBR, i) * sin) * SCALE).astype(BF16)
    for i in range(A_KV_HEADS):
        zk = sl(_EV_KA, i)
        ka_ref[0, i] = (_head_rms(zk) * (zk * kg * cos + sl(_EV_KAR, i) * sin)).astype(BF16)
        vat_ref[0, i] = vat[i * HEAD_DIM:(i + 1) * HEAD_DIM].astype(BF16)
    for i in range(B_KV_HEADS):
        kb_ref[0, i] = (sl(_EV_KB, i) * cos + sl(_EV_KBR, i) * sin).astype(BF16)
        vb_ref[0, i] = sl(_EV_VB, i).astype(BF16)


def _rot_cols(w):
    d = w.shape[0]
    wh = w.reshape(d, -1, 2, ROPE_HALF)
    return jnp.stack([-wh[:, :, 1], wh[:, :, 0]], axis=2).reshape(w.shape)


def even_weight(w_in, q_gain, k_gain):
    qa, ka, va, qb, kb, vb = _split(w_in, EVEN_WIDTHS)
    qar = _rot_cols(qa * jnp.tile(q_gain, A_HEADS))
    kar = _rot_cols(ka * jnp.tile(k_gain, A_KV_HEADS))
    return jnp.concatenate([w_in, qar, kar, _rot_cols(qb), _rot_cols(kb)], axis=1).astype(BF16)


def even_proj(x, gain, shift, scale, w_ext, w_vat, cos2, sin2, q_gain, k_gain):
    b, s, d = x.shape
    tm = _pick_tile(s, 512)
    vec = pl.BlockSpec((1, 1, d), lambda bi, i: (bi, 0, 0))
    tab = pl.BlockSpec((tm, HEAD_DIM), lambda bi, i: (i, 0))
    hv = pl.BlockSpec((1, HEAD_DIM), lambda bi, i: (0, 0))

    def hm(n):
        return pl.BlockSpec((1, n, tm, HEAD_DIM), lambda bi, i: (bi, 0, i, 0))

    def shape(n):
        return jax.ShapeDtypeStruct((b, n, s, HEAD_DIM), BF16)

    return pl.pallas_call(
        _even_proj_kernel,
        grid=(b, s // tm),
        in_specs=[
            pl.BlockSpec((1, tm, d), lambda bi, i: (bi, i, 0)),
            pl.BlockSpec((1, d), lambda bi, i: (0, 0)),
            vec, vec,
            pl.BlockSpec((d, EVEN_EXT), lambda bi, i: (0, 0)),
            pl.BlockSpec((A_KV, d), lambda bi, i: (0, 0)),
            tab, tab, hv, hv,
        ],
        out_specs=[hm(A_HEADS), hm(A_KV_HEADS),
                   pl.BlockSpec((1, A_KV_HEADS, HEAD_DIM, tm), lambda bi, i: (bi, 0, 0, i)),
                   hm(B_HEADS), hm(B_KV_HEADS), hm(B_KV_HEADS)],
        out_shape=[shape(A_HEADS), shape(A_KV_HEADS),
                   jax.ShapeDtypeStruct((b, A_KV_HEADS, HEAD_DIM, s), BF16),
                   shape(B_HEADS), shape(B_KV_HEADS), shape(B_KV_HEADS)],
        compiler_params=_cparams("parallel", "parallel"),
        name="even_proj",
    )(x, gain, shift, scale, w_ext, w_vat, cos2, sin2, q_gain, k_gain)


def _odd_proj_kernel(x_ref, gain_ref, sh_ref, sc_ref, w_ref, q_ref, k_ref, v_ref, zd_ref):
    h = _norm_mod(x_ref[0], gain_ref[...], sh_ref[0], sc_ref[0])
    z = jnp.dot(h.astype(BF16), w_ref[...], preferred_element_type=F32)
    for i in range(C_HEADS):
        lo = i * HEAD_DIM
        q_ref[0, i] = (z[:, lo:lo + HEAD_DIM] * SCALE).astype(BF16)
        k_ref[0, i] = z[:, C_W + lo:C_W + lo + HEAD_DIM].astype(BF16)
        v_ref[0, i] = z[:, 2 * C_W + lo:2 * C_W + lo + HEAD_DIM].astype(BF16)
    zd_ref[0] = z[:, 3 * C_W:]


def odd_proj(x, gain, shift, scale, w):
    b, s, d = x.shape
    n = w.shape[1]
    tm = _pick_tile(s, 512)
    vec = pl.BlockSpec((1, 1, d), lambda bi, i: (bi, 0, 0))
    hm = pl.BlockSpec((1, C_HEADS, tm, HEAD_DIM), lambda bi, i: (bi, 0, i, 0))
    hshape = jax.ShapeDtypeStruct((b, C_HEADS, s, HEAD_DIM), BF16)
    return pl.pallas_call(
        _odd_proj_kernel,
        grid=(b, s // tm),
        in_specs=[
            pl.BlockSpec((1, tm, d), lambda bi, i: (bi, i, 0)),
            pl.BlockSpec((1, d), lambda bi, i: (0, 0)),
            vec, vec,
            pl.BlockSpec((d, n), lambda bi, i: (0, 0)),
        ],
        out_specs=[hm, hm, hm, pl.BlockSpec((1, tm, D_SHIFT_W), lambda bi, i: (bi, i, 0))],
        out_shape=[hshape, hshape, hshape, jax.ShapeDtypeStruct((b, s, D_SHIFT_W), F32)],
        compiler_params=_cparams("parallel", "parallel"),
        name="odd_proj",
    )(x, gain, shift, scale, w)


def _out_proj_kernel(x_ref, g_ref, y1_ref, y2_ref, w1_ref, w2_ref, o_ref, *, y2_heads):
    acc = jnp.dot(y1_ref[0, 0], w1_ref[0], preferred_element_type=F32)
    for h in range(1, y1_ref.shape[1]):
        acc = acc + jnp.dot(y1_ref[0, h], w1_ref[h], preferred_element_type=F32)
    if y2_heads:
        for h in range(y2_ref.shape[1]):
            acc = acc + jnp.dot(y2_ref[0, h], w2_ref[h], preferred_element_type=F32)
    else:
        acc = acc + jnp.dot(y2_ref[0], w2_ref[...], preferred_element_type=F32)
    o_ref[0] = x_ref[0] + g_ref[0] * acc


def out_proj(x, gate, y1, y2, w):
    b, s, d = x.shape
    nh = y1.shape[1]
    half = nh * HEAD_DIM
    tm = _pick_tile(s, 1024)
    y2_heads = y2.ndim == 4
    hm = pl.BlockSpec((1, nh, tm, HEAD_DIM), lambda bi, i: (bi, 0, i, 0))
    wh = pl.BlockSpec((nh, HEAD_DIM, d), lambda bi, i: (0, 0, 0))
    w1 = w[:half].reshape(nh, HEAD_DIM, d)
    if y2_heads:
        y2_spec, w2_spec, w2 = hm, wh, w[half:].reshape(nh, HEAD_DIM, d)
    else:
        y2_spec = pl.BlockSpec((1, tm, half), lambda bi, i: (bi, i, 0))
        w2_spec, w2 = pl.BlockSpec((half, d), lambda bi, i: (0, 0)), w[half:]
    return pl.pallas_call(
        functools.partial(_out_proj_kernel, y2_heads=y2_heads),
        grid=(b, s // tm),
        in_specs=[
            pl.BlockSpec((1, tm, d), lambda bi, i: (bi, i, 0)),
            pl.BlockSpec((1, 1, d), lambda bi, i: (bi, 0, 0)),
            hm, y2_spec, wh, w2_spec,
        ],
        out_specs=pl.BlockSpec((1, tm, d), lambda bi, i: (bi, i, 0)),
        out_shape=jax.ShapeDtypeStruct((b, s, d), F32),
        compiler_params=_cparams("parallel", "parallel"),
        name="out_proj",
    )(x, gate, y1, y2, w1, w2)


def _ffn_kernel(x_ref, gain_ref, sh_ref, sc_ref, g_ref, wg_ref, wu_ref, wo_ref, o_ref, h_ref, acc_ref):
    j = pl.program_id(2)

    @pl.when(j == 0)
    def _():
        h_ref[...] = _norm_mod(x_ref[0], gain_ref[...], sh_ref[0], sc_ref[0]).astype(BF16)
        acc_ref[...] = jnp.zeros_like(acc_ref)

    h = h_ref[...]
    gate = jnp.dot(h, wg_ref[...], preferred_element_type=F32)
    up = jnp.dot(h, wu_ref[...], preferred_element_type=F32)
    act = gate * (1.0 / (1.0 + jnp.exp(-gate))) * up
    acc_ref[...] += jnp.dot(act.astype(BF16), wo_ref[...], preferred_element_type=F32)

    @pl.when(j == pl.num_programs(2) - 1)
    def _():
        o_ref[0] = x_ref[0] + g_ref[0] * acc_ref[...]


def ffn(x, gain, shift, scale, gate, w_in, w_out):
    b, s, d = x.shape
    f = w_out.shape[0]
    tm = _pick_tile(s, 512)
    th = 1408 if f % 1408 == 0 else _pick_tile(f, 256)
    nj = f // th
    vec = pl.BlockSpec((1, 1, d), lambda bi, i, j: (bi, 0, 0))
    return pl.pallas_call(
        _ffn_kernel,
        grid=(b, s // tm, nj),
        in_specs=[
            pl.BlockSpec((1, tm, d), lambda bi, i, j: (bi, i, 0)),
            pl.BlockSpec((1, d), lambda bi, i, j: (0, 0)),
            vec, vec, vec,
            pl.BlockSpec((d, th), lambda bi, i, j: (0, j)),
            pl.BlockSpec((d, th), lambda bi, i, j: (0, j + nj)),
            pl.BlockSpec((th, d), lambda bi, i, j: (j, 0)),
        ],
        out_specs=pl.BlockSpec((1, tm, d), lambda bi, i, j: (bi, i, 0)),
        out_shape=jax.ShapeDtypeStruct((b, s, d), F32),
        scratch_shapes=[pltpu.VMEM((tm, d), BF16), pltpu.VMEM((tm, d), F32)],
        compiler_params=_cparams("parallel", "parallel", "arbitrary"),
        name="ffn",
    )(x, gain, shift, scale, gate, w_in, w_in, w_out)


def _rms_kernel(x_ref, gain_ref, o_ref):
    x = x_ref[0]
    ms = jnp.mean(x * x, axis=-1, keepdims=True)
    o_ref[0] = x * lax.rsqrt(ms + RMS_EPS) * gain_ref[...]


def final_norm(x, gain):
    b, s, d = x.shape
    tm = _pick_tile(s, 1024)
    return pl.pallas_call(
        _rms_kernel,
        grid=(b, s // tm),
        in_specs=[pl.BlockSpec((1, tm, d), lambda bi, i: (bi, i, 0)),
                  pl.BlockSpec((1, d), lambda bi, i: (0, 0))],
        out_specs=pl.BlockSpec((1, tm, d), lambda bi, i: (bi, i, 0)),
        out_shape=jax.ShapeDtypeStruct((b, s, d), F32),
        compiler_params=_cparams("parallel", "parallel"),
        name="final_norm",
    )(x, gain)


def _flash_kernel(q_ref, k_ref, vt_ref, kx_ref, vxt_ref, o_ref, m_ref, l_ref, acc_ref):
    kv = pl.program_id(3)
    g, tq, hd = q_ref.shape[1:]

    @pl.when(kv == 0)
    def _():
        m_ref[...] = jnp.full_like(m_ref, -jnp.inf)
        l_ref[...] = jnp.zeros_like(l_ref)
        acc_ref[...] = jnp.zeros_like(acc_ref)

    q = q_ref[0].reshape(g * tq, hd)

    def update(k, vt):
        s = _nt_dot(k, q)
        m_prev = m_ref[...]
        m_new = jnp.maximum(m_prev, jnp.max(s, axis=0, keepdims=True))
        alpha = jnp.exp(m_prev - m_new)
        p = jnp.exp(s - m_new)
        l_ref[...] = alpha * l_ref[...] + jnp.sum(p, axis=0, keepdims=True)
        acc_ref[...] = alpha * acc_ref[...] + jnp.dot(vt, p.astype(BF16), preferred_element_type=F32)
        m_ref[...] = m_new

    update(k_ref[0, 0], vt_ref[0, 0])

    @pl.when(kv == pl.num_programs(3) - 1)
    def _():
        update(kx_ref[0, 0], vxt_ref[0, 0])
        ot = (acc_ref[...] / l_ref[...]).astype(BF16)
        eye = (lax.broadcasted_iota(jnp.int32, (tq, tq), 0)
               == lax.broadcasted_iota(jnp.int32, (tq, tq), 1)).astype(BF16)
        for h in range(g):
            o_ref[0, h] = _nt_dot(eye, ot[:, h * tq:(h + 1) * tq]).astype(o_ref.dtype)


def global_attention(q, k, vt, kx, vxt):
    b, h, s, hd = q.shape
    hkv = k.shape[1]
    n_ctx = kx.shape[2]
    g = h // hkv
    tq = _pick_tile(s, 256)
    tk = _pick_tile(s, 2048)
    q_spec = pl.BlockSpec((1, g, tq, hd), lambda bi, gi, i, j: (bi, gi, i, 0))
    return pl.pallas_call(
        _flash_kernel,
        grid=(b, hkv, s // tq, s // tk),
        in_specs=[
            q_spec,
            pl.BlockSpec((1, 1, tk, hd), lambda bi, gi, i, j: (bi, gi, j, 0)),
            pl.BlockSpec((1, 1, hd, tk), lambda bi, gi, i, j: (bi, gi, 0, j)),
            pl.BlockSpec((1, 1, n_ctx, hd), lambda bi, gi, i, j: (bi, gi, 0, 0)),
            pl.BlockSpec((1, 1, hd, n_ctx), lambda bi, gi, i, j: (bi, gi, 0, 0)),
        ],
        out_specs=q_spec,
        out_shape=jax.ShapeDtypeStruct((b, h, s, hd), BF16),
        scratch_shapes=[pltpu.VMEM((1, g * tq), F32), pltpu.VMEM((1, g * tq), F32),
                        pltpu.VMEM((hd, g * tq), F32)],
        compiler_params=_cparams("parallel", "parallel", "parallel", "arbitrary"),
        name="global_attention",
    )(q, k, vt, kx, vxt)


def _window_kernel(sink_ref, q_ref, kp_ref, kc_ref, kn_ref, vp_ref, vc_ref, vn_ref, kx_ref, vx_ref, o_ref,
                   *, seq):
    gi = pl.program_id(1)
    i = pl.program_id(2)
    g, tq, hd = q_ref.shape[1:]
    w = kp_ref.shape[2]
    q = q_ref[0].reshape(g * tq, hd)
    k = jnp.concatenate([kp_ref[0, 0], kc_ref[0, 0], kn_ref[0, 0]], axis=0)
    v = jnp.concatenate([vp_ref[0, 0], vc_ref[0, 0], vn_ref[0, 0]], axis=0)
    span = tq + 2 * w
    s_loc = _nt_dot(q, k)
    row = lax.broadcasted_iota(jnp.int32, (g * tq, span), 0)
    col = lax.broadcasted_iota(jnp.int32, (g * tq, span), 1)
    qpos = i * tq + row % tq
    kpos = i * tq - w + col
    valid = (jnp.abs(kpos - qpos) <= w) & (kpos >= 0) & (kpos < seq)
    s_loc = jnp.where(valid, s_loc, NEG_INF)
    s_ctx = _nt_dot(q, kx_ref[0, 0])
    hrow = lax.broadcasted_iota(jnp.int32, (g * tq, 1), 0) // tq
    sink = jnp.zeros((g * tq, 1), F32)
    for hh in range(g):
        sink = jnp.where(hrow == hh, sink_ref[gi * g + hh], sink)
    m = jnp.maximum(jnp.maximum(jnp.max(s_loc, axis=-1, keepdims=True),
                                jnp.max(s_ctx, axis=-1, keepdims=True)), sink)
    p_loc = jnp.exp(s_loc - m)
    p_ctx = jnp.exp(s_ctx - m)
    l = jnp.sum(p_loc, axis=-1, keepdims=True) + jnp.sum(p_ctx, axis=-1, keepdims=True) + jnp.exp(sink - m)
    o = jnp.dot(p_loc.astype(BF16), v, preferred_element_type=F32)
    o = o + jnp.dot(p_ctx.astype(BF16), vx_ref[0, 0], preferred_element_type=F32)
    o_ref[0] = (o / l).reshape(g, tq, hd).astype(o_ref.dtype)


def window_attention(q, k, v, kx, vx, sink):
    b, h, s, hd = q.shape
    hkv = k.shape[1]
    g = h // hkv
    n_ctx = kx.shape[2]
    tq = _pick_tile(s, 256)
    r = tq // WINDOW
    nwb = s // WINDOW
    prev = pl.BlockSpec((1, 1, WINDOW, hd), lambda bi, gi, i: (bi, gi, jnp.maximum(i * r - 1, 0), 0))
    cur = pl.BlockSpec((1, 1, tq, hd), lambda bi, gi, i: (bi, gi, i, 0))
    nxt = pl.BlockSpec((1, 1, WINDOW, hd), lambda bi, gi, i: (bi, gi, jnp.minimum((i + 1) * r, nwb - 1), 0))
    cx = pl.BlockSpec((1, 1, n_ctx, hd), lambda bi, gi, i: (bi, gi, 0, 0))
    return pl.pallas_call(
        functools.partial(_window_kernel, seq=s),
        grid=(b, hkv, s // tq),
        in_specs=[
            pl.BlockSpec(memory_space=pltpu.SMEM),
            pl.BlockSpec((1, g, tq, hd), lambda bi, gi, i: (bi, gi, i, 0)),
            prev, cur, nxt, prev, cur, nxt, cx, cx,
        ],
        out_specs=pl.BlockSpec((1, g, tq, hd), lambda bi, gi, i: (bi, gi, i, 0)),
        out_shape=jax.ShapeDtypeStruct((b, h, s, hd), BF16),
        compiler_params=_cparams("parallel", "parallel", "parallel"),
        name="window_attention",
    )(sink, q, k, k, k, v, v, v, kx, vx)


def _ctx_attn_kernel(sink_ref, q_ref, k_ref, v_ref, o_ref, *, v_transposed):
    hi = pl.program_id(1)
    s = _nt_dot(q_ref[0, 0], k_ref[0, 0])
    sink = sink_ref[hi]
    m = jnp.maximum(jnp.max(s, axis=-1, keepdims=True), sink)
    p = jnp.exp(s - m)
    l = jnp.sum(p, axis=-1, keepdims=True) + jnp.exp(sink - m)
    if v_transposed:
        o = _nt_dot(p.astype(BF16), v_ref[0, 0])
    else:
        o = jnp.dot(p.astype(BF16), v_ref[0, 0], preferred_element_type=F32)
    o_ref[0, 0] = (o / l).astype(o_ref.dtype)


def ctx_attention(q, k, v, sink, v_transposed=False):
    b, h, n, hd = q.shape
    g = h // k.shape[1]
    kv = pl.BlockSpec((1, 1, n, hd), lambda bi, hi: (bi, hi // g, 0, 0))
    vs = pl.BlockSpec((1, 1, hd, n), lambda bi, hi: (bi, hi // g, 0, 0)) if v_transposed else kv
    qs = pl.BlockSpec((1, 1, n, hd), lambda bi, hi: (bi, hi, 0, 0))
    return pl.pallas_call(
        functools.partial(_ctx_attn_kernel, v_transposed=v_transposed),
        grid=(b, h),
        in_specs=[pl.BlockSpec(memory_space=pltpu.SMEM), qs, kv, vs],
        out_specs=qs,
        out_shape=jax.ShapeDtypeStruct((b, h, n, hd), BF16),
        compiler_params=_cparams("parallel", "parallel"),
        name="ctx_attention",
    )(sink, q, k, v)


def _na_kernel(q_ref, k_ref, v_ref, kx_ref, vx_ref, bias_ref, o_ref, *, rows):
    t = pl.program_id(2)
    nkeys = NA_KEY_ROWS * GRID_W
    kstart = jnp.clip(t * NA_TILE_ROWS - NA_ROWS // 2, 0, rows - NA_KEY_ROWS)
    off = pl.multiple_of(kstart * GRID_W, GRID_W)
    q = q_ref[0, 0]
    kw = k_ref[0, 0, pl.ds(off, nkeys), :]
    vw = v_ref[0, 0, pl.ds(off, nkeys), :]
    s_loc = _nt_dot(q, kw) + bias_ref[0, 0]
    s_ctx = _nt_dot(q, kx_ref[0, 0])
    m = jnp.maximum(jnp.max(s_loc, axis=-1, keepdims=True), jnp.max(s_ctx, axis=-1, keepdims=True))
    p_loc = jnp.exp(s_loc - m)
    p_ctx = jnp.exp(s_ctx - m)
    l = jnp.sum(p_loc, axis=-1, keepdims=True) + jnp.sum(p_ctx, axis=-1, keepdims=True)
    o = jnp.dot(p_loc.astype(BF16), vw, preferred_element_type=F32)
    o = o + jnp.dot(p_ctx.astype(BF16), vx_ref[0, 0], preferred_element_type=F32)
    o_ref[0, 0] = (o / l).astype(o_ref.dtype)


def na_bias_tables(rpb, rows):
    nh = rpb.shape[0]
    i = np.arange(NA_TILE_ROWS)
    j = np.arange(NA_KEY_ROWS)
    qc = np.arange(GRID_W)
    kc = np.arange(GRID_W)
    cs = np.clip(qc - NA_COLS // 2, 0, GRID_W - NA_COLS)
    col_valid = (kc[None, :] >= cs[:, None]) & (kc[None, :] < cs[:, None] + NA_COLS)
    dcol = np.clip(kc[None, :] - qc[:, None] + NA_COLS - 1, 0, 2 * NA_COLS - 2)
    per_drow = jnp.where(col_valid[None, None], rpb.astype(F32)[:, :, dcol], NEG_INF)
    masked = jnp.full((nh, 1, GRID_W, GRID_W), NEG_INF, F32)
    per_drow = jnp.concatenate([per_drow, masked], axis=1)
    out = []
    for r0 in (0, NA_TILE_ROWS, rows - NA_TILE_ROWS):
        r = r0 + i
        rs = np.clip(r - NA_ROWS // 2, 0, rows - NA_ROWS)
        kstart = int(np.clip(r0 - NA_ROWS // 2, 0, rows - NA_KEY_ROWS))
        krow = kstart + j
        row_valid = (krow[None, :] >= rs[:, None]) & (krow[None, :] < rs[:, None] + NA_ROWS)
        drow = np.where(row_valid, krow[None, :] - r[:, None] + NA_ROWS - 1, 2 * NA_ROWS - 1)
        cols = [per_drow[:, drow[:, jj]] for jj in range(NA_KEY_ROWS)]
        dense = jnp.stack(cols, axis=3)
        out.append(dense.reshape(nh, NA_TILE_ROWS * GRID_W, NA_KEY_ROWS * GRID_W))
    return jnp.stack(out)


def neighborhood_attention(q, k, v, kx, vx, bias):
    b, h, s, hd = q.shape
    rows = s // GRID_W
    nt = rows // NA_TILE_ROWS
    tq = NA_TILE_ROWS * GRID_W
    nkeys = NA_KEY_ROWS * GRID_W
    n_ctx = kx.shape[2]
    full = pl.BlockSpec((1, 1, s, hd), lambda bi, hi, t: (bi, hi, 0, 0))
    cx = pl.BlockSpec((1, 1, n_ctx, hd), lambda bi, hi, t: (bi, hi, 0, 0))

    def case(t):
        return jnp.where(t == 0, 0, jnp.where(t == nt - 1, 2, 1))

    return pl.pallas_call(
        functools.partial(_na_kernel, rows=rows),
        grid=(b, h, nt),
        in_specs=[
            pl.BlockSpec((1, 1, tq, hd), lambda bi, hi, t: (bi, hi, t, 0)),
            full, full, cx, cx,
            pl.BlockSpec((1, 1, tq, nkeys), lambda bi, hi, t: (case(t), hi, 0, 0)),
        ],
        out_specs=pl.BlockSpec((1, 1, tq, hd), lambda bi, hi, t: (bi, hi, t, 0)),
        out_shape=jax.ShapeDtypeStruct((b, h, s, hd), BF16),
        compiler_params=_cparams("parallel", "parallel", "arbitrary"),
        name="neighborhood_attention",
    )(q, k, v, kx, vx, bias)


def _mm(a, b, exact=False):
    if exact:
        return jnp.dot(a, b, preferred_element_type=F32, precision=lax.Precision.HIGHEST)
    return jnp.dot(a.astype(BF16), b.astype(BF16), preferred_element_type=F32)


def _mm_nt(a, b):
    return lax.dot_general(a.astype(BF16), b.astype(BF16), (((1,), (1,)), ((), ())),
                           preferred_element_type=F32)


def _mm_tn(a, b):
    return lax.dot_general(a.astype(BF16), b.astype(BF16), (((0,), (0,)), ((), ())),
                           preferred_element_type=F32)


def _rwkv_chain(lw, kd, a, r, v, kap, st, reverse):
    t, w = lw.shape
    nh = w // HEAD_DIM
    row = lax.broadcasted_iota(jnp.int32, (t, w), 0)
    lane = lax.broadcasted_iota(jnp.int32, (t, w), 1)
    colj = lane % t
    lane_head = lax.broadcasted_iota(jnp.int32, (1, w), 1) // HEAD_DIM
    if reverse:
        strict, incl = colj > row, colj >= row
    else:
        strict, incl = colj < row, colj <= row

    def bd(y):
        return jnp.concatenate([jnp.where(lane_head == h, y, 0.0) for h in range(nh)], axis=0)

    def bdmm(x, y):
        return _mm(x, bd(y))

    tri = incl[:, :t].astype(F32)
    c_incl = _mm(tri, lw, exact=True)
    c_excl = c_incl - lw
    c_mid = c_incl[t // 2:t // 2 + 1, :]
    c_end = c_incl[0:1, :] if reverse else c_incl[t - 1:t, :]
    bvec = kap * a
    e_neg = jnp.exp(c_mid - c_incl)
    left = jnp.concatenate([kap * jnp.exp(c_excl - c_mid), r * jnp.exp(c_incl - c_mid)], axis=0)
    wt = jnp.concatenate([bd(bvec * e_neg), bd(kd * e_neg)], axis=0)
    m = _mm_nt(left, wt)
    l_b = jnp.where(strict, m[:t, :w], 0.0)
    l_k = jnp.where(strict, m[:t, w:], 0.0)
    m_br = jnp.where(incl, m[t:, :w], 0.0)
    m_kr = jnp.where(incl, m[t:, w:], 0.0)

    blk = RWKV_INV_BLOCK
    x = -jnp.where(row // blk == colj // blk, l_b, 0.0)
    inv = jnp.where(row == colj, 1.0, 0.0) + x
    pw = x
    span = 2
    while span < blk:
        pw = bdmm(pw, pw)
        inv = inv + bdmm(inv, pw)
        span *= 2
    while blk < t:
        off = jnp.where((row // (2 * blk) == colj // (2 * blk)) & (row // blk != colj // blk), l_b, 0.0)
        inv = inv - bdmm(bdmm(inv, off), inv)
        blk *= 2

    kt = kap * jnp.exp(c_excl)
    rt = r * jnp.exp(c_incl)
    wk = bdmm(inv, kt)
    u0 = bdmm(inv, bdmm(l_k, v))
    rq = rt - bdmm(m_br, wk)
    y0 = bdmm(m_kr, v) - bdmm(m_br, u0)
    e_end = jnp.exp(c_end - c_incl)
    b_hat = bvec * e_end
    k_hat = kd * e_end
    ri = lax.broadcasted_iota(jnp.int32, (w, w), 0)
    ci = lax.broadcasted_iota(jnp.int32, (w, w), 1)
    same_head = ri // HEAD_DIM == ci // HEAD_DIM
    gt = jnp.where(ri == ci, jnp.exp(c_end), 0.0) - jnp.where(same_head, _mm_tn(b_hat, wk), 0.0)
    c0t = jnp.where(same_head, _mm_tn(k_hat, v) - _mm_tn(b_hat, u0), 0.0)
    y = _mm(rq, st) + y0
    return y, _mm(gt, st) + c0t


def _rwkv_kernel(lwf_ref, kdf_ref, af_ref, lwb_ref, kdb_ref, ab_ref, rf_ref, vf_ref, kkf_ref,
                 rb_ref, vb_ref, kkb_ref, sin_ref, yf_ref, yb_ref, sout_ref, st_ref):
    c = pl.program_id(1)

    @pl.when(c == 0)
    def _():
        st_ref[...] = sin_ref[0]

    groups = lwf_ref.shape[2] // RWKV_LANES
    for d, (lw_ref, kd_ref, a_ref, r_ref, v_ref, kk_ref, y_ref) in enumerate((
            (lwf_ref, kdf_ref, af_ref, rf_ref, vf_ref, kkf_ref, yf_ref),
            (lwb_ref, kdb_ref, ab_ref, rb_ref, vb_ref, kkb_ref, yb_ref))):
        for hg in range(groups):
            lanes = slice(hg * RWKV_LANES, (hg + 1) * RWKV_LANES)
            chain = d * groups + hg
            y, st = _rwkv_chain(lw_ref[0, :, lanes], kd_ref[0, :, lanes], a_ref[0, :, lanes],
                                r_ref[0, :, lanes], v_ref[0, :, lanes], kk_ref[0, :, lanes],
                                st_ref[chain], reverse=bool(d))
            y_ref[0, :, lanes] = y
            st_ref[chain] = st

    @pl.when(c == pl.num_programs(1) - 1)
    def _():
        sout_ref[0] = st_ref[...]


def rwkv_scan(lw, kd, a, r, v, kk, state):
    b, length, w = r.shape
    t = RWKV_CHUNK
    nc = length // t
    chains = state.shape[1]
    fwd = pl.BlockSpec((1, t, w), lambda bi, c: (bi, c, 0))
    bwd = pl.BlockSpec((1, t, w), lambda bi, c: (bi, nc - 1 - c, 0))
    st_spec = pl.BlockSpec((1, chains, RWKV_LANES, RWKV_LANES), lambda bi, c: (bi, 0, 0, 0))
    y_shape = jax.ShapeDtypeStruct((b, length, w), F32)
    return pl.pallas_call(
        _rwkv_kernel,
        grid=(b, nc),
        in_specs=[fwd, fwd, fwd, bwd, bwd, bwd, fwd, fwd, fwd, bwd, bwd, bwd, st_spec],
        out_specs=[fwd, bwd, st_spec],
        out_shape=[y_shape, y_shape, jax.ShapeDtypeStruct(state.shape, F32)],
        scratch_shapes=[pltpu.VMEM((chains, RWKV_LANES, RWKV_LANES), F32)],
        compiler_params=_cparams("parallel", "arbitrary"),
        name="rwkv_scan",
    )(lw[0], kd[0], a[0], lw[1], kd[1], a[1], r, v, kk, r, v, kk, state)


def _exact_dot(a, b):
    return jnp.dot(a, b, preferred_element_type=F32, precision=lax.Precision.HIGHEST)


def _head_sum_matrix(width, value):
    r = lax.broadcasted_iota(jnp.int32, (width, width), 0) // HEAD_DIM
    c = lax.broadcasted_iota(jnp.int32, (width, width), 1) // HEAD_DIM
    return jnp.where(r == c, value, 0.0).astype(F32)


def _sigmoid(x):
    return 1.0 / (1.0 + jnp.exp(-x))


def _rwkv_prep_kernel(z_ref, zp_ref, zn_ref, mu_ref, kk_ref, ka_ref, w0_ref, w2_ref, a0_ref, a2_ref, g2_ref,
                      r_ref, v_ref, kn_ref, g_ref, lw0_ref, kd0_ref, av0_ref, lw1_ref, kd1_ref, av1_ref):
    i = pl.program_id(1)
    z = z_ref[0]
    tm = z.shape[0]
    halo = zp_ref.shape[1]
    row = lax.broadcasted_iota(jnp.int32, (tm, 1), 0)
    before = jnp.where(i > 0, zp_ref[0, halo - 1:halo, :], 0.0)
    after = jnp.where(i < pl.num_programs(1) - 1, zn_ref[0, 0:1, :], 0.0)
    z_prev = jnp.where(row == 0, before, pltpu.roll(z, 1, 0))
    z_next = jnp.where(row == tm - 1, after, pltpu.roll(z, tm - 1, 0))
    zs = z + (0.5 * (z_prev + z_next) - z) * mu_ref[...]

    r = zs[:, :D_W]
    k = zs[:, D_W:2 * D_W]
    v = zs[:, 2 * D_W:3 * D_W]
    o = 3 * D_W
    wd = zs[:, o:o + DECAY_LORA]
    ad = zs[:, o + DECAY_LORA:o + DECAY_LORA + ICLR_LORA]
    gd = zs[:, o + DECAY_LORA + ICLR_LORA:]
    r_ref[0] = r
    v_ref[0] = v
    g_ref[0] = _exact_dot(_sigmoid(gd), g2_ref[...])
    kraw = k * kk_ref[...]
    ss = _exact_dot(kraw * kraw, _head_sum_matrix(D_W, 1.0))
    kn_ref[0] = kraw / jnp.maximum(jnp.sqrt(ss), 1e-12)
    twd = jnp.tanh(wd)
    for d, (lw_ref, kd_ref, av_ref) in enumerate(((lw0_ref, kd0_ref, av0_ref), (lw1_ref, kd1_ref, av1_ref))):
        x = -(w0_ref[d:d + 1, :] + _exact_dot(twd, w2_ref[d]))
        softplus = jnp.maximum(x, 0.0) + jnp.log(1.0 + jnp.exp(-jnp.abs(x)))
        lw_ref[0] = -jnp.exp(-softplus - 0.5)
        a = _sigmoid(a0_ref[d:d + 1, :] + _exact_dot(ad, a2_ref[d]))
        av_ref[0] = a
        kd_ref[0] = k * (1.0 + (a - 1.0) * ka_ref[...])


def rwkv_prep(zd, mu, k_k, k_a, w0, w2, a0, a2, g2):
    b, length, width = zd.shape
    tm = _pick_tile(length, 512)
    halo = 8
    nb = length // halo
    per = tm // halo
    row = pl.BlockSpec((1, tm, D_W), lambda bi, i: (bi, i, 0))
    shape = jax.ShapeDtypeStruct((b, length, D_W), F32)

    def full(a):
        return pl.BlockSpec(a.shape, lambda bi, i: (0,) * a.ndim)

    params = (mu, k_k, k_a, w0, w2, a0, a2, g2)
    return pl.pallas_call(
        _rwkv_prep_kernel,
        grid=(b, length // tm),
        in_specs=[
            pl.BlockSpec((1, tm, width), lambda bi, i: (bi, i, 0)),
            pl.BlockSpec((1, halo, width), lambda bi, i: (bi, jnp.maximum(i * per - 1, 0), 0)),
            pl.BlockSpec((1, halo, width), lambda bi, i: (bi, jnp.minimum((i + 1) * per, nb - 1), 0)),
        ] + [full(p) for p in params],
        out_specs=[row] * 10,
        out_shape=[shape] * 10,
        compiler_params=_cparams("parallel", "parallel"),
        name="rwkv_prep",
    )(zd, zd, zd, *params)


def _rwkv_readout_kernel(yf_ref, yb_ref, r_ref, v_ref, kd0_ref, kd1_ref, g_ref, rk_ref, lnw_ref, lnb_ref, o_ref):
    y = yf_ref[0] + yb_ref[0]
    avg = _head_sum_matrix(D_W, 1.0 / HEAD_DIM)
    mean = _exact_dot(y, avg)
    yc = y - mean
    var = _exact_dot(yc * yc, avg)
    yn = yc * lax.rsqrt(var + GN_EPS) * lnw_ref[...] + lnb_ref[...]
    bonus = _exact_dot(r_ref[0] * (kd0_ref[0] + kd1_ref[0]) * rk_ref[...], _head_sum_matrix(D_W, 1.0))
    o_ref[0] = ((yn + bonus * v_ref[0]) * g_ref[0]).astype(o_ref.dtype)


def rwkv_readout(yf, yb, r, v, kd0, kd1, g, r_k, ln_w, ln_b):
    b, length, w = yf.shape
    tm = _pick_tile(length, 512)
    row = pl.BlockSpec((1, tm, w), lambda bi, i: (bi, i, 0))
    vec = pl.BlockSpec((1, w), lambda bi, i: (0, 0))
    return pl.pallas_call(
        _rwkv_readout_kernel,
        grid=(b, length // tm),
        in_specs=[row] * 7 + [vec] * 3,
        out_specs=row,
        out_shape=jax.ShapeDtypeStruct((b, length, w), BF16),
        compiler_params=_cparams("parallel", "parallel"),
        name="rwkv_readout",
    )(yf, yb, r, v, kd0, kd1, g, r_k, ln_w, ln_b)


def _rope_tables(n_tokens):
    t = jnp.arange(n_tokens, dtype=jnp.int32)
    row = (t // GRID_W).astype(F32)
    col = (t % GRID_W).astype(F32)
    inv = ROPE_THETA ** (-jnp.arange(ROPE_FREQS, dtype=F32) / ROPE_FREQS)
    ang = jnp.concatenate([row[:, None] * inv, col[:, None] * inv], axis=-1)
    cos, sin = jnp.cos(ang), jnp.sin(ang)
    return jnp.concatenate([cos, cos], axis=-1), jnp.concatenate([sin, sin], axis=-1)


def _rwkv_mixer(zd, zdc, params, r_k, ln_w, ln_b, need_ctx):
    lat = rwkv_prep(zd, *params)
    cx = rwkv_prep(zdc, *params)
    bsz = zd.shape[0]
    chains = 2 * (D_W // RWKV_LANES)
    zero = jnp.zeros((bsz, chains, RWKV_LANES, RWKV_LANES), F32)

    def scan(p, state):
        r, v, kk, _, lw0, kd0, a0, lw1, kd1, a1 = p
        return rwkv_scan((lw0, lw1), (kd0, kd1), (a0, a1), r, v, kk, state)

    def readout(p, yf, yb):
        r, v, _, g, _, kd0, _, _, kd1, _ = p
        return rwkv_readout(yf, yb, r, v, kd0, kd1, g, r_k, ln_w, ln_b)

    ycf, ycb, state = scan(cx, zero)
    yf, yb, _ = scan(lat, state)
    y_d = readout(lat, yf, yb)
    yc_d = readout(cx, ycf, ycb) if need_ctx else None
    return y_d, yc_d


def kernel(x, c, ctx, c_ctx, w_mod, b_mod, norm_mix, norm_ffn, w_in_even, w_out_even, q_norm_a, k_norm_a,
           sink_b, w_in_odd, w_out_odd, rpb_c, shift_mu, decay_w0, decay_w2, iclr_a0, iclr_a2, gate_g2,
           k_k, k_a, r_k, ln_x_w, ln_x_b, w_ffn_in, w_ffn_out, norm_out):
    bsz, s, d = x.shape
    n_ctx = ctx.shape[1]
    cos2, sin2 = _rope_tables(s)
    cos_ctx = jnp.ones((n_ctx, HEAD_DIM), F32)
    sin_ctx = jnp.zeros((n_ctx, HEAD_DIM), F32)
    act = jnp.zeros((8, d), F32).at[:bsz].set(jax.nn.silu(c)).at[bsz].set(jax.nn.silu(c_ctx))
    mods = modulation(act.astype(BF16), w_mod, b_mod)
    w_out_even = w_out_even.astype(BF16)
    w_in_odd = w_in_odd.astype(BF16)
    w_out_odd = w_out_odd.astype(BF16)
    w_ffn_in = w_ffn_in.astype(BF16)
    w_ffn_out = w_ffn_out.astype(BF16)
    no_sink = jnp.full((A_HEADS,), NEG_INF, F32)
    for layer in range(DEPTH):
        need_ctx = layer < DEPTH - 1
        mod = mods[layer, :bsz][:, None, :]
        modc = jnp.broadcast_to(mods[layer, bsz][None, None, :], (bsz, 1, 6 * d))
        sh1, sc1, g1, sh2, sc2, g2 = jnp.split(mod, 6, axis=-1)
        csh1, csc1, cg1, csh2, csc2, cg2 = jnp.split(modc, 6, axis=-1)
        gain_mix = norm_mix[layer][None, :]
        gain_ffn = norm_ffn[layer][None, :]
        i = layer // 2
        if layer % 2 == 0:
            w_out = w_out_even[i]
            qg, kg = q_norm_a[i][None, :], k_norm_a[i][None, :]
            w_ext = even_weight(w_in_even[i], q_norm_a[i], k_norm_a[i])
            w_vat = w_in_even[i][:, _EV_VA:_EV_VA + A_KV].T.astype(BF16)
            qa, ka, vat, qb, kb, vb = even_proj(x, gain_mix, sh1, sc1, w_ext, w_vat, cos2, sin2, qg, kg)
            qca, kca, vcat, qcb, kcb, vcb = even_proj(ctx, gain_mix, csh1, csc1, w_ext, w_vat,
                                                       cos_ctx, sin_ctx, qg, kg)
            y1 = global_attention(qa, ka, vat, kca, vcat)
            y2 = window_attention(qb, kb, vb, kcb, vcb, sink_b[i])
            if need_ctx:
                yc1 = ctx_attention(qca, kca, vcat, no_sink, v_transposed=True)
                yc2 = ctx_attention(qcb, kcb, vcb, sink_b[i])
        else:
            w_out = w_out_odd[i]
            q, k, v, zd = odd_proj(x, gain_mix, sh1, sc1, w_in_odd[i])
            qc, kc, vc, zdc = odd_proj(ctx, gain_mix, csh1, csc1, w_in_odd[i])
            bias = na_bias_tables(rpb_c[i], s // GRID_W)
            y1 = neighborhood_attention(q, k, v, kc, vc, bias)
            params = (shift_mu[i][None, :], k_k[i][None, :], k_a[i][None, :], decay_w0[i], decay_w2[i],
                      iclr_a0[i], iclr_a2[i], gate_g2[i])
            y2, yc2 = _rwkv_mixer(zd, zdc, params, r_k[i].reshape(1, D_W), ln_x_w[i][None, :],
                                  ln_x_b[i][None, :], need_ctx)
            if need_ctx:
                yc1 = ctx_attention(qc, kc, vc, no_sink)
        x = out_proj(x, g1, y1, y2, w_out)
        x = ffn(x, gain_ffn, sh2, sc2, g2, w_ffn_in[layer], w_ffn_out[layer])
        if need_ctx:
            ctx = out_proj(ctx, cg1, yc1, yc2, w_out)
            ctx = ffn(ctx, gain_ffn, csh2, csc2, cg2, w_ffn_in[layer], w_ffn_out[layer])
    return final_norm(x, norm_out[None, :])
```

```python
import functools

import jax
import jax.numpy as jnp
import numpy as np
from jax import lax
from jax.experimental import pallas as pl
from jax.experimental.pallas import tpu as pltpu

F32 = jnp.float32
BF16 = jnp.bfloat16

D_MODEL = 1024
DEPTH = 4
GRID_W = 64
HEAD_DIM = 64
SCALE = HEAD_DIM ** -0.5
ROPE_HALF = HEAD_DIM // 2
ROPE_FREQS = HEAD_DIM // 4
ROPE_THETA = 10000.0
A_HEADS = 8
A_KV_HEADS = 2
B_HEADS = 8
B_KV_HEADS = 2
WINDOW = 128
C_HEADS = 8
NA_ROWS = 8
NA_COLS = 16
D_HEADS = 8
DECAY_LORA = 64
ICLR_LORA = 64
GATE_LORA = 128
GN_EPS = 64e-5
RMS_EPS = 1e-6
NEG_INF = -1e30
FFN_HIDDEN = -(-8 * D_MODEL // (3 * 256)) * 256

A_Q = A_HEADS * HEAD_DIM
A_KV = A_KV_HEADS * HEAD_DIM
B_Q = B_HEADS * HEAD_DIM
B_KV = B_KV_HEADS * HEAD_DIM
C_W = C_HEADS * HEAD_DIM
D_W = D_HEADS * HEAD_DIM
EVEN_WIDTHS = (A_Q, A_KV, A_KV, B_Q, B_KV, B_KV)
D_SHIFT_W = 3 * D_W + DECAY_LORA + ICLR_LORA + GATE_LORA

VMEM_LIMIT_BYTES = 56 * 1024 * 1024
RWKV_CHUNK = HEAD_DIM
RWKV_INV_BLOCK = 16
RWKV_LANES = 256
FLASH_SUB_KEYS = 512
LOG2E = 1.4426950408889634
VT_ROWS = HEAD_DIM + 16
NA_TILE_ROWS = 8
NA_KEY_ROWS = NA_TILE_ROWS + NA_ROWS - 1


def _cparams(*sem):
    return pltpu.CompilerParams(dimension_semantics=sem, vmem_limit_bytes=VMEM_LIMIT_BYTES)


def _split(t, widths):
    return jnp.split(t, [int(o) for o in np.cumsum(widths)[:-1]], axis=-1)


def _pick_tile(n, target):
    t = min(n, target)
    while n % t:
        t //= 2
    return t


def _mod_kernel(a_ref, w_ref, b_ref, o_ref):
    w = w_ref[0].astype(BF16)
    o_ref[0] = jnp.dot(a_ref[...], w, preferred_element_type=F32) + b_ref[0]


def modulation(act, w_mod, b_mod):
    depth, d, n = w_mod.shape
    tn = 1536
    return pl.pallas_call(
        _mod_kernel,
        grid=(depth, n // tn),
        in_specs=[
            pl.BlockSpec((8, d), lambda l, j: (0, 0)),
            pl.BlockSpec((1, d, tn), lambda l, j: (l, 0, j)),
            pl.BlockSpec((1, 1, tn), lambda l, j: (l, 0, j)),
        ],
        out_specs=pl.BlockSpec((1, 8, tn), lambda l, j: (l, 0, j)),
        out_shape=jax.ShapeDtypeStruct((depth, 8, n), F32),
        compiler_params=_cparams("parallel", "parallel"),
        name="modulation",
    )(act, w_mod, b_mod.reshape(depth, 1, n))


def _norm_mod(x, gain, shift, scale):
    ms = jnp.mean(x * x, axis=-1, keepdims=True)
    h = x * lax.rsqrt(ms + RMS_EPS) * gain
    return h * (1.0 + scale) + shift


def _head_rms(z):
    return lax.rsqrt(jnp.mean(z * z, axis=-1, keepdims=True) + RMS_EPS)


def _nt_dot(a, b):
    return lax.dot_general(a, b, (((1,), (1,)), ((), ())), preferred_element_type=F32)


_EV_QA, _EV_KA, _EV_VA, _EV_QB, _EV_KB, _EV_VB = (int(o) for o in np.cumsum((0,) + EVEN_WIDTHS)[:-1])
_EV_QAR = sum(EVEN_WIDTHS)
_EV_KAR = _EV_QAR + A_Q
_EV_QBR = _EV_KAR + A_KV
_EV_KBR = _EV_QBR + B_Q
EVEN_EXT = _EV_KBR + B_KV


def _even_proj_kernel(x_ref, gain_ref, sh_ref, sc_ref, w_ref, wvt_ref, cos_ref, sin_ref, qg_ref, kg_ref,
                      qa_ref, ka_ref, vat_ref, qb_ref, kb_ref, vb_ref):
    h = _norm_mod(x_ref[0], gain_ref[...], sh_ref[0], sc_ref[0]).astype(BF16)
    z = jnp.dot(h, w_ref[...], preferred_element_type=F32)
    vat = _nt_dot(wvt_ref[...], h)
    cos = cos_ref[...]
    sin = sin_ref[...]
    qg = qg_ref[...]
    kg = kg_ref[...]

    def sl(off, i):
        return z[:, off + i * HEAD_DIM: off + (i + 1) * HEAD_DIM]

    for i in range(A_HEADS):
        zq = sl(_EV_QA, i)
        qa = _head_rms(zq) * (zq * qg * cos + sl(_EV_QAR, i) * sin)
        qa_ref[0, i] = (qa * (SCALE * LOG2E)).astype(BF16)
    for i in range(B_HEADS):
        qb_ref[0, i] = ((sl(_EV_QB, i) * cos + sl(_EV_QBR, i) * sin) * SCALE).astype(BF16)
    for i in range(A_KV_HEADS):
        zk = sl(_EV_KA, i)
        ka_ref[0, i] = (_head_rms(zk) * (zk * kg * cos + sl(_EV_KAR, i) * sin)).astype(BF16)
        vat_ref[0, i, :HEAD_DIM, :] = vat[i * HEAD_DIM:(i + 1) * HEAD_DIM].astype(BF16)
        pad = VT_ROWS - HEAD_DIM
        ones_row = lax.broadcasted_iota(jnp.int32, (pad, vat.shape[1]), 0) == 0
        vat_ref[0, i, HEAD_DIM:, :] = ones_row.astype(BF16)
    for i in range(B_KV_HEADS):
        kb_ref[0, i] = (sl(_EV_KB, i) * cos + sl(_EV_KBR, i) * sin).astype(BF16)
        vb_ref[0, i] = sl(_EV_VB, i).astype(BF16)


def _rot_cols(w):
    d = w.shape[0]
    wh = w.reshape(d, -1, 2, ROPE_HALF)
    return jnp.stack([-wh[:, :, 1], wh[:, :, 0]], axis=2).reshape(w.shape)


def even_weight(w_in, q_gain, k_gain):
    qa, ka, va, qb, kb, vb = _split(w_in, EVEN_WIDTHS)
    qar = _rot_cols(qa * jnp.tile(q_gain, A_HEADS))
    kar = _rot_cols(ka * jnp.tile(k_gain, A_KV_HEADS))
    return jnp.concatenate([w_in, qar, kar, _rot_cols(qb), _rot_cols(kb)], axis=1).astype(BF16)


def even_proj(x, gain, shift, scale, w_ext, w_vat, cos2, sin2, q_gain, k_gain):
    b, s, d = x.shape
    tm = _pick_tile(s, 512)
    vec = pl.BlockSpec((1, 1, d), lambda bi, i: (bi, 0, 0))
    tab = pl.BlockSpec((tm, HEAD_DIM), lambda bi, i: (i, 0))
    hv = pl.BlockSpec((1, HEAD_DIM), lambda bi, i: (0, 0))

    def hm(n):
        return pl.BlockSpec((1, n, tm, HEAD_DIM), lambda bi, i: (bi, 0, i, 0))

    def shape(n):
        return jax.ShapeDtypeStruct((b, n, s, HEAD_DIM), BF16)

    return pl.pallas_call(
        _even_proj_kernel,
        grid=(b, s // tm),
        in_specs=[
            pl.BlockSpec((1, tm, d), lambda bi, i: (bi, i, 0)),
            pl.BlockSpec((1, d), lambda bi, i: (0, 0)),
            vec, vec,
            pl.BlockSpec((d, EVEN_EXT), lambda bi, i: (0, 0)),
            pl.BlockSpec((A_KV, d), lambda bi, i: (0, 0)),
            tab, tab, hv, hv,
        ],
        out_specs=[hm(A_HEADS), hm(A_KV_HEADS),
                   pl.BlockSpec((1, A_KV_HEADS, VT_ROWS, tm), lambda bi, i: (bi, 0, 0, i)),
                   hm(B_HEADS), hm(B_KV_HEADS), hm(B_KV_HEADS)],
        out_shape=[shape(A_HEADS), shape(A_KV_HEADS),
                   jax.ShapeDtypeStruct((b, A_KV_HEADS, VT_ROWS, s), BF16),
                   shape(B_HEADS), shape(B_KV_HEADS), shape(B_KV_HEADS)],
        compiler_params=_cparams("parallel", "parallel"),
        name="even_proj",
    )(x, gain, shift, scale, w_ext, w_vat, cos2, sin2, q_gain, k_gain)


def _odd_proj_kernel(x_ref, gain_ref, sh_ref, sc_ref, w_ref, q_ref, k_ref, v_ref, zd_ref):
    h = _norm_mod(x_ref[0], gain_ref[...], sh_ref[0], sc_ref[0])
    z = jnp.dot(h.astype(BF16), w_ref[...], preferred_element_type=F32)
    for i in range(C_HEADS):
        lo = i * HEAD_DIM
        q_ref[0, i] = (z[:, lo:lo + HEAD_DIM] * SCALE).astype(BF16)
        k_ref[0, i] = z[:, C_W + lo:C_W + lo + HEAD_DIM].astype(BF16)
        v_ref[0, i] = z[:, 2 * C_W + lo:2 * C_W + lo + HEAD_DIM].astype(BF16)
    zd_ref[0] = z[:, 3 * C_W:]


def odd_proj(x, gain, shift, scale, w):
    b, s, d = x.shape
    n = w.shape[1]
    tm = _pick_tile(s, 512)
    vec = pl.BlockSpec((1, 1, d), lambda bi, i: (bi, 0, 0))
    hm = pl.BlockSpec((1, C_HEADS, tm, HEAD_DIM), lambda bi, i: (bi, 0, i, 0))
    hshape = jax.ShapeDtypeStruct((b, C_HEADS, s, HEAD_DIM), BF16)
    return pl.pallas_call(
        _odd_proj_kernel,
        grid=(b, s // tm),
        in_specs=[
            pl.BlockSpec((1, tm, d), lambda bi, i: (bi, i, 0)),
            pl.BlockSpec((1, d), lambda bi, i: (0, 0)),
            vec, vec,
            pl.BlockSpec((d, n), lambda bi, i: (0, 0)),
        ],
        out_specs=[hm, hm, hm, pl.BlockSpec((1, tm, D_SHIFT_W), lambda bi, i: (bi, i, 0))],
        out_shape=[hshape, hshape, hshape, jax.ShapeDtypeStruct((b, s, D_SHIFT_W), F32)],
        compiler_params=_cparams("parallel", "parallel"),
        name="odd_proj",
    )(x, gain, shift, scale, w)


def _out_proj_kernel(x_ref, g_ref, y1_ref, y2_ref, w1_ref, w2_ref, o_ref, *, y2_heads):
    acc = jnp.dot(y1_ref[0, 0], w1_ref[0], preferred_element_type=F32)
    for h in range(1, y1_ref.shape[1]):
        acc = acc + jnp.dot(y1_ref[0, h], w1_ref[h], preferred_element_type=F32)
    if y2_heads:
        for h in range(y2_ref.shape[1]):
            acc = acc + jnp.dot(y2_ref[0, h], w2_ref[h], preferred_element_type=F32)
    else:
        acc = acc + jnp.dot(y2_ref[0], w2_ref[...], preferred_element_type=F32)
    o_ref[0] = x_ref[0] + g_ref[0] * acc


def out_proj(x, gate, y1, y2, w):
    b, s, d = x.shape
    nh = y1.shape[1]
    half = nh * HEAD_DIM
    tm = _pick_tile(s, 1024)
    y2_heads = y2.ndim == 4
    hm = pl.BlockSpec((1, nh, tm, HEAD_DIM), lambda bi, i: (bi, 0, i, 0))
    wh = pl.BlockSpec((nh, HEAD_DIM, d), lambda bi, i: (0, 0, 0))
    w1 = w[:half].reshape(nh, HEAD_DIM, d)
    if y2_heads:
        y2_spec, w2_spec, w2 = hm, wh, w[half:].reshape(nh, HEAD_DIM, d)
    else:
        y2_spec = pl.BlockSpec((1, tm, half), lambda bi, i: (bi, i, 0))
        w2_spec, w2 = pl.BlockSpec((half, d), lambda bi, i: (0, 0)), w[half:]
    return pl.pallas_call(
        functools.partial(_out_proj_kernel, y2_heads=y2_heads),
        grid=(b, s // tm),
        in_specs=[
            pl.BlockSpec((1, tm, d), lambda bi, i: (bi, i, 0)),
            pl.BlockSpec((1, 1, d), lambda bi, i: (bi, 0, 0)),
            hm, y2_spec, wh, w2_spec,
        ],
        out_specs=pl.BlockSpec((1, tm, d), lambda bi, i: (bi, i, 0)),
        out_shape=jax.ShapeDtypeStruct((b, s, d), F32),
        compiler_params=_cparams("parallel", "parallel"),
        name="out_proj",
    )(x, gate, y1, y2, w1, w2)


def _ffn_kernel(x_ref, gain_ref, sh_ref, sc_ref, g_ref, wg_ref, wu_ref, wo_ref, o_ref, h_ref, acc_ref):
    j = pl.program_id(2)

    @pl.when(j == 0)
    def _():
        h_ref[...] = _norm_mod(x_ref[0], gain_ref[...], sh_ref[0], sc_ref[0]).astype(BF16)
        acc_ref[...] = jnp.zeros_like(acc_ref)

    h = h_ref[...]
    gate = jnp.dot(h, wg_ref[...], preferred_element_type=F32)
    up = jnp.dot(h, wu_ref[...], preferred_element_type=F32)
    act = gate * (1.0 / (1.0 + jnp.exp(-gate))) * up
    acc_ref[...] += jnp.dot(act.astype(BF16), wo_ref[...], preferred_element_type=F32)

    @pl.when(j == pl.num_programs(2) - 1)
    def _():
        o_ref[0] = x_ref[0] + g_ref[0] * acc_ref[...]


def ffn(x, gain, shift, scale, gate, w_in, w_out):
    b, s, d = x.shape
    f = w_out.shape[0]
    tm = _pick_tile(s, 512)
    th = 1408 if f % 1408 == 0 else _pick_tile(f, 256)
    nj = f // th
    vec = pl.BlockSpec((1, 1, d), lambda bi, i, j: (bi, 0, 0))
    return pl.pallas_call(
        _ffn_kernel,
        grid=(b, s // tm, nj),
        in_specs=[
            pl.BlockSpec((1, tm, d), lambda bi, i, j: (bi, i, 0)),
            pl.BlockSpec((1, d), lambda bi, i, j: (0, 0)),
            vec, vec, vec,
            pl.BlockSpec((d, th), lambda bi, i, j: (0, j)),
            pl.BlockSpec((d, th), lambda bi, i, j: (0, j + nj)),
            pl.BlockSpec((th, d), lambda bi, i, j: (j, 0)),
        ],
        out_specs=pl.BlockSpec((1, tm, d), lambda bi, i, j: (bi, i, 0)),
        out_shape=jax.ShapeDtypeStruct((b, s, d), F32),
        scratch_shapes=[pltpu.VMEM((tm, d), BF16), pltpu.VMEM((tm, d), F32)],
        compiler_params=_cparams("parallel", "parallel", "arbitrary"),
        name="ffn",
    )(x, gain, shift, scale, gate, w_in, w_in, w_out)


def _rms_kernel(x_ref, gain_ref, o_ref):
    x = x_ref[0]
    ms = jnp.mean(x * x, axis=-1, keepdims=True)
    o_ref[0] = x * lax.rsqrt(ms + RMS_EPS) * gain_ref[...]


def final_norm(x, gain):
    b, s, d = x.shape
    tm = _pick_tile(s, 1024)
    return pl.pallas_call(
        _rms_kernel,
        grid=(b, s // tm),
        in_specs=[pl.BlockSpec((1, tm, d), lambda bi, i: (bi, i, 0)),
                  pl.BlockSpec((1, d), lambda bi, i: (0, 0))],
        out_specs=pl.BlockSpec((1, tm, d), lambda bi, i: (bi, i, 0)),
        out_shape=jax.ShapeDtypeStruct((b, s, d), F32),
        compiler_params=_cparams("parallel", "parallel"),
        name="final_norm",
    )(x, gain)


def _flash_kernel(q_ref, k_ref, vt_ref, kx_ref, vxt_ref, o_ref, m_ref, acc_ref):
    kv = pl.program_id(3)
    g, tq, hd = q_ref.shape[1:]

    @pl.when(kv == 0)
    def _():
        m_ref[...] = jnp.full_like(m_ref, -jnp.inf)
        acc_ref[...] = jnp.zeros_like(acc_ref)

    q = q_ref[0].reshape(g * tq, hd)

    def scores(k):
        return _nt_dot(k, q)

    def accumulate(s, vt):
        m_prev = m_ref[...]
        m_new = jnp.maximum(m_prev, jnp.max(s, axis=0, keepdims=True))
        p = jnp.exp2(s - m_new).astype(BF16)
        acc_ref[...] = jnp.exp2(m_prev - m_new) * acc_ref[...] + jnp.dot(vt, p, preferred_element_type=F32)
        m_ref[...] = m_new

    tk = k_ref.shape[2]
    sub = min(tk, FLASH_SUB_KEYS)
    nsub = tk // sub
    s_next = scores(k_ref[0, 0, 0:sub, :])
    for c in range(nsub):
        s_cur = s_next
        if c + 1 < nsub:
            s_next = scores(k_ref[0, 0, (c + 1) * sub:(c + 2) * sub, :])
        accumulate(s_cur, vt_ref[0, 0, :, c * sub:(c + 1) * sub])

    @pl.when(kv == pl.num_programs(3) - 1)
    def _():
        accumulate(scores(kx_ref[0, 0]), vxt_ref[0, 0])
        acc = acc_ref[...]
        ot = (acc[:hd] / acc[hd:hd + 1]).astype(BF16)
        eye = (lax.broadcasted_iota(jnp.int32, (tq, tq), 0)
               == lax.broadcasted_iota(jnp.int32, (tq, tq), 1)).astype(BF16)
        for h in range(g):
            o_ref[0, h] = _nt_dot(eye, ot[:, h * tq:(h + 1) * tq]).astype(o_ref.dtype)


def global_attention(q, k, vt, kx, vxt):
    b, h, s, hd = q.shape
    hkv = k.shape[1]
    n_ctx = kx.shape[2]
    vr = vt.shape[2]
    g = h // hkv
    tq = _pick_tile(s, 256)
    tk = _pick_tile(s, 2048)
    q_spec = pl.BlockSpec((1, g, tq, hd), lambda bi, gi, i, j: (bi, gi, i, 0))
    return pl.pallas_call(
        _flash_kernel,
        grid=(b, hkv, s // tq, s // tk),
        in_specs=[
            q_spec,
            pl.BlockSpec((1, 1, tk, hd), lambda bi, gi, i, j: (bi, gi, j, 0)),
            pl.BlockSpec((1, 1, vr, tk), lambda bi, gi, i, j: (bi, gi, 0, j)),
            pl.BlockSpec((1, 1, n_ctx, hd), lambda bi, gi, i, j: (bi, gi, 0, 0)),
            pl.BlockSpec((1, 1, vr, n_ctx), lambda bi, gi, i, j: (bi, gi, 0, 0)),
        ],
        out_specs=q_spec,
        out_shape=jax.ShapeDtypeStruct((b, h, s, hd), BF16),
        scratch_shapes=[pltpu.VMEM((1, g * tq), F32), pltpu.VMEM((vr, g * tq), F32)],
        compiler_params=_cparams("parallel", "parallel", "parallel", "arbitrary"),
        name="global_attention",
    )(q, k, vt, kx, vxt)


def _window_kernel(sink_ref, q_ref, kp_ref, kc_ref, kn_ref, vp_ref, vc_ref, vn_ref, kx_ref, vx_ref, o_ref,
                   *, seq):
    gi = pl.program_id(1)
    i = pl.program_id(2)
    g, tq, hd = q_ref.shape[1:]
    w = kp_ref.shape[2]
    q = q_ref[0].reshape(g * tq, hd)
    k = jnp.concatenate([kp_ref[0, 0], kc_ref[0, 0], kn_ref[0, 0]], axis=0)
    v = jnp.concatenate([vp_ref[0, 0], vc_ref[0, 0], vn_ref[0, 0]], axis=0)
    span = tq + 2 * w
    s_loc = _nt_dot(q, k)
    row = lax.broadcasted_iota(jnp.int32, (g * tq, span), 0)
    col = lax.broadcasted_iota(jnp.int32, (g * tq, span), 1)
    qpos = i * tq + row % tq
    kpos = i * tq - w + col
    valid = (jnp.abs(kpos - qpos) <= w) & (kpos >= 0) & (kpos < seq)
    s_loc = jnp.where(valid, s_loc, NEG_INF)
    s_ctx = _nt_dot(q, kx_ref[0, 0])
    hrow = lax.broadcasted_iota(jnp.int32, (g * tq, 1), 0) // tq
    sink = jnp.zeros((g * tq, 1), F32)
    for hh in range(g):
        sink = jnp.where(hrow == hh, sink_ref[gi * g + hh], sink)
    m = jnp.maximum(jnp.maximum(jnp.max(s_loc, axis=-1, keepdims=True),
                                jnp.max(s_ctx, axis=-1, keepdims=True)), sink)
    p_loc = jnp.exp(s_loc - m)
    p_ctx = jnp.exp(s_ctx - m)
    l = jnp.sum(p_loc, axis=-1, keepdims=True) + jnp.sum(p_ctx, axis=-1, keepdims=True) + jnp.exp(sink - m)
    o = jnp.dot(p_loc.astype(BF16), v, preferred_element_type=F32)
    o = o + jnp.dot(p_ctx.astype(BF16), vx_ref[0, 0], preferred_element_type=F32)
    o_ref[0] = (o / l).reshape(g, tq, hd).astype(o_ref.dtype)


def window_attention(q, k, v, kx, vx, sink):
    b, h, s, hd = q.shape
    hkv = k.shape[1]
    g = h // hkv
    n_ctx = kx.shape[2]
    tq = _pick_tile(s, 256)
    r = tq // WINDOW
    nwb = s // WINDOW
    prev = pl.BlockSpec((1, 1, WINDOW, hd), lambda bi, gi, i: (bi, gi, jnp.maximum(i * r - 1, 0), 0))
    cur = pl.BlockSpec((1, 1, tq, hd), lambda bi, gi, i: (bi, gi, i, 0))
    nxt = pl.BlockSpec((1, 1, WINDOW, hd), lambda bi, gi, i: (bi, gi, jnp.minimum((i + 1) * r, nwb - 1), 0))
    cx = pl.BlockSpec((1, 1, n_ctx, hd), lambda bi, gi, i: (bi, gi, 0, 0))
    return pl.pallas_call(
        functools.partial(_window_kernel, seq=s),
        grid=(b, hkv, s // tq),
        in_specs=[
            pl.BlockSpec(memory_space=pltpu.SMEM),
            pl.BlockSpec((1, g, tq, hd), lambda bi, gi, i: (bi, gi, i, 0)),
            prev, cur, nxt, prev, cur, nxt, cx, cx,
        ],
        out_specs=pl.BlockSpec((1, g, tq, hd), lambda bi, gi, i: (bi, gi, i, 0)),
        out_shape=jax.ShapeDtypeStruct((b, h, s, hd), BF16),
        compiler_params=_cparams("parallel", "parallel", "parallel"),
        name="window_attention",
    )(sink, q, k, k, k, v, v, v, kx, vx)


def _ctx_attn_kernel(sink_ref, q_ref, k_ref, v_ref, o_ref, *, mixer_a):
    hi = pl.program_id(1)
    s = _nt_dot(q_ref[0, 0], k_ref[0, 0])
    sink = sink_ref[hi]
    m = jnp.maximum(jnp.max(s, axis=-1, keepdims=True), sink)
    if mixer_a:
        p = jnp.exp2(s - m)
        l = jnp.sum(p, axis=-1, keepdims=True)
        o = _nt_dot(p.astype(BF16), v_ref[0, 0, :HEAD_DIM, :])
    else:
        p = jnp.exp(s - m)
        l = jnp.sum(p, axis=-1, keepdims=True) + jnp.exp(sink - m)
        o = jnp.dot(p.astype(BF16), v_ref[0, 0], preferred_element_type=F32)
    o_ref[0, 0] = (o / l).astype(o_ref.dtype)


def ctx_attention(q, k, v, sink, mixer_a=False):
    b, h, n, hd = q.shape
    g = h // k.shape[1]
    kv = pl.BlockSpec((1, 1, n, hd), lambda bi, hi: (bi, hi // g, 0, 0))
    vs = pl.BlockSpec((1, 1, v.shape[2], n), lambda bi, hi: (bi, hi // g, 0, 0)) if mixer_a else kv
    qs = pl.BlockSpec((1, 1, n, hd), lambda bi, hi: (bi, hi, 0, 0))
    return pl.pallas_call(
        functools.partial(_ctx_attn_kernel, mixer_a=mixer_a),
        grid=(b, h),
        in_specs=[pl.BlockSpec(memory_space=pltpu.SMEM), qs, kv, vs],
        out_specs=qs,
        out_shape=jax.ShapeDtypeStruct((b, h, n, hd), BF16),
        compiler_params=_cparams("parallel", "parallel"),
        name="ctx_attention",
    )(sink, q, k, v)


def _na_kernel(q_ref, k_ref, v_ref, kx_ref, vx_ref, bias_ref, o_ref, *, rows):
    t = pl.program_id(2)
    nkeys = NA_KEY_ROWS * GRID_W
    kstart = jnp.clip(t * NA_TILE_ROWS - NA_ROWS // 2, 0, rows - NA_KEY_ROWS)
    off = pl.multiple_of(kstart * GRID_W, GRID_W)
    q = q_ref[0, 0]
    kw = k_ref[0, 0, pl.ds(off, nkeys), :]
    vw = v_ref[0, 0, pl.ds(off, nkeys), :]
    s_loc = _nt_dot(q, kw) + bias_ref[0, 0]
    s_ctx = _nt_dot(q, kx_ref[0, 0])
    m = jnp.maximum(jnp.max(s_loc, axis=-1, keepdims=True), jnp.max(s_ctx, axis=-1, keepdims=True))
    p_loc = jnp.exp(s_loc - m)
    p_ctx = jnp.exp(s_ctx - m)
    l = jnp.sum(p_loc, axis=-1, keepdims=True) + jnp.sum(p_ctx, axis=-1, keepdims=True)
    o = jnp.dot(p_loc.astype(BF16), vw, preferred_element_type=F32)
    o = o + jnp.dot(p_ctx.astype(BF16), vx_ref[0, 0], preferred_element_type=F32)
    o_ref[0, 0] = (o / l).astype(o_ref.dtype)


def na_bias_tables(rpb, rows):
    nh = rpb.shape[0]
    i = np.arange(NA_TILE_ROWS)
    j = np.arange(NA_KEY_ROWS)
    qc = np.arange(GRID_W)
    kc = np.arange(GRID_W)
    cs = np.clip(qc - NA_COLS // 2, 0, GRID_W - NA_COLS)
    col_valid = (kc[None, :] >= cs[:, None]) & (kc[None, :] < cs[:, None] + NA_COLS)
    dcol = np.clip(kc[None, :] - qc[:, None] + NA_COLS - 1, 0, 2 * NA_COLS - 2)
    per_drow = jnp.where(col_valid[None, None], rpb.astype(F32)[:, :, dcol], NEG_INF)
    masked = jnp.full((nh, 1, GRID_W, GRID_W), NEG_INF, F32)
    per_drow = jnp.concatenate([per_drow, masked], axis=1)
    out = []
    for r0 in (0, NA_TILE_ROWS, rows - NA_TILE_ROWS):
        r = r0 + i
        rs = np.clip(r - NA_ROWS // 2, 0, rows - NA_ROWS)
        kstart = int(np.clip(r0 - NA_ROWS // 2, 0, rows - NA_KEY_ROWS))
        krow = kstart + j
        row_valid = (krow[None, :] >= rs[:, None]) & (krow[None, :] < rs[:, None] + NA_ROWS)
        drow = np.where(row_valid, krow[None, :] - r[:, None] + NA_ROWS - 1, 2 * NA_ROWS - 1)
        cols = [per_drow[:, drow[:, jj]] for jj in range(NA_KEY_ROWS)]
        dense = jnp.stack(cols, axis=3)
        out.append(dense.reshape(nh, NA_TILE_ROWS * GRID_W, NA_KEY_ROWS * GRID_W))
    return jnp.stack(out)


def neighborhood_attention(q, k, v, kx, vx, bias):
    b, h, s, hd = q.shape
    rows = s // GRID_W
    nt = rows // NA_TILE_ROWS
    tq = NA_TILE_ROWS * GRID_W
    nkeys = NA_KEY_ROWS * GRID_W
    n_ctx = kx.shape[2]
    full = pl.BlockSpec((1, 1, s, hd), lambda bi, hi, t: (bi, hi, 0, 0))
    cx = pl.BlockSpec((1, 1, n_ctx, hd), lambda bi, hi, t: (bi, hi, 0, 0))

    def case(t):
        return jnp.where(t == 0, 0, jnp.where(t == nt - 1, 2, 1))

    return pl.pallas_call(
        functools.partial(_na_kernel, rows=rows),
        grid=(b, h, nt),
        in_specs=[
            pl.BlockSpec((1, 1, tq, hd), lambda bi, hi, t: (bi, hi, t, 0)),
            full, full, cx, cx,
            pl.BlockSpec((1, 1, tq, nkeys), lambda bi, hi, t: (case(t), hi, 0, 0)),
        ],
        out_specs=pl.BlockSpec((1, 1, tq, hd), lambda bi, hi, t: (bi, hi, t, 0)),
        out_shape=jax.ShapeDtypeStruct((b, h, s, hd), BF16),
        compiler_params=_cparams("parallel", "parallel", "arbitrary"),
        name="neighborhood_attention",
    )(q, k, v, kx, vx, bias)


def _mm(a, b, exact=False):
    if exact:
        return jnp.dot(a, b, preferred_element_type=F32, precision=lax.Precision.HIGHEST)
    return jnp.dot(a.astype(BF16), b.astype(BF16), preferred_element_type=F32)


def _mm_nt(a, b):
    return lax.dot_general(a.astype(BF16), b.astype(BF16), (((1,), (1,)), ((), ())),
                           preferred_element_type=F32)


def _mm_tn(a, b):
    return lax.dot_general(a.astype(BF16), b.astype(BF16), (((0,), (0,)), ((), ())),
                           preferred_element_type=F32)


def _rwkv_chains(chains):
    t, w = chains[0][0].shape
    nh = w // HEAD_DIM
    n = range(len(chains))
    lw, kd, a, r, v, kap, st, rev = (list(col) for col in zip(*chains))
    row = lax.broadcasted_iota(jnp.int32, (t, w), 0)
    colj = lax.broadcasted_iota(jnp.int32, (t, w), 1) % t
    lane_head = lax.broadcasted_iota(jnp.int32, (1, w), 1) // HEAD_DIM
    strict_d = (colj < row, colj > row)
    incl_d = (colj <= row, colj >= row)
    tri_d = tuple(m[:, :t].astype(F32) for m in incl_d)
    strict = [strict_d[int(rev[i])] for i in n]
    incl = [incl_d[int(rev[i])] for i in n]

    def bd(y):
        return jnp.concatenate([jnp.where(lane_head == h, y, 0.0) for h in range(nh)], axis=0)

    def bdmm(xs, ys):
        return [_mm(xs[i], bd(ys[i])) for i in n]

    c_incl = [_mm(tri_d[int(rev[i])], lw[i], exact=True) for i in n]
    c_excl = [c_incl[i] - lw[i] for i in n]
    c_mid = [c_incl[i][t // 2:t // 2 + 1, :] for i in n]
    c_end = [c_incl[i][0:1, :] if rev[i] else c_incl[i][t - 1:t, :] for i in n]
    bvec = [kap[i] * a[i] for i in n]
    e_neg = [jnp.exp(c_mid[i] - c_incl[i]) for i in n]
    left = [jnp.concatenate([kap[i] * jnp.exp(c_excl[i] - c_mid[i]), r[i] * jnp.exp(c_incl[i] - c_mid[i])],
                            axis=0) for i in n]
    wt = [jnp.concatenate([bd(bvec[i] * e_neg[i]), bd(kd[i] * e_neg[i])], axis=0) for i in n]
    m = [_mm_nt(left[i], wt[i]) for i in n]
    l_b = [jnp.where(strict[i], m[i][:t, :w], 0.0) for i in n]
    l_k = [jnp.where(strict[i], m[i][:t, w:], 0.0) for i in n]
    m_br = [jnp.where(incl[i], m[i][t:, :w], 0.0) for i in n]
    m_kr = [jnp.where(incl[i], m[i][t:, w:], 0.0) for i in n]

    blk = RWKV_INV_BLOCK
    same = row // blk == colj // blk
    eye = jnp.where(row == colj, 1.0, 0.0)
    pw = [-jnp.where(same, l_b[i], 0.0) for i in n]
    inv = [eye + pw[i] for i in n]
    span = 2
    while span < blk:
        pw = bdmm(pw, pw)
        step = bdmm(inv, pw)
        inv = [inv[i] + step[i] for i in n]
        span *= 2
    while blk < t:
        pair = (row // (2 * blk) == colj // (2 * blk)) & (row // blk != colj // blk)
        off = [jnp.where(pair, l_b[i], 0.0) for i in n]
        step = bdmm(bdmm(inv, off), inv)
        inv = [inv[i] - step[i] for i in n]
        blk *= 2

    kt = [kap[i] * jnp.exp(c_excl[i]) for i in n]
    rt = [r[i] * jnp.exp(c_incl[i]) for i in n]
    lkv = bdmm(l_k, v)
    mkv = bdmm(m_kr, v)
    wk = bdmm(inv, kt)
    u0 = bdmm(inv, lkv)
    mbw = bdmm(m_br, wk)
    mbu = bdmm(m_br, u0)
    rq = [rt[i] - mbw[i] for i in n]
    y0 = [mkv[i] - mbu[i] for i in n]
    e_end = [jnp.exp(c_end[i] - c_incl[i]) for i in n]
    b_hat = [bvec[i] * e_end[i] for i in n]
    k_hat = [kd[i] * e_end[i] for i in n]
    ri = lax.broadcasted_iota(jnp.int32, (w, w), 0)
    ci = lax.broadcasted_iota(jnp.int32, (w, w), 1)
    same_head = ri // HEAD_DIM == ci // HEAD_DIM
    diag = ri == ci
    bw = [_mm_tn(b_hat[i], wk[i]) for i in n]
    kv = [_mm_tn(k_hat[i], v[i]) for i in n]
    bu = [_mm_tn(b_hat[i], u0[i]) for i in n]
    gt = [jnp.where(diag, jnp.exp(c_end[i]), 0.0) - jnp.where(same_head, bw[i], 0.0) for i in n]
    c0t = [jnp.where(same_head, kv[i] - bu[i], 0.0) for i in n]
    ys = [_mm(rq[i], st[i]) + y0[i] for i in n]
    sts = [_mm(gt[i], st[i]) + c0t[i] for i in n]
    return ys, sts


def _rwkv_kernel(lwf_ref, kdf_ref, af_ref, lwb_ref, kdb_ref, ab_ref, rf_ref, vf_ref, kkf_ref,
                 rb_ref, vb_ref, kkb_ref, sin_ref, yf_ref, yb_ref, sout_ref, st_ref):
    c = pl.program_id(0)

    @pl.when(c == 0)
    def _():
        st_ref[...] = sin_ref[...]

    bsz = lwf_ref.shape[0]
    groups = lwf_ref.shape[2] // RWKV_LANES
    chains, slots = [], []
    for bi in range(bsz):
        for d, (lw_ref, kd_ref, a_ref, r_ref, v_ref, kk_ref, y_ref) in enumerate((
                (lwf_ref, kdf_ref, af_ref, rf_ref, vf_ref, kkf_ref, yf_ref),
                (lwb_ref, kdb_ref, ab_ref, rb_ref, vb_ref, kkb_ref, yb_ref))):
            for hg in range(groups):
                lanes = slice(hg * RWKV_LANES, (hg + 1) * RWKV_LANES)
                chain = d * groups + hg
                chains.append((lw_ref[bi, :, lanes], kd_ref[bi, :, lanes], a_ref[bi, :, lanes],
                               r_ref[bi, :, lanes], v_ref[bi, :, lanes], kk_ref[bi, :, lanes],
                               st_ref[bi, chain], bool(d)))
                slots.append((y_ref, bi, lanes, chain))
    ys, sts = _rwkv_chains(chains)
    for (y_ref, bi, lanes, chain), y, st in zip(slots, ys, sts):
        y_ref[bi, :, lanes] = y
        st_ref[bi, chain] = st

    @pl.when(c == pl.num_programs(0) - 1)
    def _():
        sout_ref[...] = st_ref[...]


def rwkv_scan(lw, kd, a, r, v, kk, state):
    b, length, w = r.shape
    t = RWKV_CHUNK
    nc = length // t
    fwd = pl.BlockSpec((b, t, w), lambda c: (0, c, 0))
    bwd = pl.BlockSpec((b, t, w), lambda c: (0, nc - 1 - c, 0))
    st_spec = pl.BlockSpec(state.shape, lambda c: (0, 0, 0, 0))
    y_shape = jax.ShapeDtypeStruct((b, length, w), F32)
    return pl.pallas_call(
        _rwkv_kernel,
        grid=(nc,),
        in_specs=[fwd, fwd, fwd, bwd, bwd, bwd, fwd, fwd, fwd, bwd, bwd, bwd, st_spec],
        out_specs=[fwd, bwd, st_spec],
        out_shape=[y_shape, y_shape, jax.ShapeDtypeStruct(state.shape, F32)],
        scratch_shapes=[pltpu.VMEM(state.shape, F32)],
        compiler_params=_cparams("arbitrary"),
        name="rwkv_scan",
    )(lw[0], kd[0], a[0], lw[1], kd[1], a[1], r, v, kk, r, v, kk, state)


def _exact_dot(a, b):
    return jnp.dot(a, b, preferred_element_type=F32, precision=lax.Precision.HIGHEST)


def _head_sum_matrix(width, value):
    r = lax.broadcasted_iota(jnp.int32, (width, width), 0) // HEAD_DIM
    c = lax.broadcasted_iota(jnp.int32, (width, width), 1) // HEAD_DIM
    return jnp.where(r == c, value, 0.0).astype(F32)


def _sigmoid(x):
    return 1.0 / (1.0 + jnp.exp(-x))


def _rwkv_prep_kernel(z_ref, zp_ref, zn_ref, mu_ref, kk_ref, ka_ref, w0_ref, w2_ref, a0_ref, a2_ref, g2_ref,
                      r_ref, v_ref, kn_ref, g_ref, lw0_ref, kd0_ref, av0_ref, lw1_ref, kd1_ref, av1_ref):
    i = pl.program_id(1)
    z = z_ref[0]
    tm = z.shape[0]
    halo = zp_ref.shape[1]
    row = lax.broadcasted_iota(jnp.int32, (tm, 1), 0)
    before = jnp.where(i > 0, zp_ref[0, halo - 1:halo, :], 0.0)
    after = jnp.where(i < pl.num_programs(1) - 1, zn_ref[0, 0:1, :], 0.0)
    z_prev = jnp.where(row == 0, before, pltpu.roll(z, 1, 0))
    z_next = jnp.where(row == tm - 1, after, pltpu.roll(z, tm - 1, 0))
    zs = z + (0.5 * (z_prev + z_next) - z) * mu_ref[...]

    r = zs[:, :D_W]
    k = zs[:, D_W:2 * D_W]
    v = zs[:, 2 * D_W:3 * D_W]
    o = 3 * D_W
    wd = zs[:, o:o + DECAY_LORA]
    ad = zs[:, o + DECAY_LORA:o + DECAY_LORA + ICLR_LORA]
    gd = zs[:, o + DECAY_LORA + ICLR_LORA:]
    r_ref[0] = r
    v_ref[0] = v
    g_ref[0] = _exact_dot(_sigmoid(gd), g2_ref[...])
    kraw = k * kk_ref[...]
    ss = _exact_dot(kraw * kraw, _head_sum_matrix(D_W, 1.0))
    kn_ref[0] = kraw / jnp.maximum(jnp.sqrt(ss), 1e-12)
    twd = jnp.tanh(wd)
    for d, (lw_ref, kd_ref, av_ref) in enumerate(((lw0_ref, kd0_ref, av0_ref), (lw1_ref, kd1_ref, av1_ref))):
        x = -(w0_ref[d:d + 1, :] + _exact_dot(twd, w2_ref[d]))
        softplus = jnp.maximum(x, 0.0) + jnp.log(1.0 + jnp.exp(-jnp.abs(x)))
        lw_ref[0] = -jnp.exp(-softplus - 0.5)
        a = _sigmoid(a0_ref[d:d + 1, :] + _exact_dot(ad, a2_ref[d]))
        av_ref[0] = a
        kd_ref[0] = k * (1.0 + (a - 1.0) * ka_ref[...])


def rwkv_prep(zd, mu, k_k, k_a, w0, w2, a0, a2, g2):
    b, length, width = zd.shape
    tm = _pick_tile(length, 512)
    halo = 8
    nb = length // halo
    per = tm // halo
    row = pl.BlockSpec((1, tm, D_W), lambda bi, i: (bi, i, 0))
    shape = jax.ShapeDtypeStruct((b, length, D_W), F32)

    def full(a):
        return pl.BlockSpec(a.shape, lambda bi, i: (0,) * a.ndim)

    params = (mu, k_k, k_a, w0, w2, a0, a2, g2)
    return pl.pallas_call(
        _rwkv_prep_kernel,
        grid=(b, length // tm),
        in_specs=[
            pl.BlockSpec((1, tm, width), lambda bi, i: (bi, i, 0)),
            pl.BlockSpec((1, halo, width), lambda bi, i: (bi, jnp.maximum(i * per - 1, 0), 0)),
            pl.BlockSpec((1, halo, width), lambda bi, i: (bi, jnp.minimum((i + 1) * per, nb - 1), 0)),
        ] + [full(p) for p in params],
        out_specs=[row] * 10,
        out_shape=[shape] * 10,
        compiler_params=_cparams("parallel", "parallel"),
        name="rwkv_prep",
    )(zd, zd, zd, *params)


def _rwkv_readout_kernel(yf_ref, yb_ref, r_ref, v_ref, kd0_ref, kd1_ref, g_ref, rk_ref, lnw_ref, lnb_ref, o_ref):
    y = yf_ref[0] + yb_ref[0]
    avg = _head_sum_matrix(D_W, 1.0 / HEAD_DIM)
    mean = _exact_dot(y, avg)
    yc = y - mean
    var = _exact_dot(yc * yc, avg)
    yn = yc * lax.rsqrt(var + GN_EPS) * lnw_ref[...] + lnb_ref[...]
    bonus = _exact_dot(r_ref[0] * (kd0_ref[0] + kd1_ref[0]) * rk_ref[...], _head_sum_matrix(D_W, 1.0))
    o_ref[0] = ((yn + bonus * v_ref[0]) * g_ref[0]).astype(o_ref.dtype)


def rwkv_readout(yf, yb, r, v, kd0, kd1, g, r_k, ln_w, ln_b):
    b, length, w = yf.shape
    tm = _pick_tile(length, 512)
    row = pl.BlockSpec((1, tm, w), lambda bi, i: (bi, i, 0))
    vec = pl.BlockSpec((1, w), lambda bi, i: (0, 0))
    return pl.pallas_call(
        _rwkv_readout_kernel,
        grid=(b, length // tm),
        in_specs=[row] * 7 + [vec] * 3,
        out_specs=row,
        out_shape=jax.ShapeDtypeStruct((b, length, w), BF16),
        compiler_params=_cparams("parallel", "parallel"),
        name="rwkv_readout",
    )(yf, yb, r, v, kd0, kd1, g, r_k, ln_w, ln_b)


def _rope_tables(n_tokens):
    t = jnp.arange(n_tokens, dtype=jnp.int32)
    row = (t // GRID_W).astype(F32)
    col = (t % GRID_W).astype(F32)
    inv = ROPE_THETA ** (-jnp.arange(ROPE_FREQS, dtype=F32) / ROPE_FREQS)
    ang = jnp.concatenate([row[:, None] * inv, col[:, None] * inv], axis=-1)
    cos, sin = jnp.cos(ang), jnp.sin(ang)
    return jnp.concatenate([cos, cos], axis=-1), jnp.concatenate([sin, sin], axis=-1)


def _rwkv_mixer(zd, zdc, params, r_k, ln_w, ln_b, need_ctx):
    lat = rwkv_prep(zd, *params)
    cx = rwkv_prep(zdc, *params)
    bsz = zd.shape[0]
    chains = 2 * (D_W // RWKV_LANES)
    zero = jnp.zeros((bsz, chains, RWKV_LANES, RWKV_LANES), F32)

    def scan(p, state):
        r, v, kk, _, lw0, kd0, a0, lw1, kd1, a1 = p
        return rwkv_scan((lw0, lw1), (kd0, kd1), (a0, a1), r, v, kk, state)

    def readout(p, yf, yb):
        r, v, _, g, _, kd0, _, _, kd1, _ = p
        return rwkv_readout(yf, yb, r, v, kd0, kd1, g, r_k, ln_w, ln_b)

    ycf, ycb, state = scan(cx, zero)
    yf, yb, _ = scan(lat, state)
    y_d = readout(lat, yf, yb)
    yc_d = readout(cx, ycf, ycb) if need_ctx else None
    return y_d, yc_d


def kernel(x, c, ctx, c_ctx, w_mod, b_mod, norm_mix, norm_ffn, w_in_even, w_out_even, q_norm_a, k_norm_a,
           sink_b, w_in_odd, w_out_odd, rpb_c, shift_mu, decay_w0, decay_w2, iclr_a0, iclr_a2, gate_g2,
           k_k, k_a, r_k, ln_x_w, ln_x_b, w_ffn_in, w_ffn_out, norm_out):
    bsz, s, d = x.shape
    n_ctx = ctx.shape[1]
    cos2, sin2 = _rope_tables(s)
    cos_ctx = jnp.ones((n_ctx, HEAD_DIM), F32)
    sin_ctx = jnp.zeros((n_ctx, HEAD_DIM), F32)
    act = jnp.zeros((8, d), F32).at[:bsz].set(jax.nn.silu(c)).at[bsz].set(jax.nn.silu(c_ctx))
    mods = modulation(act.astype(BF16), w_mod, b_mod)
    w_out_even = w_out_even.astype(BF16)
    w_in_odd = w_in_odd.astype(BF16)
    w_out_odd = w_out_odd.astype(BF16)
    w_ffn_in = w_ffn_in.astype(BF16)
    w_ffn_out = w_ffn_out.astype(BF16)
    no_sink = jnp.full((A_HEADS,), NEG_INF, F32)
    for layer in range(DEPTH):
        need_ctx = layer < DEPTH - 1
        mod = mods[layer, :bsz][:, None, :]
        modc = jnp.broadcast_to(mods[layer, bsz][None, None, :], (bsz, 1, 6 * d))
        sh1, sc1, g1, sh2, sc2, g2 = jnp.split(mod, 6, axis=-1)
        csh1, csc1, cg1, csh2, csc2, cg2 = jnp.split(modc, 6, axis=-1)
        gain_mix = norm_mix[layer][None, :]
        gain_ffn = norm_ffn[layer][None, :]
        i = layer // 2
        if layer % 2 == 0:
            w_out = w_out_even[i]
            qg, kg = q_norm_a[i][None, :], k_norm_a[i][None, :]
            w_ext = even_weight(w_in_even[i], q_norm_a[i], k_norm_a[i])
            w_vat = w_in_even[i][:, _EV_VA:_EV_VA + A_KV].T.astype(BF16)
            qa, ka, vat, qb, kb, vb = even_proj(x, gain_mix, sh1, sc1, w_ext, w_vat, cos2, sin2, qg, kg)
            qca, kca, vcat, qcb, kcb, vcb = even_proj(ctx, gain_mix, csh1, csc1, w_ext, w_vat,
                                                       cos_ctx, sin_ctx, qg, kg)
            y1 = global_attention(qa, ka, vat, kca, vcat)
            y2 = window_attention(qb, kb, vb, kcb, vcb, sink_b[i])
            if need_ctx:
                yc1 = ctx_attention(qca, kca, vcat, no_sink, mixer_a=True)
                yc2 = ctx_attention(qcb, kcb, vcb, sink_b[i])
        else:
            w_out = w_out_odd[i]
            q, k, v, zd = odd_proj(x, gain_mix, sh1, sc1, w_in_odd[i])
            qc, kc, vc, zdc = odd_proj(ctx, gain_mix, csh1, csc1, w_in_odd[i])
            bias = na_bias_tables(rpb_c[i], s // GRID_W)
            y1 = neighborhood_attention(q, k, v, kc, vc, bias)
            params = (shift_mu[i][None, :], k_k[i][None, :], k_a[i][None, :], decay_w0[i], decay_w2[i],
                      iclr_a0[i], iclr_a2[i], gate_g2[i])
            y2, yc2 = _rwkv_mixer(zd, zdc, params, r_k[i].reshape(1, D_W), ln_x_w[i][None, :],
                                  ln_x_b[i][None, :], need_ctx)
            if need_ctx:
                yc1 = ctx_attention(qc, kc, vc, no_sink)
        x = out_proj(x, g1, y1, y2, w_out)
        x = ffn(x, gain_ffn, sh2, sc2, g2, w_ffn_in[layer], w_ffn_out[layer])
        if need_ctx:
            ctx = out_proj(ctx, cg1, yc1, yc2, w_out)
            ctx = ffn(ctx, gain_ffn, csh2, csc2, cg2, w_ffn_in[layer], w_ffn_out[layer])
    return final_norm(x, norm_out[None, :])
```

```python
import functools

import jax
import jax.numpy as jnp
import numpy as np
from jax import lax
from jax.experimental import pallas as pl
from jax.experimental.pallas import tpu as pltpu

F32 = jnp.float32
BF16 = jnp.bfloat16

D_MODEL = 1024
DEPTH = 4
GRID_W = 64
HEAD_DIM = 64
SCALE = HEAD_DIM ** -0.5
ROPE_HALF = HEAD_DIM // 2
ROPE_FREQS = HEAD_DIM // 4
ROPE_THETA = 10000.0
A_HEADS = 8
A_KV_HEADS = 2
B_HEADS = 8
B_KV_HEADS = 2
WINDOW = 128
C_HEADS = 8
NA_ROWS = 8
NA_COLS = 16
D_HEADS = 8
DECAY_LORA = 64
ICLR_LORA = 64
GATE_LORA = 128
GN_EPS = 64e-5
RMS_EPS = 1e-6
NEG_INF = -1e30
FFN_HIDDEN = -(-8 * D_MODEL // (3 * 256)) * 256

A_Q = A_HEADS * HEAD_DIM
A_KV = A_KV_HEADS * HEAD_DIM
B_Q = B_HEADS * HEAD_DIM
B_KV = B_KV_HEADS * HEAD_DIM
C_W = C_HEADS * HEAD_DIM
D_W = D_HEADS * HEAD_DIM
EVEN_WIDTHS = (A_Q, A_KV, A_KV, B_Q, B_KV, B_KV)
D_SHIFT_W = 3 * D_W + DECAY_LORA + ICLR_LORA + GATE_LORA

VMEM_LIMIT_BYTES = 56 * 1024 * 1024
RWKV_CHUNK = HEAD_DIM
RWKV_INV_BLOCK = 16
RWKV_LANES = 256
FLASH_SUB_KEYS = 512
LOG2E = 1.4426950408889634
VT_ROWS = HEAD_DIM + 16
NA_TILE_ROWS = 8
NA_KEY_ROWS = NA_TILE_ROWS + NA_ROWS - 1


def _cparams(*sem):
    return pltpu.CompilerParams(dimension_semantics=sem, vmem_limit_bytes=VMEM_LIMIT_BYTES)


def _split(t, widths):
    return jnp.split(t, [int(o) for o in np.cumsum(widths)[:-1]], axis=-1)


def _pick_tile(n, target):
    t = min(n, target)
    while n % t:
        t //= 2
    return t


def _mod_kernel(a_ref, w_ref, b_ref, o_ref):
    w = w_ref[0].astype(BF16)
    o_ref[0] = jnp.dot(a_ref[...], w, preferred_element_type=F32) + b_ref[0]


def modulation(act, w_mod, b_mod):
    depth, d, n = w_mod.shape
    tn = 1536
    return pl.pallas_call(
        _mod_kernel,
        grid=(depth, n // tn),
        in_specs=[
            pl.BlockSpec((8, d), lambda l, j: (0, 0)),
            pl.BlockSpec((1, d, tn), lambda l, j: (l, 0, j)),
            pl.BlockSpec((1, 1, tn), lambda l, j: (l, 0, j)),
        ],
        out_specs=pl.BlockSpec((1, 8, tn), lambda l, j: (l, 0, j)),
        out_shape=jax.ShapeDtypeStruct((depth, 8, n), F32),
        compiler_params=_cparams("parallel", "parallel"),
        name="modulation",
    )(act, w_mod, b_mod.reshape(depth, 1, n))


def _norm_mod(x, gain, shift, scale):
    ms = jnp.mean(x * x, axis=-1, keepdims=True)
    h = x * lax.rsqrt(ms + RMS_EPS) * gain
    return h * (1.0 + scale) + shift


def _head_rms(z):
    return lax.rsqrt(jnp.mean(z * z, axis=-1, keepdims=True) + RMS_EPS)


def _nt_dot(a, b):
    return lax.dot_general(a, b, (((1,), (1,)), ((), ())), preferred_element_type=F32)


_EV_QA, _EV_KA, _EV_VA, _EV_QB, _EV_KB, _EV_VB = (int(o) for o in np.cumsum((0,) + EVEN_WIDTHS)[:-1])
_EV_QAR = sum(EVEN_WIDTHS)
_EV_KAR = _EV_QAR + A_Q
_EV_QBR = _EV_KAR + A_KV
_EV_KBR = _EV_QBR + B_Q
EVEN_EXT = _EV_KBR + B_KV


def _even_proj_kernel(x_ref, gain_ref, sh_ref, sc_ref, w_ref, wvt_ref, cos_ref, sin_ref, qg_ref, kg_ref,
                      qa_ref, ka_ref, vat_ref, qb_ref, kb_ref, vb_ref):
    h = _norm_mod(x_ref[0], gain_ref[...], sh_ref[0], sc_ref[0]).astype(BF16)
    z = jnp.dot(h, w_ref[...], preferred_element_type=F32)
    vat = _nt_dot(wvt_ref[...], h)
    cos = cos_ref[...]
    sin = sin_ref[...]
    qg = qg_ref[...]
    kg = kg_ref[...]

    def sl(off, i):
        return z[:, off + i * HEAD_DIM: off + (i + 1) * HEAD_DIM]

    for i in range(A_HEADS):
        zq = sl(_EV_QA, i)
        qa = _head_rms(zq) * (zq * qg * cos + sl(_EV_QAR, i) * sin)
        qa_ref[0, i] = (qa * (SCALE * LOG2E)).astype(BF16)
    for i in range(B_HEADS):
        qb_ref[0, i] = ((sl(_EV_QB, i) * cos + sl(_EV_QBR, i) * sin) * SCALE).astype(BF16)
    for i in range(A_KV_HEADS):
        zk = sl(_EV_KA, i)
        ka_ref[0, i] = (_head_rms(zk) * (zk * kg * cos + sl(_EV_KAR, i) * sin)).astype(BF16)
        vat_ref[0, i, :HEAD_DIM, :] = vat[i * HEAD_DIM:(i + 1) * HEAD_DIM].astype(BF16)
        pad = VT_ROWS - HEAD_DIM
        ones_row = lax.broadcasted_iota(jnp.int32, (pad, vat.shape[1]), 0) == 0
        vat_ref[0, i, HEAD_DIM:, :] = ones_row.astype(BF16)
    for i in range(B_KV_HEADS):
        kb_ref[0, i] = (sl(_EV_KB, i) * cos + sl(_EV_KBR, i) * sin).astype(BF16)
        vb_ref[0, i] = sl(_EV_VB, i).astype(BF16)


def _rot_cols(w):
    d = w.shape[0]
    wh = w.reshape(d, -1, 2, ROPE_HALF)
    return jnp.stack([-wh[:, :, 1], wh[:, :, 0]], axis=2).reshape(w.shape)


def even_weight(w_in, q_gain, k_gain):
    qa, ka, va, qb, kb, vb = _split(w_in, EVEN_WIDTHS)
    qar = _rot_cols(qa * jnp.tile(q_gain, A_HEADS))
    kar = _rot_cols(ka * jnp.tile(k_gain, A_KV_HEADS))
    return jnp.concatenate([w_in, qar, kar, _rot_cols(qb), _rot_cols(kb)], axis=1).astype(BF16)


def even_proj(x, gain, shift, scale, w_ext, w_vat, cos2, sin2, q_gain, k_gain):
    b, s, d = x.shape
    tm = _pick_tile(s, 512)
    vec = pl.BlockSpec((1, 1, d), lambda bi, i: (bi, 0, 0))
    tab = pl.BlockSpec((tm, HEAD_DIM), lambda bi, i: (i, 0))
    hv = pl.BlockSpec((1, HEAD_DIM), lambda bi, i: (0, 0))

    def hm(n):
        return pl.BlockSpec((1, n, tm, HEAD_DIM), lambda bi, i: (bi, 0, i, 0))

    def shape(n):
        return jax.ShapeDtypeStruct((b, n, s, HEAD_DIM), BF16)

    return pl.pallas_call(
        _even_proj_kernel,
        grid=(b, s // tm),
        in_specs=[
            pl.BlockSpec((1, tm, d), lambda bi, i: (bi, i, 0)),
            pl.BlockSpec((1, d), lambda bi, i: (0, 0)),
            vec, vec,
            pl.BlockSpec((d, EVEN_EXT), lambda bi, i: (0, 0)),
            pl.BlockSpec((A_KV, d), lambda bi, i: (0, 0)),
            tab, tab, hv, hv,
        ],
        out_specs=[hm(A_HEADS), hm(A_KV_HEADS),
                   pl.BlockSpec((1, A_KV_HEADS, VT_ROWS, tm), lambda bi, i: (bi, 0, 0, i)),
                   hm(B_HEADS), hm(B_KV_HEADS), hm(B_KV_HEADS)],
        out_shape=[shape(A_HEADS), shape(A_KV_HEADS),
                   jax.ShapeDtypeStruct((b, A_KV_HEADS, VT_ROWS, s), BF16),
                   shape(B_HEADS), shape(B_KV_HEADS), shape(B_KV_HEADS)],
        compiler_params=_cparams("parallel", "parallel"),
        name="even_proj",
    )(x, gain, shift, scale, w_ext, w_vat, cos2, sin2, q_gain, k_gain)


def _odd_proj_kernel(x_ref, gain_ref, sh_ref, sc_ref, w_ref, q_ref, k_ref, v_ref, zd_ref):
    h = _norm_mod(x_ref[0], gain_ref[...], sh_ref[0], sc_ref[0])
    z = jnp.dot(h.astype(BF16), w_ref[...], preferred_element_type=F32)
    for i in range(C_HEADS):
        lo = i * HEAD_DIM
        q_ref[0, i] = (z[:, lo:lo + HEAD_DIM] * SCALE).astype(BF16)
        k_ref[0, i] = z[:, C_W + lo:C_W + lo + HEAD_DIM].astype(BF16)
        v_ref[0, i] = z[:, 2 * C_W + lo:2 * C_W + lo + HEAD_DIM].astype(BF16)
    zd_ref[0] = z[:, 3 * C_W:]


def odd_proj(x, gain, shift, scale, w):
    b, s, d = x.shape
    n = w.shape[1]
    tm = _pick_tile(s, 512)
    vec = pl.BlockSpec((1, 1, d), lambda bi, i: (bi, 0, 0))
    hm = pl.BlockSpec((1, C_HEADS, tm, HEAD_DIM), lambda bi, i: (bi, 0, i, 0))
    hshape = jax.ShapeDtypeStruct((b, C_HEADS, s, HEAD_DIM), BF16)
    return pl.pallas_call(
        _odd_proj_kernel,
        grid=(b, s // tm),
        in_specs=[
            pl.BlockSpec((1, tm, d), lambda bi, i: (bi, i, 0)),
            pl.BlockSpec((1, d), lambda bi, i: (0, 0)),
            vec, vec,
            pl.BlockSpec((d, n), lambda bi, i: (0, 0)),
        ],
        out_specs=[hm, hm, hm, pl.BlockSpec((1, tm, D_SHIFT_W), lambda bi, i: (bi, i, 0))],
        out_shape=[hshape, hshape, hshape, jax.ShapeDtypeStruct((b, s, D_SHIFT_W), F32)],
        compiler_params=_cparams("parallel", "parallel"),
        name="odd_proj",
    )(x, gain, shift, scale, w)


def _out_proj_kernel(x_ref, g_ref, y1_ref, y2_ref, w1_ref, w2_ref, o_ref, *, y2_heads):
    acc = jnp.dot(y1_ref[0, 0], w1_ref[0], preferred_element_type=F32)
    for h in range(1, y1_ref.shape[1]):
        acc = acc + jnp.dot(y1_ref[0, h], w1_ref[h], preferred_element_type=F32)
    if y2_heads:
        for h in range(y2_ref.shape[1]):
            acc = acc + jnp.dot(y2_ref[0, h], w2_ref[h], preferred_element_type=F32)
    else:
        acc = acc + jnp.dot(y2_ref[0], w2_ref[...], preferred_element_type=F32)
    o_ref[0] = x_ref[0] + g_ref[0] * acc


def out_proj(x, gate, y1, y2, w):
    b, s, d = x.shape
    nh = y1.shape[1]
    half = nh * HEAD_DIM
    tm = _pick_tile(s, 1024)
    y2_heads = y2.ndim == 4
    hm = pl.BlockSpec((1, nh, tm, HEAD_DIM), lambda bi, i: (bi, 0, i, 0))
    wh = pl.BlockSpec((nh, HEAD_DIM, d), lambda bi, i: (0, 0, 0))
    w1 = w[:half].reshape(nh, HEAD_DIM, d)
    if y2_heads:
        y2_spec, w2_spec, w2 = hm, wh, w[half:].reshape(nh, HEAD_DIM, d)
    else:
        y2_spec = pl.BlockSpec((1, tm, half), lambda bi, i: (bi, i, 0))
        w2_spec, w2 = pl.BlockSpec((half, d), lambda bi, i: (0, 0)), w[half:]
    return pl.pallas_call(
        functools.partial(_out_proj_kernel, y2_heads=y2_heads),
        grid=(b, s // tm),
        in_specs=[
            pl.BlockSpec((1, tm, d), lambda bi, i: (bi, i, 0)),
            pl.BlockSpec((1, 1, d), lambda bi, i: (bi, 0, 0)),
            hm, y2_spec, wh, w2_spec,
        ],
        out_specs=pl.BlockSpec((1, tm, d), lambda bi, i: (bi, i, 0)),
        out_shape=jax.ShapeDtypeStruct((b, s, d), F32),
        compiler_params=_cparams("parallel", "parallel"),
        name="out_proj",
    )(x, gate, y1, y2, w1, w2)


def _ffn_kernel(x_ref, gain_ref, sh_ref, sc_ref, g_ref, wg_ref, wu_ref, wo_ref, o_ref, h_ref, acc_ref):
    j = pl.program_id(2)

    @pl.when(j == 0)
    def _():
        h_ref[...] = _norm_mod(x_ref[0], gain_ref[...], sh_ref[0], sc_ref[0]).astype(BF16)
        acc_ref[...] = jnp.zeros_like(acc_ref)

    h = h_ref[...]
    gate = jnp.dot(h, wg_ref[...], preferred_element_type=F32)
    up = jnp.dot(h, wu_ref[...], preferred_element_type=F32)
    act = gate * (1.0 / (1.0 + jnp.exp(-gate))) * up
    acc_ref[...] += jnp.dot(act.astype(BF16), wo_ref[...], preferred_element_type=F32)

    @pl.when(j == pl.num_programs(2) - 1)
    def _():
        o_ref[0] = x_ref[0] + g_ref[0] * acc_ref[...]


def ffn(x, gain, shift, scale, gate, w_in, w_out):
    b, s, d = x.shape
    f = w_out.shape[0]
    tm = _pick_tile(s, 512)
    th = 1408 if f % 1408 == 0 else _pick_tile(f, 256)
    nj = f // th
    vec = pl.BlockSpec((1, 1, d), lambda bi, i, j: (bi, 0, 0))
    return pl.pallas_call(
        _ffn_kernel,
        grid=(b, s // tm, nj),
        in_specs=[
            pl.BlockSpec((1, tm, d), lambda bi, i, j: (bi, i, 0)),
            pl.BlockSpec((1, d), lambda bi, i, j: (0, 0)),
            vec, vec, vec,
            pl.BlockSpec((d, th), lambda bi, i, j: (0, j)),
            pl.BlockSpec((d, th), lambda bi, i, j: (0, j + nj)),
            pl.BlockSpec((th, d), lambda bi, i, j: (j, 0)),
        ],
        out_specs=pl.BlockSpec((1, tm, d), lambda bi, i, j: (bi, i, 0)),
        out_shape=jax.ShapeDtypeStruct((b, s, d), F32),
        scratch_shapes=[pltpu.VMEM((tm, d), BF16), pltpu.VMEM((tm, d), F32)],
        compiler_params=_cparams("parallel", "parallel", "arbitrary"),
        name="ffn",
    )(x, gain, shift, scale, gate, w_in, w_in, w_out)


def _rms_kernel(x_ref, gain_ref, o_ref):
    x = x_ref[0]
    ms = jnp.mean(x * x, axis=-1, keepdims=True)
    o_ref[0] = x * lax.rsqrt(ms + RMS_EPS) * gain_ref[...]


def final_norm(x, gain):
    b, s, d = x.shape
    tm = _pick_tile(s, 1024)
    return pl.pallas_call(
        _rms_kernel,
        grid=(b, s // tm),
        in_specs=[pl.BlockSpec((1, tm, d), lambda bi, i: (bi, i, 0)),
                  pl.BlockSpec((1, d), lambda bi, i: (0, 0))],
        out_specs=pl.BlockSpec((1, tm, d), lambda bi, i: (bi, i, 0)),
        out_shape=jax.ShapeDtypeStruct((b, s, d), F32),
        compiler_params=_cparams("parallel", "parallel"),
        name="final_norm",
    )(x, gain)


def _flash_kernel(q_ref, k_ref, vt_ref, kx_ref, vxt_ref, o_ref, m_ref, acc_ref):
    kv = pl.program_id(3)
    g, tq, hd = q_ref.shape[1:]

    @pl.when(kv == 0)
    def _():
        m_ref[...] = jnp.full_like(m_ref, -jnp.inf)
        acc_ref[...] = jnp.zeros_like(acc_ref)

    q = q_ref[0].reshape(g * tq, hd)

    def scores(k):
        return _nt_dot(k, q)

    def accumulate(s, vt):
        m_prev = m_ref[...]
        m_new = jnp.maximum(m_prev, jnp.max(s, axis=0, keepdims=True))
        p = jnp.exp2(s - m_new).astype(BF16)
        acc_ref[...] = jnp.exp2(m_prev - m_new) * acc_ref[...] + jnp.dot(vt, p, preferred_element_type=F32)
        m_ref[...] = m_new

    tk = k_ref.shape[2]
    sub = min(tk, FLASH_SUB_KEYS)
    nsub = tk // sub
    s_next = scores(k_ref[0, 0, 0:sub, :])
    for c in range(nsub):
        s_cur = s_next
        if c + 1 < nsub:
            s_next = scores(k_ref[0, 0, (c + 1) * sub:(c + 2) * sub, :])
        accumulate(s_cur, vt_ref[0, 0, :, c * sub:(c + 1) * sub])

    @pl.when(kv == pl.num_programs(3) - 1)
    def _():
        accumulate(scores(kx_ref[0, 0]), vxt_ref[0, 0])
        acc = acc_ref[...]
        ot = (acc[:hd] / acc[hd:hd + 1]).astype(BF16)
        tb = min(tq, 256)
        eye = (lax.broadcasted_iota(jnp.int32, (tb, tb), 0)
               == lax.broadcasted_iota(jnp.int32, (tb, tb), 1)).astype(BF16)
        for h in range(g):
            for j in range(tq // tb):
                lo = h * tq + j * tb
                o_ref[0, h, j * tb:(j + 1) * tb, :] = _nt_dot(eye, ot[:, lo:lo + tb]).astype(o_ref.dtype)


def global_attention(q, k, vt, kx, vxt):
    b, h, s, hd = q.shape
    hkv = k.shape[1]
    n_ctx = kx.shape[2]
    vr = vt.shape[2]
    g = h // hkv
    tq = _pick_tile(s, 1024)
    tk = _pick_tile(s, 2048)
    q_spec = pl.BlockSpec((1, g, tq, hd), lambda bi, gi, i, j: (bi, gi, i, 0))
    return pl.pallas_call(
        _flash_kernel,
        grid=(b, hkv, s // tq, s // tk),
        in_specs=[
            q_spec,
            pl.BlockSpec((1, 1, tk, hd), lambda bi, gi, i, j: (bi, gi, j, 0)),
            pl.BlockSpec((1, 1, vr, tk), lambda bi, gi, i, j: (bi, gi, 0, j)),
            pl.BlockSpec((1, 1, n_ctx, hd), lambda bi, gi, i, j: (bi, gi, 0, 0)),
            pl.BlockSpec((1, 1, vr, n_ctx), lambda bi, gi, i, j: (bi, gi, 0, 0)),
        ],
        out_specs=q_spec,
        out_shape=jax.ShapeDtypeStruct((b, h, s, hd), BF16),
        scratch_shapes=[pltpu.VMEM((1, g * tq), F32), pltpu.VMEM((vr, g * tq), F32)],
        compiler_params=_cparams("parallel", "parallel", "parallel", "arbitrary"),
        name="global_attention",
    )(q, k, vt, kx, vxt)


def _window_kernel(sink_ref, q_ref, kp_ref, kc_ref, kn_ref, vp_ref, vc_ref, vn_ref, kx_ref, vx_ref, o_ref,
                   *, seq):
    gi = pl.program_id(1)
    i = pl.program_id(2)
    g, tq, hd = q_ref.shape[1:]
    w = kp_ref.shape[2]
    q = q_ref[0].reshape(g * tq, hd)
    k = jnp.concatenate([kp_ref[0, 0], kc_ref[0, 0], kn_ref[0, 0]], axis=0)
    v = jnp.concatenate([vp_ref[0, 0], vc_ref[0, 0], vn_ref[0, 0]], axis=0)
    span = tq + 2 * w
    s_loc = _nt_dot(q, k)
    row = lax.broadcasted_iota(jnp.int32, (g * tq, span), 0)
    col = lax.broadcasted_iota(jnp.int32, (g * tq, span), 1)
    qpos = i * tq + row % tq
    kpos = i * tq - w + col
    valid = (jnp.abs(kpos - qpos) <= w) & (kpos >= 0) & (kpos < seq)
    s_loc = jnp.where(valid, s_loc, NEG_INF)
    s_ctx = _nt_dot(q, kx_ref[0, 0])
    hrow = lax.broadcasted_iota(jnp.int32, (g * tq, 1), 0) // tq
    sink = jnp.zeros((g * tq, 1), F32)
    for hh in range(g):
        sink = jnp.where(hrow == hh, sink_ref[gi * g + hh], sink)
    m = jnp.maximum(jnp.maximum(jnp.max(s_loc, axis=-1, keepdims=True),
                                jnp.max(s_ctx, axis=-1, keepdims=True)), sink)
    p_loc = jnp.exp(s_loc - m)
    p_ctx = jnp.exp(s_ctx - m)
    l = jnp.sum(p_loc, axis=-1, keepdims=True) + jnp.sum(p_ctx, axis=-1, keepdims=True) + jnp.exp(sink - m)
    o = jnp.dot(p_loc.astype(BF16), v, preferred_element_type=F32)
    o = o + jnp.dot(p_ctx.astype(BF16), vx_ref[0, 0], preferred_element_type=F32)
    o_ref[0] = (o / l).reshape(g, tq, hd).astype(o_ref.dtype)


def window_attention(q, k, v, kx, vx, sink):
    b, h, s, hd = q.shape
    hkv = k.shape[1]
    g = h // hkv
    n_ctx = kx.shape[2]
    tq = _pick_tile(s, 256)
    r = tq // WINDOW
    nwb = s // WINDOW
    prev = pl.BlockSpec((1, 1, WINDOW, hd), lambda bi, gi, i: (bi, gi, jnp.maximum(i * r - 1, 0), 0))
    cur = pl.BlockSpec((1, 1, tq, hd), lambda bi, gi, i: (bi, gi, i, 0))
    nxt = pl.BlockSpec((1, 1, WINDOW, hd), lambda bi, gi, i: (bi, gi, jnp.minimum((i + 1) * r, nwb - 1), 0))
    cx = pl.BlockSpec((1, 1, n_ctx, hd), lambda bi, gi, i: (bi, gi, 0, 0))
    return pl.pallas_call(
        functools.partial(_window_kernel, seq=s),
        grid=(b, hkv, s // tq),
        in_specs=[
            pl.BlockSpec(memory_space=pltpu.SMEM),
            pl.BlockSpec((1, g, tq, hd), lambda bi, gi, i: (bi, gi, i, 0)),
            prev, cur, nxt, prev, cur, nxt, cx, cx,
        ],
        out_specs=pl.BlockSpec((1, g, tq, hd), lambda bi, gi, i: (bi, gi, i, 0)),
        out_shape=jax.ShapeDtypeStruct((b, h, s, hd), BF16),
        compiler_params=_cparams("parallel", "parallel", "parallel"),
        name="window_attention",
    )(sink, q, k, k, k, v, v, v, kx, vx)


def _ctx_attn_kernel(sink_ref, q_ref, k_ref, v_ref, o_ref, *, mixer_a):
    hi = pl.program_id(1)
    s = _nt_dot(q_ref[0, 0], k_ref[0, 0])
    sink = sink_ref[hi]
    m = jnp.maximum(jnp.max(s, axis=-1, keepdims=True), sink)
    if mixer_a:
        p = jnp.exp2(s - m)
        l = jnp.sum(p, axis=-1, keepdims=True)
        o = _nt_dot(p.astype(BF16), v_ref[0, 0, :HEAD_DIM, :])
    else:
        p = jnp.exp(s - m)
        l = jnp.sum(p, axis=-1, keepdims=True) + jnp.exp(sink - m)
        o = jnp.dot(p.astype(BF16), v_ref[0, 0], preferred_element_type=F32)
    o_ref[0, 0] = (o / l).astype(o_ref.dtype)


def ctx_attention(q, k, v, sink, mixer_a=False):
    b, h, n, hd = q.shape
    g = h // k.shape[1]
    kv = pl.BlockSpec((1, 1, n, hd), lambda bi, hi: (bi, hi // g, 0, 0))
    vs = pl.BlockSpec((1, 1, v.shape[2], n), lambda bi, hi: (bi, hi // g, 0, 0)) if mixer_a else kv
    qs = pl.BlockSpec((1, 1, n, hd), lambda bi, hi: (bi, hi, 0, 0))
    return pl.pallas_call(
        functools.partial(_ctx_attn_kernel, mixer_a=mixer_a),
        grid=(b, h),
        in_specs=[pl.BlockSpec(memory_space=pltpu.SMEM), qs, kv, vs],
        out_specs=qs,
        out_shape=jax.ShapeDtypeStruct((b, h, n, hd), BF16),
        compiler_params=_cparams("parallel", "parallel"),
        name="ctx_attention",
    )(sink, q, k, v)


def _na_kernel(q_ref, k_ref, v_ref, kx_ref, vx_ref, bias_ref, o_ref, *, rows):
    t = pl.program_id(2)
    nkeys = NA_KEY_ROWS * GRID_W
    kstart = jnp.clip(t * NA_TILE_ROWS - NA_ROWS // 2, 0, rows - NA_KEY_ROWS)
    off = pl.multiple_of(kstart * GRID_W, GRID_W)
    q = q_ref[0, 0]
    kw = k_ref[0, 0, pl.ds(off, nkeys), :]
    vw = v_ref[0, 0, pl.ds(off, nkeys), :]
    s_loc = _nt_dot(q, kw) + bias_ref[0, 0]
    s_ctx = _nt_dot(q, kx_ref[0, 0])
    m = jnp.maximum(jnp.max(s_loc, axis=-1, keepdims=True), jnp.max(s_ctx, axis=-1, keepdims=True))
    p_loc = jnp.exp(s_loc - m)
    p_ctx = jnp.exp(s_ctx - m)
    l = jnp.sum(p_loc, axis=-1, keepdims=True) + jnp.sum(p_ctx, axis=-1, keepdims=True)
    o = jnp.dot(p_loc.astype(BF16), vw, preferred_element_type=F32)
    o = o + jnp.dot(p_ctx.astype(BF16), vx_ref[0, 0], preferred_element_type=F32)
    o_ref[0, 0] = (o / l).astype(o_ref.dtype)


def na_bias_tables(rpb, rows):
    nh = rpb.shape[0]
    i = np.arange(NA_TILE_ROWS)
    j = np.arange(NA_KEY_ROWS)
    qc = np.arange(GRID_W)
    kc = np.arange(GRID_W)
    cs = np.clip(qc - NA_COLS // 2, 0, GRID_W - NA_COLS)
    col_valid = (kc[None, :] >= cs[:, None]) & (kc[None, :] < cs[:, None] + NA_COLS)
    dcol = np.clip(kc[None, :] - qc[:, None] + NA_COLS - 1, 0, 2 * NA_COLS - 2)
    per_drow = jnp.where(col_valid[None, None], rpb.astype(F32)[:, :, dcol], NEG_INF)
    masked = jnp.full((nh, 1, GRID_W, GRID_W), NEG_INF, F32)
    per_drow = jnp.concatenate([per_drow, masked], axis=1)
    out = []
    for r0 in (0, NA_TILE_ROWS, rows - NA_TILE_ROWS):
        r = r0 + i
        rs = np.clip(r - NA_ROWS // 2, 0, rows - NA_ROWS)
        kstart = int(np.clip(r0 - NA_ROWS // 2, 0, rows - NA_KEY_ROWS))
        krow = kstart + j
        row_valid = (krow[None, :] >= rs[:, None]) & (krow[None, :] < rs[:, None] + NA_ROWS)
        drow = np.where(row_valid, krow[None, :] - r[:, None] + NA_ROWS - 1, 2 * NA_ROWS - 1)
        cols = [per_drow[:, drow[:, jj]] for jj in range(NA_KEY_ROWS)]
        dense = jnp.stack(cols, axis=3)
        out.append(dense.reshape(nh, NA_TILE_ROWS * GRID_W, NA_KEY_ROWS * GRID_W))
    return jnp.stack(out)


def neighborhood_attention(q, k, v, kx, vx, bias):
    b, h, s, hd = q.shape
    rows = s // GRID_W
    nt = rows // NA_TILE_ROWS
    tq = NA_TILE_ROWS * GRID_W
    nkeys = NA_KEY_ROWS * GRID_W
    n_ctx = kx.shape[2]
    full = pl.BlockSpec((1, 1, s, hd), lambda bi, hi, t: (bi, hi, 0, 0))
    cx = pl.BlockSpec((1, 1, n_ctx, hd), lambda bi, hi, t: (bi, hi, 0, 0))

    def case(t):
        return jnp.where(t == 0, 0, jnp.where(t == nt - 1, 2, 1))

    return pl.pallas_call(
        functools.partial(_na_kernel, rows=rows),
        grid=(b, h, nt),
        in_specs=[
            pl.BlockSpec((1, 1, tq, hd), lambda bi, hi, t: (bi, hi, t, 0)),
            full, full, cx, cx,
            pl.BlockSpec((1, 1, tq, nkeys), lambda bi, hi, t: (case(t), hi, 0, 0)),
        ],
        out_specs=pl.BlockSpec((1, 1, tq, hd), lambda bi, hi, t: (bi, hi, t, 0)),
        out_shape=jax.ShapeDtypeStruct((b, h, s, hd), BF16),
        compiler_params=_cparams("parallel", "parallel", "arbitrary"),
        name="neighborhood_attention",
    )(q, k, v, kx, vx, bias)


def _mm(a, b, exact=False):
    if exact:
        return jnp.dot(a, b, preferred_element_type=F32, precision=lax.Precision.HIGHEST)
    return jnp.dot(a.astype(BF16), b.astype(BF16), preferred_element_type=F32)


def _mm_nt(a, b):
    return lax.dot_general(a.astype(BF16), b.astype(BF16), (((1,), (1,)), ((), ())),
                           preferred_element_type=F32)


def _mm_tn(a, b):
    return lax.dot_general(a.astype(BF16), b.astype(BF16), (((0,), (0,)), ((), ())),
                           preferred_element_type=F32)


def _rwkv_chains(chains):
    t, w = chains[0][0].shape
    nh = w // HEAD_DIM
    n = range(len(chains))
    lw, kd, a, r, v, kap, st, rev = (list(col) for col in zip(*chains))
    row = lax.broadcasted_iota(jnp.int32, (t, w), 0)
    colj = lax.broadcasted_iota(jnp.int32, (t, w), 1) % t
    lane_head = lax.broadcasted_iota(jnp.int32, (1, w), 1) // HEAD_DIM
    strict_d = (colj < row, colj > row)
    incl_d = (colj <= row, colj >= row)
    tri_d = tuple(m[:, :t].astype(F32) for m in incl_d)
    strict = [strict_d[int(rev[i])] for i in n]
    incl = [incl_d[int(rev[i])] for i in n]

    def bd(y):
        return jnp.concatenate([jnp.where(lane_head == h, y, 0.0) for h in range(nh)], axis=0)

    def bdmm(xs, ys):
        return [_mm(xs[i], bd(ys[i])) for i in n]

    c_incl = [_mm(tri_d[int(rev[i])], lw[i], exact=True) for i in n]
    c_excl = [c_incl[i] - lw[i] for i in n]
    c_mid = [c_incl[i][t // 2:t // 2 + 1, :] for i in n]
    c_end = [c_incl[i][0:1, :] if rev[i] else c_incl[i][t - 1:t, :] for i in n]
    bvec = [kap[i] * a[i] for i in n]
    e_neg = [jnp.exp(c_mid[i] - c_incl[i]) for i in n]
    left = [jnp.concatenate([kap[i] * jnp.exp(c_excl[i] - c_mid[i]), r[i] * jnp.exp(c_incl[i] - c_mid[i])],
                            axis=0) for i in n]
    wt = [jnp.concatenate([bd(bvec[i] * e_neg[i]), bd(kd[i] * e_neg[i])], axis=0) for i in n]
    m = [_mm_nt(left[i], wt[i]) for i in n]
    l_b = [jnp.where(strict[i], m[i][:t, :w], 0.0) for i in n]
    l_k = [jnp.where(strict[i], m[i][:t, w:], 0.0) for i in n]
    m_br = [jnp.where(incl[i], m[i][t:, :w], 0.0) for i in n]
    m_kr = [jnp.where(incl[i], m[i][t:, w:], 0.0) for i in n]

    blk = RWKV_INV_BLOCK
    same = row // blk == colj // blk
    eye = jnp.where(row == colj, 1.0, 0.0)
    pw = [-jnp.where(same, l_b[i], 0.0) for i in n]
    inv = [eye + pw[i] for i in n]
    span = 2
    while span < blk:
        pw = bdmm(pw, pw)
        step = bdmm(inv, pw)
        inv = [inv[i] + step[i] for i in n]
        span *= 2
    while blk < t:
        pair = (row // (2 * blk) == colj // (2 * blk)) & (row // blk != colj // blk)
        off = [jnp.where(pair, l_b[i], 0.0) for i in n]
        step = bdmm(bdmm(inv, off), inv)
        inv = [inv[i] - step[i] for i in n]
        blk *= 2

    kt = [kap[i] * jnp.exp(c_excl[i]) for i in n]
    rt = [r[i] * jnp.exp(c_incl[i]) for i in n]
    lkv = bdmm(l_k, v)
    mkv = bdmm(m_kr, v)
    wk = bdmm(inv, kt)
    u0 = bdmm(inv, lkv)
    mbw = bdmm(m_br, wk)
    mbu = bdmm(m_br, u0)
    rq = [rt[i] - mbw[i] for i in n]
    y0 = [mkv[i] - mbu[i] for i in n]
    e_end = [jnp.exp(c_end[i] - c_incl[i]) for i in n]
    b_hat = [bvec[i] * e_end[i] for i in n]
    k_hat = [kd[i] * e_end[i] for i in n]
    ri = lax.broadcasted_iota(jnp.int32, (w, w), 0)
    ci = lax.broadcasted_iota(jnp.int32, (w, w), 1)
    same_head = ri // HEAD_DIM == ci // HEAD_DIM
    diag = ri == ci
    bw = [_mm_tn(b_hat[i], wk[i]) for i in n]
    kv = [_mm_tn(k_hat[i], v[i]) for i in n]
    bu = [_mm_tn(b_hat[i], u0[i]) for i in n]
    gt = [jnp.where(diag, jnp.exp(c_end[i]), 0.0) - jnp.where(same_head, bw[i], 0.0) for i in n]
    c0t = [jnp.where(same_head, kv[i] - bu[i], 0.0) for i in n]
    ys = [_mm(rq[i], st[i]) + y0[i] for i in n]
    sts = [_mm(gt[i], st[i]) + c0t[i] for i in n]
    return ys, sts


def _rwkv_kernel(lwf_ref, kdf_ref, af_ref, lwb_ref, kdb_ref, ab_ref, rf_ref, vf_ref, kkf_ref,
                 rb_ref, vb_ref, kkb_ref, sin_ref, yf_ref, yb_ref, sout_ref, st_ref):
    c = pl.program_id(0)

    @pl.when(c == 0)
    def _():
        st_ref[...] = sin_ref[...]

    bsz = lwf_ref.shape[0]
    groups = lwf_ref.shape[2] // RWKV_LANES
    chains, slots = [], []
    for bi in range(bsz):
        for d, (lw_ref, kd_ref, a_ref, r_ref, v_ref, kk_ref, y_ref) in enumerate((
                (lwf_ref, kdf_ref, af_ref, rf_ref, vf_ref, kkf_ref, yf_ref),
                (lwb_ref, kdb_ref, ab_ref, rb_ref, vb_ref, kkb_ref, yb_ref))):
            for hg in range(groups):
                lanes = slice(hg * RWKV_LANES, (hg + 1) * RWKV_LANES)
                chain = d * groups + hg
                chains.append((lw_ref[bi, :, lanes], kd_ref[bi, :, lanes], a_ref[bi, :, lanes],
                               r_ref[bi, :, lanes], v_ref[bi, :, lanes], kk_ref[bi, :, lanes],
                               st_ref[bi, chain], bool(d)))
                slots.append((y_ref, bi, lanes, chain))
    ys, sts = _rwkv_chains(chains)
    for (y_ref, bi, lanes, chain), y, st in zip(slots, ys, sts):
        y_ref[bi, :, lanes] = y
        st_ref[bi, chain] = st

    @pl.when(c == pl.num_programs(0) - 1)
    def _():
        sout_ref[...] = st_ref[...]


def rwkv_scan(lw, kd, a, r, v, kk, state):
    b, length, w = r.shape
    t = RWKV_CHUNK
    nc = length // t
    fwd = pl.BlockSpec((b, t, w), lambda c: (0, c, 0))
    bwd = pl.BlockSpec((b, t, w), lambda c: (0, nc - 1 - c, 0))
    st_spec = pl.BlockSpec(state.shape, lambda c: (0, 0, 0, 0))
    y_shape = jax.ShapeDtypeStruct((b, length, w), F32)
    return pl.pallas_call(
        _rwkv_kernel,
        grid=(nc,),
        in_specs=[fwd, fwd, fwd, bwd, bwd, bwd, fwd, fwd, fwd, bwd, bwd, bwd, st_spec],
        out_specs=[fwd, bwd, st_spec],
        out_shape=[y_shape, y_shape, jax.ShapeDtypeStruct(state.shape, F32)],
        scratch_shapes=[pltpu.VMEM(state.shape, F32)],
        compiler_params=_cparams("arbitrary"),
        name="rwkv_scan",
    )(lw[0], kd[0], a[0], lw[1], kd[1], a[1], r, v, kk, r, v, kk, state)


def _exact_dot(a, b):
    return jnp.dot(a, b, preferred_element_type=F32, precision=lax.Precision.HIGHEST)


def _head_sum_matrix(width, value):
    r = lax.broadcasted_iota(jnp.int32, (width, width), 0) // HEAD_DIM
    c = lax.broadcasted_iota(jnp.int32, (width, width), 1) // HEAD_DIM
    return jnp.where(r == c, value, 0.0).astype(BF16)


def _head_sum(x, mat):
    hi = x.astype(BF16)
    lo = (x - hi.astype(F32)).astype(BF16)
    return jnp.dot(hi, mat, preferred_element_type=F32) + jnp.dot(lo, mat, preferred_element_type=F32)


def _sigmoid(x):
    return 1.0 / (1.0 + jnp.exp(-x))


def _rwkv_prep_kernel(z_ref, zp_ref, zn_ref, mu_ref, kk_ref, ka_ref, w0_ref, w2_ref, a0_ref, a2_ref, g2_ref,
                      r_ref, v_ref, kn_ref, g_ref, lw0_ref, kd0_ref, av0_ref, lw1_ref, kd1_ref, av1_ref):
    i = pl.program_id(1)
    z = z_ref[0]
    tm = z.shape[0]
    halo = zp_ref.shape[1]
    row = lax.broadcasted_iota(jnp.int32, (tm, 1), 0)
    before = jnp.where(i > 0, zp_ref[0, halo - 1:halo, :], 0.0)
    after = jnp.where(i < pl.num_programs(1) - 1, zn_ref[0, 0:1, :], 0.0)
    z_prev = jnp.where(row == 0, before, pltpu.roll(z, 1, 0))
    z_next = jnp.where(row == tm - 1, after, pltpu.roll(z, tm - 1, 0))
    zs = z + (0.5 * (z_prev + z_next) - z) * mu_ref[...]

    r = zs[:, :D_W]
    k = zs[:, D_W:2 * D_W]
    v = zs[:, 2 * D_W:3 * D_W]
    o = 3 * D_W
    wd = zs[:, o:o + DECAY_LORA]
    ad = zs[:, o + DECAY_LORA:o + DECAY_LORA + ICLR_LORA]
    gd = zs[:, o + DECAY_LORA + ICLR_LORA:]
    r_ref[0] = r
    v_ref[0] = v
    g_ref[0] = _exact_dot(_sigmoid(gd), g2_ref[...])
    kraw = k * kk_ref[...]
    ss = _head_sum(kraw * kraw, _head_sum_matrix(D_W, 1.0))
    kn_ref[0] = kraw / jnp.maximum(jnp.sqrt(ss), 1e-12)
    twd = jnp.tanh(wd)
    for d, (lw_ref, kd_ref, av_ref) in enumerate(((lw0_ref, kd0_ref, av0_ref), (lw1_ref, kd1_ref, av1_ref))):
        x = -(w0_ref[d:d + 1, :] + _exact_dot(twd, w2_ref[d]))
        softplus = jnp.maximum(x, 0.0) + jnp.log(1.0 + jnp.exp(-jnp.abs(x)))
        lw_ref[0] = -jnp.exp(-softplus - 0.5)
        a = _sigmoid(a0_ref[d:d + 1, :] + _exact_dot(ad, a2_ref[d]))
        av_ref[0] = a
        kd_ref[0] = k * (1.0 + (a - 1.0) * ka_ref[...])


def rwkv_prep(zd, mu, k_k, k_a, w0, w2, a0, a2, g2):
    b, length, width = zd.shape
    tm = _pick_tile(length, 512)
    halo = 8
    nb = length // halo
    per = tm // halo
    row = pl.BlockSpec((1, tm, D_W), lambda bi, i: (bi, i, 0))
    shape = jax.ShapeDtypeStruct((b, length, D_W), F32)

    def full(a):
        return pl.BlockSpec(a.shape, lambda bi, i: (0,) * a.ndim)

    params = (mu, k_k, k_a, w0, w2, a0, a2, g2)
    return pl.pallas_call(
        _rwkv_prep_kernel,
        grid=(b, length // tm),
        in_specs=[
            pl.BlockSpec((1, tm, width), lambda bi, i: (bi, i, 0)),
            pl.BlockSpec((1, halo, width), lambda bi, i: (bi, jnp.maximum(i * per - 1, 0), 0)),
            pl.BlockSpec((1, halo, width), lambda bi, i: (bi, jnp.minimum((i + 1) * per, nb - 1), 0)),
        ] + [full(p) for p in params],
        out_specs=[row] * 10,
        out_shape=[shape] * 10,
        compiler_params=_cparams("parallel", "parallel"),
        name="rwkv_prep",
    )(zd, zd, zd, *params)


def _rwkv_readout_kernel(yf_ref, yb_ref, r_ref, v_ref, kd0_ref, kd1_ref, g_ref, rk_ref, lnw_ref, lnb_ref, o_ref):
    y = yf_ref[0] + yb_ref[0]
    avg = _head_sum_matrix(D_W, 1.0 / HEAD_DIM)
    mean = _head_sum(y, avg)
    yc = y - mean
    var = _head_sum(yc * yc, avg)
    yn = yc * lax.rsqrt(var + GN_EPS) * lnw_ref[...] + lnb_ref[...]
    bonus = _head_sum(r_ref[0] * (kd0_ref[0] + kd1_ref[0]) * rk_ref[...], _head_sum_matrix(D_W, 1.0))
    o_ref[0] = ((yn + bonus * v_ref[0]) * g_ref[0]).astype(o_ref.dtype)


def rwkv_readout(yf, yb, r, v, kd0, kd1, g, r_k, ln_w, ln_b):
    b, length, w = yf.shape
    tm = _pick_tile(length, 512)
    row = pl.BlockSpec((1, tm, w), lambda bi, i: (bi, i, 0))
    vec = pl.BlockSpec((1, w), lambda bi, i: (0, 0))
    return pl.pallas_call(
        _rwkv_readout_kernel,
        grid=(b, length // tm),
        in_specs=[row] * 7 + [vec] * 3,
        out_specs=row,
        out_shape=jax.ShapeDtypeStruct((b, length, w), BF16),
        compiler_params=_cparams("parallel", "parallel"),
        name="rwkv_readout",
    )(yf, yb, r, v, kd0, kd1, g, r_k, ln_w, ln_b)


def _rope_tables(n_tokens):
    t = jnp.arange(n_tokens, dtype=jnp.int32)
    row = (t // GRID_W).astype(F32)
    col = (t % GRID_W).astype(F32)
    inv = ROPE_THETA ** (-jnp.arange(ROPE_FREQS, dtype=F32) / ROPE_FREQS)
    ang = jnp.concatenate([row[:, None] * inv, col[:, None] * inv], axis=-1)
    cos, sin = jnp.cos(ang), jnp.sin(ang)
    return jnp.concatenate([cos, cos], axis=-1), jnp.concatenate([sin, sin], axis=-1)


def _rwkv_mixer(zd, zdc, params, r_k, ln_w, ln_b, need_ctx):
    lat = rwkv_prep(zd, *params)
    cx = rwkv_prep(zdc, *params)
    bsz = zd.shape[0]
    chains = 2 * (D_W // RWKV_LANES)
    zero = jnp.zeros((bsz, chains, RWKV_LANES, RWKV_LANES), F32)

    def scan(p, state):
        r, v, kk, _, lw0, kd0, a0, lw1, kd1, a1 = p
        return rwkv_scan((lw0, lw1), (kd0, kd1), (a0, a1), r, v, kk, state)

    def readout(p, yf, yb):
        r, v, _, g, _, kd0, _, _, kd1, _ = p
        return rwkv_readout(yf, yb, r, v, kd0, kd1, g, r_k, ln_w, ln_b)

    ycf, ycb, state = scan(cx, zero)
    yf, yb, _ = scan(lat, state)
    y_d = readout(lat, yf, yb)
    yc_d = readout(cx, ycf, ycb) if need_ctx else None
    return y_d, yc_d


def kernel(x, c, ctx, c_ctx, w_mod, b_mod, norm_mix, norm_ffn, w_in_even, w_out_even, q_norm_a, k_norm_a,
           sink_b, w_in_odd, w_out_odd, rpb_c, shift_mu, decay_w0, decay_w2, iclr_a0, iclr_a2, gate_g2,
           k_k, k_a, r_k, ln_x_w, ln_x_b, w_ffn_in, w_ffn_out, norm_out):
    bsz, s, d = x.shape
    n_ctx = ctx.shape[1]
    cos2, sin2 = _rope_tables(s)
    cos_ctx = jnp.ones((n_ctx, HEAD_DIM), F32)
    sin_ctx = jnp.zeros((n_ctx, HEAD_DIM), F32)
    act = jnp.zeros((8, d), F32).at[:bsz].set(jax.nn.silu(c)).at[bsz].set(jax.nn.silu(c_ctx))
    mods = modulation(act.astype(BF16), w_mod, b_mod)
    w_out_even = w_out_even.astype(BF16)
    w_in_odd = w_in_odd.astype(BF16)
    w_out_odd = w_out_odd.astype(BF16)
    w_ffn_in = w_ffn_in.astype(BF16)
    w_ffn_out = w_ffn_out.astype(BF16)
    no_sink = jnp.full((A_HEADS,), NEG_INF, F32)
    for layer in range(DEPTH):
        need_ctx = layer < DEPTH - 1
        mod = mods[layer, :bsz][:, None, :]
        modc = jnp.broadcast_to(mods[layer, bsz][None, None, :], (bsz, 1, 6 * d))
        sh1, sc1, g1, sh2, sc2, g2 = jnp.split(mod, 6, axis=-1)
        csh1, csc1, cg1, csh2, csc2, cg2 = jnp.split(modc, 6, axis=-1)
        gain_mix = norm_mix[layer][None, :]
        gain_ffn = norm_ffn[layer][None, :]
        i = layer // 2
        if layer % 2 == 0:
            w_out = w_out_even[i]
            qg, kg = q_norm_a[i][None, :], k_norm_a[i][None, :]
            w_ext = even_weight(w_in_even[i], q_norm_a[i], k_norm_a[i])
            w_vat = w_in_even[i][:, _EV_VA:_EV_VA + A_KV].T.astype(BF16)
            qa, ka, vat, qb, kb, vb = even_proj(x, gain_mix, sh1, sc1, w_ext, w_vat, cos2, sin2, qg, kg)
            qca, kca, vcat, qcb, kcb, vcb = even_proj(ctx, gain_mix, csh1, csc1, w_ext, w_vat,
                                                       cos_ctx, sin_ctx, qg, kg)
            y1 = global_attention(qa, ka, vat, kca, vcat)
            y2 = window_attention(qb, kb, vb, kcb, vcb, sink_b[i])
            if need_ctx:
                yc1 = ctx_attention(qca, kca, vcat, no_sink, mixer_a=True)
                yc2 = ctx_attention(qcb, kcb, vcb, sink_b[i])
        else:
            w_out = w_out_odd[i]
            q, k, v, zd = odd_proj(x, gain_mix, sh1, sc1, w_in_odd[i])
            qc, kc, vc, zdc = odd_proj(ctx, gain_mix, csh1, csc1, w_in_odd[i])
            bias = na_bias_tables(rpb_c[i], s // GRID_W)
            y1 = neighborhood_attention(q, k, v, kc, vc, bias)
            params = (shift_mu[i][None, :], k_k[i][None, :], k_a[i][None, :], decay_w0[i], decay_w2[i],
                      iclr_a0[i], iclr_a2[i], gate_g2[i])
            y2, yc2 = _rwkv_mixer(zd, zdc, params, r_k[i].reshape(1, D_W), ln_x_w[i][None, :],
                                  ln_x_b[i][None, :], need_ctx)
            if need_ctx:
                yc1 = ctx_attention(qc, kc, vc, no_sink)
        x = out_proj(x, g1, y1, y2, w_out)
        x = ffn(x, gain_ffn, sh2, sc2, g2, w_ffn_in[layer], w_ffn_out[layer])
        if need_ctx:
            ctx = out_proj(ctx, cg1, yc1, yc2, w_out)
            ctx = ffn(ctx, gain_ffn, csh2, csc2, cg2, w_ffn_in[layer], w_ffn_out[layer])
    return final_norm(x, norm_out[None, :])
```

```python
import functools

import jax
import jax.numpy as jnp
import numpy as np
from jax import lax
from jax.experimental import pallas as pl
from jax.experimental.pallas import tpu as pltpu

F32 = jnp.float32
BF16 = jnp.bfloat16

D_MODEL = 1024
DEPTH = 4
GRID_W = 64
HEAD_DIM = 64
SCALE = HEAD_DIM ** -0.5
ROPE_HALF = HEAD_DIM // 2
ROPE_FREQS = HEAD_DIM // 4
ROPE_THETA = 10000.0
A_HEADS = 8
A_KV_HEADS = 2
B_HEADS = 8
B_KV_HEADS = 2
WINDOW = 128
C_HEADS = 8
NA_ROWS = 8
NA_COLS = 16
D_HEADS = 8
DECAY_LORA = 64
ICLR_LORA = 64
GATE_LORA = 128
GN_EPS = 64e-5
RMS_EPS = 1e-6
NEG_INF = -1e30
FFN_HIDDEN = -(-8 * D_MODEL // (3 * 256)) * 256

A_Q = A_HEADS * HEAD_DIM
A_KV = A_KV_HEADS * HEAD_DIM
B_Q = B_HEADS * HEAD_DIM
B_KV = B_KV_HEADS * HEAD_DIM
C_W = C_HEADS * HEAD_DIM
D_W = D_HEADS * HEAD_DIM
EVEN_WIDTHS = (A_Q, A_KV, A_KV, B_Q, B_KV, B_KV)
D_SHIFT_W = 3 * D_W + DECAY_LORA + ICLR_LORA + GATE_LORA

VMEM_LIMIT_BYTES = 56 * 1024 * 1024
RWKV_CHUNK = HEAD_DIM
RWKV_INV_BLOCK = 16
RWKV_LANES = 256
FLASH_SUB_KEYS = 512
LOG2E = 1.4426950408889634
VT_ROWS = HEAD_DIM + 16
NA_TILE_ROWS = 8
NA_KEY_ROWS = NA_TILE_ROWS + NA_ROWS


def _cparams(*sem):
    return pltpu.CompilerParams(dimension_semantics=sem, vmem_limit_bytes=VMEM_LIMIT_BYTES)


def _split(t, widths):
    return jnp.split(t, [int(o) for o in np.cumsum(widths)[:-1]], axis=-1)


def _pick_tile(n, target):
    t = min(n, target)
    while n % t:
        t //= 2
    return t


def _mod_kernel(a_ref, w_ref, b_ref, o_ref):
    w = w_ref[0].astype(BF16)
    o_ref[0] = jnp.dot(a_ref[...], w, preferred_element_type=F32) + b_ref[0]


def modulation(act, w_mod, b_mod):
    depth, d, n = w_mod.shape
    tn = 1536
    return pl.pallas_call(
        _mod_kernel,
        grid=(depth, n // tn),
        in_specs=[
            pl.BlockSpec((8, d), lambda l, j: (0, 0)),
            pl.BlockSpec((1, d, tn), lambda l, j: (l, 0, j)),
            pl.BlockSpec((1, 1, tn), lambda l, j: (l, 0, j)),
        ],
        out_specs=pl.BlockSpec((1, 8, tn), lambda l, j: (l, 0, j)),
        out_shape=jax.ShapeDtypeStruct((depth, 8, n), F32),
        compiler_params=_cparams("parallel", "parallel"),
        name="modulation",
    )(act, w_mod, b_mod.reshape(depth, 1, n))


def _norm_mod(x, gain, shift, scale):
    ms = jnp.mean(x * x, axis=-1, keepdims=True)
    h = x * lax.rsqrt(ms + RMS_EPS) * gain
    return h * (1.0 + scale) + shift


def _head_rms(z):
    return lax.rsqrt(jnp.mean(z * z, axis=-1, keepdims=True) + RMS_EPS)


def _nt_dot(a, b):
    return lax.dot_general(a, b, (((1,), (1,)), ((), ())), preferred_element_type=F32)


def _tn_dot(a, b):
    return lax.dot_general(a, b, (((0,), (0,)), ((), ())), preferred_element_type=F32)


def _transpose_heads(ot, o_ref, g, tq):
    tb = min(tq, 256)
    eye = (lax.broadcasted_iota(jnp.int32, (tb, tb), 0)
           == lax.broadcasted_iota(jnp.int32, (tb, tb), 1)).astype(BF16)
    for h in range(g):
        for j in range(tq // tb):
            lo = h * tq + j * tb
            o_ref[h, j * tb:(j + 1) * tb, :] = _nt_dot(eye, ot[:, lo:lo + tb]).astype(o_ref.dtype)


_EV_QA, _EV_KA, _EV_VA, _EV_QB, _EV_KB, _EV_VB = (int(o) for o in np.cumsum((0,) + EVEN_WIDTHS)[:-1])
_EV_QAR = sum(EVEN_WIDTHS)
_EV_KAR = _EV_QAR + A_Q
_EV_QBR = _EV_KAR + A_KV
_EV_KBR = _EV_QBR + B_Q
EVEN_EXT = _EV_KBR + B_KV


def _even_proj_kernel(x_ref, gain_ref, sh_ref, sc_ref, w_ref, wvt_ref, cos_ref, sin_ref, qg_ref, kg_ref,
                      qa_ref, ka_ref, vat_ref, qb_ref, kb_ref, vbt_ref):
    h = _norm_mod(x_ref[0], gain_ref[...], sh_ref[0], sc_ref[0]).astype(BF16)
    z = jnp.dot(h, w_ref[...], preferred_element_type=F32)
    vt = _nt_dot(wvt_ref[...], h)
    pad_rows = (lax.broadcasted_iota(jnp.int32, (VT_ROWS - HEAD_DIM, vt.shape[1]), 0) == 0).astype(BF16)
    cos = cos_ref[...]
    sin = sin_ref[...]
    qg = qg_ref[...]
    kg = kg_ref[...]

    def sl(off, i):
        return z[:, off + i * HEAD_DIM: off + (i + 1) * HEAD_DIM]

    for i in range(A_HEADS):
        zq = sl(_EV_QA, i)
        qa = _head_rms(zq) * (zq * qg * cos + sl(_EV_QAR, i) * sin)
        qa_ref[0, i] = (qa * (SCALE * LOG2E)).astype(BF16)
    for i in range(B_HEADS):
        qb_ref[0, i] = ((sl(_EV_QB, i) * cos + sl(_EV_QBR, i) * sin) * SCALE).astype(BF16)
    for i in range(A_KV_HEADS):
        zk = sl(_EV_KA, i)
        ka_ref[0, i] = (_head_rms(zk) * (zk * kg * cos + sl(_EV_KAR, i) * sin)).astype(BF16)
        vat_ref[0, i, :HEAD_DIM, :] = vt[i * HEAD_DIM:(i + 1) * HEAD_DIM].astype(BF16)
        vat_ref[0, i, HEAD_DIM:, :] = pad_rows
    for i in range(B_KV_HEADS):
        kb_ref[0, i] = (sl(_EV_KB, i) * cos + sl(_EV_KBR, i) * sin).astype(BF16)
        vbt_ref[0, i, :HEAD_DIM, :] = vt[A_KV + i * HEAD_DIM:A_KV + (i + 1) * HEAD_DIM].astype(BF16)
        vbt_ref[0, i, HEAD_DIM:, :] = pad_rows


def _rot_cols(w):
    d = w.shape[0]
    wh = w.reshape(d, -1, 2, ROPE_HALF)
    return jnp.stack([-wh[:, :, 1], wh[:, :, 0]], axis=2).reshape(w.shape)


def even_weight(w_in, q_gain, k_gain):
    qa, ka, va, qb, kb, vb = _split(w_in, EVEN_WIDTHS)
    qar = _rot_cols(qa * jnp.tile(q_gain, A_HEADS))
    kar = _rot_cols(ka * jnp.tile(k_gain, A_KV_HEADS))
    return jnp.concatenate([w_in, qar, kar, _rot_cols(qb), _rot_cols(kb)], axis=1).astype(BF16)


def even_proj(x, gain, shift, scale, w_ext, w_vt, cos2, sin2, q_gain, k_gain):
    b, s, d = x.shape
    tm = _pick_tile(s, 512)
    vec = pl.BlockSpec((1, 1, d), lambda bi, i: (bi, 0, 0))
    tab = pl.BlockSpec((tm, HEAD_DIM), lambda bi, i: (i, 0))
    hv = pl.BlockSpec((1, HEAD_DIM), lambda bi, i: (0, 0))

    def hm(n):
        return pl.BlockSpec((1, n, tm, HEAD_DIM), lambda bi, i: (bi, 0, i, 0))

    def shape(n):
        return jax.ShapeDtypeStruct((b, n, s, HEAD_DIM), BF16)

    def fm(n):
        return pl.BlockSpec((1, n, VT_ROWS, tm), lambda bi, i: (bi, 0, 0, i))

    def fshape(n):
        return jax.ShapeDtypeStruct((b, n, VT_ROWS, s), BF16)

    return pl.pallas_call(
        _even_proj_kernel,
        grid=(b, s // tm),
        in_specs=[
            pl.BlockSpec((1, tm, d), lambda bi, i: (bi, i, 0)),
            pl.BlockSpec((1, d), lambda bi, i: (0, 0)),
            vec, vec,
            pl.BlockSpec((d, EVEN_EXT), lambda bi, i: (0, 0)),
            pl.BlockSpec((A_KV + B_KV, d), lambda bi, i: (0, 0)),
            tab, tab, hv, hv,
        ],
        out_specs=[hm(A_HEADS), hm(A_KV_HEADS), fm(A_KV_HEADS), hm(B_HEADS), hm(B_KV_HEADS), fm(B_KV_HEADS)],
        out_shape=[shape(A_HEADS), shape(A_KV_HEADS), fshape(A_KV_HEADS),
                   shape(B_HEADS), shape(B_KV_HEADS), fshape(B_KV_HEADS)],
        compiler_params=_cparams("parallel", "parallel"),
        name="even_proj",
    )(x, gain, shift, scale, w_ext, w_vt, cos2, sin2, q_gain, k_gain)


def _odd_proj_kernel(x_ref, gain_ref, sh_ref, sc_ref, w_ref, q_ref, k_ref, v_ref, zd_ref):
    h = _norm_mod(x_ref[0], gain_ref[...], sh_ref[0], sc_ref[0])
    z = jnp.dot(h.astype(BF16), w_ref[...], preferred_element_type=F32)
    for i in range(C_HEADS):
        lo = i * HEAD_DIM
        q_ref[0, i] = (z[:, lo:lo + HEAD_DIM] * SCALE).astype(BF16)
        k_ref[0, i] = z[:, C_W + lo:C_W + lo + HEAD_DIM].astype(BF16)
        v_ref[0, i] = z[:, 2 * C_W + lo:2 * C_W + lo + HEAD_DIM].astype(BF16)
    zd_ref[0] = z[:, 3 * C_W:]


def odd_proj(x, gain, shift, scale, w):
    b, s, d = x.shape
    n = w.shape[1]
    tm = _pick_tile(s, 512)
    vec = pl.BlockSpec((1, 1, d), lambda bi, i: (bi, 0, 0))
    hm = pl.BlockSpec((1, C_HEADS, tm, HEAD_DIM), lambda bi, i: (bi, 0, i, 0))
    hshape = jax.ShapeDtypeStruct((b, C_HEADS, s, HEAD_DIM), BF16)
    return pl.pallas_call(
        _odd_proj_kernel,
        grid=(b, s // tm),
        in_specs=[
            pl.BlockSpec((1, tm, d), lambda bi, i: (bi, i, 0)),
            pl.BlockSpec((1, d), lambda bi, i: (0, 0)),
            vec, vec,
            pl.BlockSpec((d, n), lambda bi, i: (0, 0)),
        ],
        out_specs=[hm, hm, hm, pl.BlockSpec((1, tm, D_SHIFT_W), lambda bi, i: (bi, i, 0))],
        out_shape=[hshape, hshape, hshape, jax.ShapeDtypeStruct((b, s, D_SHIFT_W), F32)],
        compiler_params=_cparams("parallel", "parallel"),
        name="odd_proj",
    )(x, gain, shift, scale, w)


def _out_proj_kernel(x_ref, g_ref, y1_ref, y2_ref, w1_ref, w2_ref, o_ref, buf1_ref, buf2_ref, *, y2_heads):
    def gather_heads(y_ref, buf_ref):
        for h in range(y_ref.shape[1]):
            buf_ref[:, h * HEAD_DIM:(h + 1) * HEAD_DIM] = y_ref[0, h]
        return buf_ref[...]

    acc = jnp.dot(gather_heads(y1_ref, buf1_ref), w1_ref[...], preferred_element_type=F32)
    y2 = gather_heads(y2_ref, buf2_ref) if y2_heads else y2_ref[0]
    acc = acc + jnp.dot(y2, w2_ref[...], preferred_element_type=F32)
    o_ref[0] = x_ref[0] + g_ref[0] * acc


def out_proj(x, gate, y1, y2, w):
    b, s, d = x.shape
    nh = y1.shape[1]
    half = nh * HEAD_DIM
    tm = _pick_tile(s, 1024)
    y2_heads = y2.ndim == 4
    hm = pl.BlockSpec((1, nh, tm, HEAD_DIM), lambda bi, i: (bi, 0, i, 0))
    wspec = pl.BlockSpec((half, d), lambda bi, i: (0, 0))
    y2_spec = hm if y2_heads else pl.BlockSpec((1, tm, half), lambda bi, i: (bi, i, 0))
    return pl.pallas_call(
        functools.partial(_out_proj_kernel, y2_heads=y2_heads),
        grid=(b, s // tm),
        in_specs=[
            pl.BlockSpec((1, tm, d), lambda bi, i: (bi, i, 0)),
            pl.BlockSpec((1, 1, d), lambda bi, i: (bi, 0, 0)),
            hm, y2_spec, wspec, wspec,
        ],
        out_specs=pl.BlockSpec((1, tm, d), lambda bi, i: (bi, i, 0)),
        out_shape=jax.ShapeDtypeStruct((b, s, d), F32),
        scratch_shapes=[pltpu.VMEM((tm, half), BF16), pltpu.VMEM((tm, half), BF16)],
        compiler_params=_cparams("parallel", "parallel"),
        name="out_proj",
    )(x, gate, y1, y2, w[:half], w[half:])


def _ffn_kernel(x_ref, gain_ref, sh_ref, sc_ref, g_ref, wg_ref, wu_ref, wo_ref, o_ref, h_ref, acc_ref):
    j = pl.program_id(2)

    @pl.when(j == 0)
    def _():
        h_ref[...] = _norm_mod(x_ref[0], gain_ref[...], sh_ref[0], sc_ref[0]).astype(BF16)
        acc_ref[...] = jnp.zeros_like(acc_ref)

    h = h_ref[...]
    gate = jnp.dot(h, wg_ref[...], preferred_element_type=F32)
    up = jnp.dot(h, wu_ref[...], preferred_element_type=F32)
    act = gate * (1.0 / (1.0 + jnp.exp(-gate))) * up
    acc_ref[...] += jnp.dot(act.astype(BF16), wo_ref[...], preferred_element_type=F32)

    @pl.when(j == pl.num_programs(2) - 1)
    def _():
        o_ref[0] = x_ref[0] + g_ref[0] * acc_ref[...]


def ffn(x, gain, shift, scale, gate, w_in, w_out):
    b, s, d = x.shape
    f = w_out.shape[0]
    tm = _pick_tile(s, 512)
    th = 1408 if f % 1408 == 0 else _pick_tile(f, 256)
    nj = f // th
    vec = pl.BlockSpec((1, 1, d), lambda bi, i, j: (bi, 0, 0))
    return pl.pallas_call(
        _ffn_kernel,
        grid=(b, s // tm, nj),
        in_specs=[
            pl.BlockSpec((1, tm, d), lambda bi, i, j: (bi, i, 0)),
            pl.BlockSpec((1, d), lambda bi, i, j: (0, 0)),
            vec, vec, vec,
            pl.BlockSpec((d, th), lambda bi, i, j: (0, j)),
            pl.BlockSpec((d, th), lambda bi, i, j: (0, j + nj)),
            pl.BlockSpec((th, d), lambda bi, i, j: (j, 0)),
        ],
        out_specs=pl.BlockSpec((1, tm, d), lambda bi, i, j: (bi, i, 0)),
        out_shape=jax.ShapeDtypeStruct((b, s, d), F32),
        scratch_shapes=[pltpu.VMEM((tm, d), BF16), pltpu.VMEM((tm, d), F32)],
        compiler_params=_cparams("parallel", "parallel", "arbitrary"),
        name="ffn",
    )(x, gain, shift, scale, gate, w_in, w_in, w_out)


def _rms_kernel(x_ref, gain_ref, o_ref):
    x = x_ref[0]
    ms = jnp.mean(x * x, axis=-1, keepdims=True)
    o_ref[0] = x * lax.rsqrt(ms + RMS_EPS) * gain_ref[...]


def final_norm(x, gain):
    b, s, d = x.shape
    tm = _pick_tile(s, 1024)
    return pl.pallas_call(
        _rms_kernel,
        grid=(b, s // tm),
        in_specs=[pl.BlockSpec((1, tm, d), lambda bi, i: (bi, i, 0)),
                  pl.BlockSpec((1, d), lambda bi, i: (0, 0))],
        out_specs=pl.BlockSpec((1, tm, d), lambda bi, i: (bi, i, 0)),
        out_shape=jax.ShapeDtypeStruct((b, s, d), F32),
        compiler_params=_cparams("parallel", "parallel"),
        name="final_norm",
    )(x, gain)


def _flash_kernel(q_ref, k_ref, vt_ref, kx_ref, vxt_ref, o_ref, m_ref, acc_ref):
    kv = pl.program_id(3)
    g, tq, hd = q_ref.shape[1:]

    @pl.when(kv == 0)
    def _():
        m_ref[...] = jnp.full_like(m_ref, -jnp.inf)
        acc_ref[...] = jnp.zeros_like(acc_ref)

    q = q_ref[0].reshape(g * tq, hd)

    def scores(k):
        return _nt_dot(k, q)

    def accumulate(s, vt):
        m_prev = m_ref[...]
        m_new = jnp.maximum(m_prev, jnp.max(s, axis=0, keepdims=True))
        p = jnp.exp2(s - m_new).astype(BF16)
        acc_ref[...] = jnp.exp2(m_prev - m_new) * acc_ref[...] + jnp.dot(vt, p, preferred_element_type=F32)
        m_ref[...] = m_new

    tk = k_ref.shape[2]
    sub = min(tk, FLASH_SUB_KEYS)
    nsub = tk // sub
    s_next = scores(k_ref[0, 0, 0:sub, :])
    for c in range(nsub):
        s_cur = s_next
        if c + 1 < nsub:
            s_next = scores(k_ref[0, 0, (c + 1) * sub:(c + 2) * sub, :])
        accumulate(s_cur, vt_ref[0, 0, :, c * sub:(c + 1) * sub])

    @pl.when(kv == pl.num_programs(3) - 1)
    def _():
        accumulate(scores(kx_ref[0, 0]), vxt_ref[0, 0])
        acc = acc_ref[...]
        ot = (acc[:hd] / acc[hd:hd + 1]).astype(BF16)
        _transpose_heads(ot, o_ref.at[0], g, tq)


def global_attention(q, k, vt, kx, vxt):
    b, h, s, hd = q.shape
    hkv = k.shape[1]
    n_ctx = kx.shape[2]
    vr = vt.shape[2]
    g = h // hkv
    tq = _pick_tile(s, 1024)
    tk = _pick_tile(s, 2048)
    q_spec = pl.BlockSpec((1, g, tq, hd), lambda bi, gi, i, j: (bi, gi, i, 0))
    return pl.pallas_call(
        _flash_kernel,
        grid=(b, hkv, s // tq, s // tk),
        in_specs=[
            q_spec,
            pl.BlockSpec((1, 1, tk, hd), lambda bi, gi, i, j: (bi, gi, j, 0)),
            pl.BlockSpec((1, 1, vr, tk), lambda bi, gi, i, j: (bi, gi, 0, j)),
            pl.BlockSpec((1, 1, n_ctx, hd), lambda bi, gi, i, j: (bi, gi, 0, 0)),
            pl.BlockSpec((1, 1, vr, n_ctx), lambda bi, gi, i, j: (bi, gi, 0, 0)),
        ],
        out_specs=q_spec,
        out_shape=jax.ShapeDtypeStruct((b, h, s, hd), BF16),
        scratch_shapes=[pltpu.VMEM((1, g * tq), F32), pltpu.VMEM((vr, g * tq), F32)],
        compiler_params=_cparams("parallel", "parallel", "parallel", "arbitrary"),
        name="global_attention",
    )(q, k, vt, kx, vxt)


def _window_kernel(sink_ref, band_ref, q_ref, kp_ref, kc_ref, kn_ref, vp_ref, vc_ref, vn_ref, kx_ref, vxt_ref,
                   o_ref, *, seq):
    gi = pl.program_id(1)
    i = pl.program_id(2)
    g, tq, hd = q_ref.shape[1:]
    w = kp_ref.shape[2]
    nq = g * tq
    q = q_ref[0].reshape(nq, hd)
    k = jnp.concatenate([kp_ref[0, 0], kc_ref[0, 0], kn_ref[0, 0]], axis=0)
    vt = jnp.concatenate([vp_ref[0, 0], vc_ref[0, 0], vn_ref[0, 0]], axis=1)
    span = tq + 2 * w
    kpos = i * tq - w + lax.broadcasted_iota(jnp.int32, (span, 1), 0)
    outside = jnp.where((kpos >= 0) & (kpos < seq), 0.0, NEG_INF)
    s_loc = _nt_dot(k, q) + band_ref[...] + outside
    s_ctx = _nt_dot(kx_ref[0, 0], q)
    head = lax.broadcasted_iota(jnp.int32, (1, nq), 1) // tq
    sink = jnp.zeros((1, nq), F32)
    for hh in range(g):
        sink = jnp.where(head == hh, sink_ref[gi * g + hh], sink)
    m = jnp.maximum(jnp.maximum(jnp.max(s_loc, axis=0, keepdims=True),
                                jnp.max(s_ctx, axis=0, keepdims=True)), sink)
    acc = jnp.dot(vt, jnp.exp(s_loc - m).astype(BF16), preferred_element_type=F32)
    acc = acc + jnp.dot(vxt_ref[0, 0], jnp.exp(s_ctx - m).astype(BF16), preferred_element_type=F32)
    l = acc[hd:hd + 1] + jnp.exp(sink - m)
    _transpose_heads((acc[:hd] / l).astype(BF16), o_ref.at[0], g, tq)


def window_attention(q, k, vt, kx, vxt, sink):
    b, h, s, hd = q.shape
    hkv = k.shape[1]
    vr = vt.shape[2]
    g = h // hkv
    n_ctx = kx.shape[2]
    tq = _pick_tile(s, 256)
    r = tq // WINDOW
    nwb = s // WINDOW

    def before(i):
        return jnp.maximum(i * r - 1, 0)

    def after(i):
        return jnp.minimum((i + 1) * r, nwb - 1)

    span = tq + 2 * WINDOW
    krow = np.arange(span)[:, None] - WINDOW
    qcol = np.arange(g * tq)[None, :] % tq
    band = jnp.asarray(np.where(np.abs(krow - qcol) <= WINDOW, 0.0, NEG_INF), F32)
    return pl.pallas_call(
        functools.partial(_window_kernel, seq=s),
        grid=(b, hkv, s // tq),
        in_specs=[
            pl.BlockSpec(memory_space=pltpu.SMEM),
            pl.BlockSpec((span, g * tq), lambda bi, gi, i: (0, 0)),
            pl.BlockSpec((1, g, tq, hd), lambda bi, gi, i: (bi, gi, i, 0)),
            pl.BlockSpec((1, 1, WINDOW, hd), lambda bi, gi, i: (bi, gi, before(i), 0)),
            pl.BlockSpec((1, 1, tq, hd), lambda bi, gi, i: (bi, gi, i, 0)),
            pl.BlockSpec((1, 1, WINDOW, hd), lambda bi, gi, i: (bi, gi, after(i), 0)),
            pl.BlockSpec((1, 1, vr, WINDOW), lambda bi, gi, i: (bi, gi, 0, before(i))),
            pl.BlockSpec((1, 1, vr, tq), lambda bi, gi, i: (bi, gi, 0, i)),
            pl.BlockSpec((1, 1, vr, WINDOW), lambda bi, gi, i: (bi, gi, 0, after(i))),
            pl.BlockSpec((1, 1, n_ctx, hd), lambda bi, gi, i: (bi, gi, 0, 0)),
            pl.BlockSpec((1, 1, vr, n_ctx), lambda bi, gi, i: (bi, gi, 0, 0)),
        ],
        out_specs=pl.BlockSpec((1, g, tq, hd), lambda bi, gi, i: (bi, gi, i, 0)),
        out_shape=jax.ShapeDtypeStruct((b, h, s, hd), BF16),
        compiler_params=_cparams("parallel", "parallel", "parallel"),
        name="window_attention",
    )(sink, band, q, k, k, k, vt, vt, vt, kx, vxt)


def _ctx_attn_kernel(sink_ref, q_ref, k_ref, v_ref, o_ref, *, base2, feature_major):
    hi = pl.program_id(1)
    s = _nt_dot(q_ref[0, 0], k_ref[0, 0])
    sink = sink_ref[hi]
    m = jnp.maximum(jnp.max(s, axis=-1, keepdims=True), sink)
    ex = jnp.exp2 if base2 else jnp.exp
    p = ex(s - m)
    l = jnp.sum(p, axis=-1, keepdims=True) + ex(sink - m)
    if feature_major:
        o = _nt_dot(p.astype(BF16), v_ref[0, 0, :HEAD_DIM, :])
    else:
        o = jnp.dot(p.astype(BF16), v_ref[0, 0], preferred_element_type=F32)
    o_ref[0, 0] = (o / l).astype(o_ref.dtype)


def ctx_attention(q, k, v, sink, base2=False, feature_major=False):
    b, h, n, hd = q.shape
    g = h // k.shape[1]
    kv = pl.BlockSpec((1, 1, n, hd), lambda bi, hi: (bi, hi // g, 0, 0))
    vs = pl.BlockSpec((1, 1, v.shape[2], n), lambda bi, hi: (bi, hi // g, 0, 0)) if feature_major else kv
    qs = pl.BlockSpec((1, 1, n, hd), lambda bi, hi: (bi, hi, 0, 0))
    return pl.pallas_call(
        functools.partial(_ctx_attn_kernel, base2=base2, feature_major=feature_major),
        grid=(b, h),
        in_specs=[pl.BlockSpec(memory_space=pltpu.SMEM), qs, kv, vs],
        out_specs=qs,
        out_shape=jax.ShapeDtypeStruct((b, h, n, hd), BF16),
        compiler_params=_cparams("parallel", "parallel"),
        name="ctx_attention",
    )(sink, q, k, v)


def _na_kernel(q_ref, k_ref, v_ref, kx_ref, vx_ref, bias_ref, o_ref, *, rows):
    t = pl.program_id(2)
    nkeys = NA_KEY_ROWS * GRID_W
    kstart = jnp.clip(t * NA_TILE_ROWS - NA_ROWS // 2, 0, rows - NA_KEY_ROWS)
    off = pl.multiple_of(kstart * GRID_W, GRID_W)
    q = q_ref[0, 0]
    tq = q.shape[0]
    kw = k_ref[0, 0, pl.ds(off, nkeys), :]
    vw = v_ref[0, 0, pl.ds(off, nkeys), :]
    s_loc = _nt_dot(kw, q) + bias_ref[0, 0]
    s_ctx = _nt_dot(kx_ref[0, 0], q)
    m = jnp.maximum(jnp.max(s_loc, axis=0, keepdims=True), jnp.max(s_ctx, axis=0, keepdims=True))
    p_loc = jnp.exp(s_loc - m)
    p_ctx = jnp.exp(s_ctx - m)
    l = jnp.sum(p_loc, axis=0, keepdims=True) + jnp.sum(p_ctx, axis=0, keepdims=True)
    ot = _tn_dot(vw, p_loc.astype(BF16)) + _tn_dot(vx_ref[0, 0], p_ctx.astype(BF16))
    _transpose_heads((ot / l).astype(BF16), o_ref.at[0], 1, tq)


def na_bias_tables(rpb, rows):
    nh = rpb.shape[0]
    i = np.arange(NA_TILE_ROWS)
    j = np.arange(NA_KEY_ROWS)
    qc = np.arange(GRID_W)
    kc = np.arange(GRID_W)
    cs = np.clip(qc - NA_COLS // 2, 0, GRID_W - NA_COLS)
    col_valid = (kc[:, None] >= cs[None, :]) & (kc[:, None] < cs[None, :] + NA_COLS)
    dcol = np.clip(kc[:, None] - qc[None, :] + NA_COLS - 1, 0, 2 * NA_COLS - 2)
    per_drow = jnp.where(col_valid[None, None], rpb.astype(F32)[:, :, dcol], NEG_INF)
    masked = jnp.full((nh, 1, GRID_W, GRID_W), NEG_INF, F32)
    per_drow = jnp.concatenate([per_drow, masked], axis=1)
    out = []
    for r0 in (0, NA_TILE_ROWS, rows - NA_TILE_ROWS):
        r = r0 + i
        rs = np.clip(r - NA_ROWS // 2, 0, rows - NA_ROWS)
        kstart = int(np.clip(r0 - NA_ROWS // 2, 0, rows - NA_KEY_ROWS))
        krow = kstart + j
        row_valid = (krow[:, None] >= rs[None, :]) & (krow[:, None] < rs[None, :] + NA_ROWS)
        drow = np.where(row_valid, krow[:, None] - r[None, :] + NA_ROWS - 1, 2 * NA_ROWS - 1)
        cols = [per_drow[:, drow[:, ii]] for ii in range(NA_TILE_ROWS)]
        dense = jnp.stack(cols, axis=3)
        out.append(dense.reshape(nh, NA_KEY_ROWS * GRID_W, NA_TILE_ROWS * GRID_W))
    return jnp.stack(out)


def neighborhood_attention(q, k, v, kx, vx, bias):
    b, h, s, hd = q.shape
    rows = s // GRID_W
    nt = rows // NA_TILE_ROWS
    tq = NA_TILE_ROWS * GRID_W
    nkeys = NA_KEY_ROWS * GRID_W
    n_ctx = kx.shape[2]
    full = pl.BlockSpec((1, 1, s, hd), lambda bi, hi, t: (bi, hi, 0, 0))
    cx = pl.BlockSpec((1, 1, n_ctx, hd), lambda bi, hi, t: (bi, hi, 0, 0))

    def case(t):
        return jnp.where(t == 0, 0, jnp.where(t == nt - 1, 2, 1))

    return pl.pallas_call(
        functools.partial(_na_kernel, rows=rows),
        grid=(b, h, nt),
        in_specs=[
            pl.BlockSpec((1, 1, tq, hd), lambda bi, hi, t: (bi, hi, t, 0)),
            full, full, cx, cx,
            pl.BlockSpec((1, 1, nkeys, tq), lambda bi, hi, t: (case(t), hi, 0, 0)),
        ],
        out_specs=pl.BlockSpec((1, 1, tq, hd), lambda bi, hi, t: (bi, hi, t, 0)),
        out_shape=jax.ShapeDtypeStruct((b, h, s, hd), BF16),
        compiler_params=_cparams("parallel", "parallel", "arbitrary"),
        name="neighborhood_attention",
    )(q, k, v, kx, vx, bias)


def _mm(a, b, exact=False):
    if exact:
        return jnp.dot(a, b, preferred_element_type=F32, precision=lax.Precision.HIGHEST)
    return jnp.dot(a.astype(BF16), b.astype(BF16), preferred_element_type=F32)


def _mm_nt(a, b):
    return lax.dot_general(a.astype(BF16), b.astype(BF16), (((1,), (1,)), ((), ())),
                           preferred_element_type=F32)


def _mm_tn(a, b):
    return lax.dot_general(a.astype(BF16), b.astype(BF16), (((0,), (0,)), ((), ())),
                           preferred_element_type=F32)


def _rwkv_chains(chains):
    t, w = chains[0][0].shape
    nh = w // HEAD_DIM
    n = range(len(chains))
    lw, kd, a, r, v, kap, st, rev = (list(col) for col in zip(*chains))
    row = lax.broadcasted_iota(jnp.int32, (t, w), 0)
    colj = lax.broadcasted_iota(jnp.int32, (t, w), 1) % t
    lane_head = lax.broadcasted_iota(jnp.int32, (1, w), 1) // HEAD_DIM
    strict_d = (colj < row, colj > row)
    incl_d = (colj <= row, colj >= row)
    tri_d = tuple(m[:, :t].astype(F32) for m in incl_d)
    strict = [strict_d[int(rev[i])] for i in n]
    incl = [incl_d[int(rev[i])] for i in n]

    def bd(y):
        return jnp.concatenate([jnp.where(lane_head == h, y, 0.0) for h in range(nh)], axis=0)

    def bdmm(xs, ys):
        return [_mm(xs[i], bd(ys[i])) for i in n]

    c_incl = [_mm(tri_d[int(rev[i])], lw[i], exact=True) for i in n]
    c_excl = [c_incl[i] - lw[i] for i in n]
    c_mid = [c_incl[i][t // 2:t // 2 + 1, :] for i in n]
    c_end = [c_incl[i][0:1, :] if rev[i] else c_incl[i][t - 1:t, :] for i in n]
    bvec = [kap[i] * a[i] for i in n]
    e_neg = [jnp.exp(c_mid[i] - c_incl[i]) for i in n]
    left = [jnp.concatenate([kap[i] * jnp.exp(c_excl[i] - c_mid[i]), r[i] * jnp.exp(c_incl[i] - c_mid[i])],
                            axis=0) for i in n]
    wt = [jnp.concatenate([bd(bvec[i] * e_neg[i]), bd(kd[i] * e_neg[i])], axis=0) for i in n]
    m = [_mm_nt(left[i], wt[i]) for i in n]
    l_b = [jnp.where(strict[i], m[i][:t, :w], 0.0) for i in n]
    l_k = [jnp.where(strict[i], m[i][:t, w:], 0.0) for i in n]
    m_br = [jnp.where(incl[i], m[i][t:, :w], 0.0) for i in n]
    m_kr = [jnp.where(incl[i], m[i][t:, w:], 0.0) for i in n]

    blk = RWKV_INV_BLOCK
    same = row // blk == colj // blk
    eye = jnp.where(row == colj, 1.0, 0.0)
    pw = [-jnp.where(same, l_b[i], 0.0) for i in n]
    inv = [eye + pw[i] for i in n]
    span = 2
    while span < blk:
        pw = bdmm(pw, pw)
        step = bdmm(inv, pw)
        inv = [inv[i] + step[i] for i in n]
        span *= 2
    while blk < t:
        pair = (row // (2 * blk) == colj // (2 * blk)) & (row // blk != colj // blk)
        off = [jnp.where(pair, l_b[i], 0.0) for i in n]
        step = bdmm(bdmm(inv, off), inv)
        inv = [inv[i] - step[i] for i in n]
        blk *= 2

    kt = [kap[i] * jnp.exp(c_excl[i]) for i in n]
    rt = [r[i] * jnp.exp(c_incl[i]) for i in n]
    lkv = bdmm(l_k, v)
    mkv = bdmm(m_kr, v)
    wk = bdmm(inv, kt)
    u0 = bdmm(inv, lkv)
    mbw = bdmm(m_br, wk)
    mbu = bdmm(m_br, u0)
    rq = [rt[i] - mbw[i] for i in n]
    y0 = [mkv[i] - mbu[i] for i in n]
    e_end = [jnp.exp(c_end[i] - c_incl[i]) for i in n]
    b_hat = [bvec[i] * e_end[i] for i in n]
    k_hat = [kd[i] * e_end[i] for i in n]
    ri = lax.broadcasted_iota(jnp.int32, (w, w), 0)
    ci = lax.broadcasted_iota(jnp.int32, (w, w), 1)
    same_head = ri // HEAD_DIM == ci // HEAD_DIM
    diag = ri == ci
    bw = [_mm_tn(b_hat[i], wk[i]) for i in n]
    kv = [_mm_tn(k_hat[i], v[i]) for i in n]
    bu = [_mm_tn(b_hat[i], u0[i]) for i in n]
    gt = [jnp.where(diag, jnp.exp(c_end[i]), 0.0) - jnp.where(same_head, bw[i], 0.0) for i in n]
    c0t = [jnp.where(same_head, kv[i] - bu[i], 0.0) for i in n]
    ys = [_mm(rq[i], st[i]) + y0[i] for i in n]
    sts = [_mm(gt[i], st[i]) + c0t[i] for i in n]
    return ys, sts


def _rwkv_kernel(lwf_ref, kdf_ref, af_ref, lwb_ref, kdb_ref, ab_ref, rf_ref, vf_ref, kkf_ref,
                 rb_ref, vb_ref, kkb_ref, sin_ref, yf_ref, yb_ref, sout_ref, st_ref):
    c = pl.program_id(0)

    @pl.when(c == 0)
    def _():
        st_ref[...] = sin_ref[...]

    bsz = lwf_ref.shape[0]
    groups = lwf_ref.shape[2] // RWKV_LANES
    chains, slots = [], []
    for bi in range(bsz):
        for d, (lw_ref, kd_ref, a_ref, r_ref, v_ref, kk_ref, y_ref) in enumerate((
                (lwf_ref, kdf_ref, af_ref, rf_ref, vf_ref, kkf_ref, yf_ref),
                (lwb_ref, kdb_ref, ab_ref, rb_ref, vb_ref, kkb_ref, yb_ref))):
            for hg in range(groups):
                lanes = slice(hg * RWKV_LANES, (hg + 1) * RWKV_LANES)
                chain = d * groups + hg
                chains.append((lw_ref[bi, :, lanes], kd_ref[bi, :, lanes], a_ref[bi, :, lanes],
                               r_ref[bi, :, lanes], v_ref[bi, :, lanes], kk_ref[bi, :, lanes],
                               st_ref[bi, chain], bool(d)))
                slots.append((y_ref, bi, lanes, chain))
    ys, sts = _rwkv_chains(chains)
    for (y_ref, bi, lanes, chain), y, st in zip(slots, ys, sts):
        y_ref[bi, :, lanes] = y
        st_ref[bi, chain] = st

    @pl.when(c == pl.num_programs(0) - 1)
    def _():
        sout_ref[...] = st_ref[...]


def rwkv_scan(lw, kd, a, r, v, kk, state):
    b, length, w = r.shape
    t = RWKV_CHUNK
    nc = length // t
    fwd = pl.BlockSpec((b, t, w), lambda c: (0, c, 0))
    bwd = pl.BlockSpec((b, t, w), lambda c: (0, nc - 1 - c, 0))
    st_spec = pl.BlockSpec(state.shape, lambda c: (0, 0, 0, 0))
    y_shape = jax.ShapeDtypeStruct((b, length, w), F32)
    return pl.pallas_call(
        _rwkv_kernel,
        grid=(nc,),
        in_specs=[fwd, fwd, fwd, bwd, bwd, bwd, fwd, fwd, fwd, bwd, bwd, bwd, st_spec],
        out_specs=[fwd, bwd, st_spec],
        out_shape=[y_shape, y_shape, jax.ShapeDtypeStruct(state.shape, F32)],
        scratch_shapes=[pltpu.VMEM(state.shape, F32)],
        compiler_params=_cparams("arbitrary"),
        name="rwkv_scan",
    )(lw[0], kd[0], a[0], lw[1], kd[1], a[1], r, v, kk, r, v, kk, state)


def _exact_dot(a, b):
    return jnp.dot(a, b, preferred_element_type=F32, precision=lax.Precision.HIGHEST)


def _head_sum_matrix(width, value):
    r = lax.broadcasted_iota(jnp.int32, (width, width), 0) // HEAD_DIM
    c = lax.broadcasted_iota(jnp.int32, (width, width), 1) // HEAD_DIM
    return jnp.where(r == c, value, 0.0).astype(BF16)


def _head_sum(x, mat):
    hi = x.astype(BF16)
    lo = (x - hi.astype(F32)).astype(BF16)
    return jnp.dot(hi, mat, preferred_element_type=F32) + jnp.dot(lo, mat, preferred_element_type=F32)


def _sigmoid(x):
    return 1.0 / (1.0 + jnp.exp(-x))


def _rwkv_prep_kernel(z_ref, zp_ref, zn_ref, mu_ref, kk_ref, ka_ref, w0_ref, w2_ref, a0_ref, a2_ref, g2_ref,
                      r_ref, v_ref, kn_ref, g_ref, lw0_ref, kd0_ref, av0_ref, lw1_ref, kd1_ref, av1_ref):
    i = pl.program_id(1)
    z = z_ref[0]
    tm = z.shape[0]
    halo = zp_ref.shape[1]
    row = lax.broadcasted_iota(jnp.int32, (tm, 1), 0)
    before = jnp.where(i > 0, zp_ref[0, halo - 1:halo, :], 0.0)
    after = jnp.where(i < pl.num_programs(1) - 1, zn_ref[0, 0:1, :], 0.0)
    z_prev = jnp.where(row == 0, before, pltpu.roll(z, 1, 0))
    z_next = jnp.where(row == tm - 1, after, pltpu.roll(z, tm - 1, 0))
    zs = z + (0.5 * (z_prev + z_next) - z) * mu_ref[...]

    r = zs[:, :D_W]
    k = zs[:, D_W:2 * D_W]
    v = zs[:, 2 * D_W:3 * D_W]
    o = 3 * D_W
    wd = zs[:, o:o + DECAY_LORA]
    ad = zs[:, o + DECAY_LORA:o + DECAY_LORA + ICLR_LORA]
    gd = zs[:, o + DECAY_LORA + ICLR_LORA:]
    r_ref[0] = r
    v_ref[0] = v
    g_ref[0] = _exact_dot(_sigmoid(gd), g2_ref[...])
    kraw = k * kk_ref[...]
    ss = _head_sum(kraw * kraw, _head_sum_matrix(D_W, 1.0))
    kn_ref[0] = kraw / jnp.maximum(jnp.sqrt(ss), 1e-12)
    twd = jnp.tanh(wd)
    for d, (lw_ref, kd_ref, av_ref) in enumerate(((lw0_ref, kd0_ref, av0_ref), (lw1_ref, kd1_ref, av1_ref))):
        x = -(w0_ref[d:d + 1, :] + _exact_dot(twd, w2_ref[d]))
        softplus = jnp.maximum(x, 0.0) + jnp.log(1.0 + jnp.exp(-jnp.abs(x)))
        lw_ref[0] = -jnp.exp(-softplus - 0.5)
        a = _sigmoid(a0_ref[d:d + 1, :] + _exact_dot(ad, a2_ref[d]))
        av_ref[0] = a
        kd_ref[0] = k * (1.0 + (a - 1.0) * ka_ref[...])


def rwkv_prep(zd, mu, k_k, k_a, w0, w2, a0, a2, g2):
    b, length, width = zd.shape
    tm = _pick_tile(length, 512)
    halo = 8
    nb = length // halo
    per = tm // halo
    row = pl.BlockSpec((1, tm, D_W), lambda bi, i: (bi, i, 0))
    shape = jax.ShapeDtypeStruct((b, length, D_W), F32)

    def full(a):
        return pl.BlockSpec(a.shape, lambda bi, i: (0,) * a.ndim)

    params = (mu, k_k, k_a, w0, w2, a0, a2, g2)
    return pl.pallas_call(
        _rwkv_prep_kernel,
        grid=(b, length // tm),
        in_specs=[
            pl.BlockSpec((1, tm, width), lambda bi, i: (bi, i, 0)),
            pl.BlockSpec((1, halo, width), lambda bi, i: (bi, jnp.maximum(i * per - 1, 0), 0)),
            pl.BlockSpec((1, halo, width), lambda bi, i: (bi, jnp.minimum((i + 1) * per, nb - 1), 0)),
        ] + [full(p) for p in params],
        out_specs=[row] * 10,
        out_shape=[shape] * 10,
        compiler_params=_cparams("parallel", "parallel"),
        name="rwkv_prep",
    )(zd, zd, zd, *params)


def _rwkv_readout_kernel(yf_ref, yb_ref, r_ref, v_ref, kd0_ref, kd1_ref, g_ref, rk_ref, lnw_ref, lnb_ref, o_ref):
    y = yf_ref[0] + yb_ref[0]
    avg = _head_sum_matrix(D_W, 1.0 / HEAD_DIM)
    mean = _head_sum(y, avg)
    yc = y - mean
    var = _head_sum(yc * yc, avg)
    yn = yc * lax.rsqrt(var + GN_EPS) * lnw_ref[...] + lnb_ref[...]
    bonus = _head_sum(r_ref[0] * (kd0_ref[0] + kd1_ref[0]) * rk_ref[...], _head_sum_matrix(D_W, 1.0))
    o_ref[0] = ((yn + bonus * v_ref[0]) * g_ref[0]).astype(o_ref.dtype)


def rwkv_readout(yf, yb, r, v, kd0, kd1, g, r_k, ln_w, ln_b):
    b, length, w = yf.shape
    tm = _pick_tile(length, 512)
    row = pl.BlockSpec((1, tm, w), lambda bi, i: (bi, i, 0))
    vec = pl.BlockSpec((1, w), lambda bi, i: (0, 0))
    return pl.pallas_call(
        _rwkv_readout_kernel,
        grid=(b, length // tm),
        in_specs=[row] * 7 + [vec] * 3,
        out_specs=row,
        out_shape=jax.ShapeDtypeStruct((b, length, w), BF16),
        compiler_params=_cparams("parallel", "parallel"),
        name="rwkv_readout",
    )(yf, yb, r, v, kd0, kd1, g, r_k, ln_w, ln_b)


def _rope_tables(n_tokens):
    t = jnp.arange(n_tokens, dtype=jnp.int32)
    row = (t // GRID_W).astype(F32)
    col = (t % GRID_W).astype(F32)
    inv = ROPE_THETA ** (-jnp.arange(ROPE_FREQS, dtype=F32) / ROPE_FREQS)
    ang = jnp.concatenate([row[:, None] * inv, col[:, None] * inv], axis=-1)
    cos, sin = jnp.cos(ang), jnp.sin(ang)
    return jnp.concatenate([cos, cos], axis=-1), jnp.concatenate([sin, sin], axis=-1)


def _rwkv_mixer(zd, zdc, params, r_k, ln_w, ln_b, need_ctx):
    lat = rwkv_prep(zd, *params)
    cx = rwkv_prep(zdc, *params)
    bsz = zd.shape[0]
    chains = 2 * (D_W // RWKV_LANES)
    zero = jnp.zeros((bsz, chains, RWKV_LANES, RWKV_LANES), F32)

    def scan(p, state):
        r, v, kk, _, lw0, kd0, a0, lw1, kd1, a1 = p
        return rwkv_scan((lw0, lw1), (kd0, kd1), (a0, a1), r, v, kk, state)

    def readout(p, yf, yb):
        r, v, _, g, _, kd0, _, _, kd1, _ = p
        return rwkv_readout(yf, yb, r, v, kd0, kd1, g, r_k, ln_w, ln_b)

    ycf, ycb, state = scan(cx, zero)
    yf, yb, _ = scan(lat, state)
    y_d = readout(lat, yf, yb)
    yc_d = readout(cx, ycf, ycb) if need_ctx else None
    return y_d, yc_d


def kernel(x, c, ctx, c_ctx, w_mod, b_mod, norm_mix, norm_ffn, w_in_even, w_out_even, q_norm_a, k_norm_a,
           sink_b, w_in_odd, w_out_odd, rpb_c, shift_mu, decay_w0, decay_w2, iclr_a0, iclr_a2, gate_g2,
           k_k, k_a, r_k, ln_x_w, ln_x_b, w_ffn_in, w_ffn_out, norm_out):
    bsz, s, d = x.shape
    n_ctx = ctx.shape[1]
    cos2, sin2 = _rope_tables(s)
    cos_ctx = jnp.ones((n_ctx, HEAD_DIM), F32)
    sin_ctx = jnp.zeros((n_ctx, HEAD_DIM), F32)
    act = jnp.zeros((8, d), F32).at[:bsz].set(jax.nn.silu(c)).at[bsz].set(jax.nn.silu(c_ctx))
    mods = modulation(act.astype(BF16), w_mod, b_mod)
    w_out_even = w_out_even.astype(BF16)
    w_in_odd = w_in_odd.astype(BF16)
    w_out_odd = w_out_odd.astype(BF16)
    w_ffn_in = w_ffn_in.astype(BF16)
    w_ffn_out = w_ffn_out.astype(BF16)
    no_sink = jnp.full((A_HEADS,), NEG_INF, F32)
    for layer in range(DEPTH):
        need_ctx = layer < DEPTH - 1
        mod = mods[layer, :bsz][:, None, :]
        modc = jnp.broadcast_to(mods[layer, bsz][None, None, :], (bsz, 1, 6 * d))
        sh1, sc1, g1, sh2, sc2, g2 = jnp.split(mod, 6, axis=-1)
        csh1, csc1, cg1, csh2, csc2, cg2 = jnp.split(modc, 6, axis=-1)
        gain_mix = norm_mix[layer][None, :]
        gain_ffn = norm_ffn[layer][None, :]
        i = layer // 2
        if layer % 2 == 0:
            w_out = w_out_even[i]
            qg, kg = q_norm_a[i][None, :], k_norm_a[i][None, :]
            w_ext = even_weight(w_in_even[i], q_norm_a[i], k_norm_a[i])
            w_vt = jnp.concatenate([w_in_even[i][:, _EV_VA:_EV_VA + A_KV],
                                    w_in_even[i][:, _EV_VB:_EV_VB + B_KV]], axis=1).T.astype(BF16)
            qa, ka, vat, qb, kb, vbt = even_proj(x, gain_mix, sh1, sc1, w_ext, w_vt, cos2, sin2, qg, kg)
            qca, kca, vcat, qcb, kcb, vcbt = even_proj(ctx, gain_mix, csh1, csc1, w_ext, w_vt,
                                                        cos_ctx, sin_ctx, qg, kg)
            y1 = global_attention(qa, ka, vat, kca, vcat)
            y2 = window_attention(qb, kb, vbt, kcb, vcbt, sink_b[i])
            if need_ctx:
                yc1 = ctx_attention(qca, kca, vcat, no_sink, base2=True, feature_major=True)
                yc2 = ctx_attention(qcb, kcb, vcbt, sink_b[i], feature_major=True)
        else:
            w_out = w_out_odd[i]
            q, k, v, zd = odd_proj(x, gain_mix, sh1, sc1, w_in_odd[i])
            qc, kc, vc, zdc = odd_proj(ctx, gain_mix, csh1, csc1, w_in_odd[i])
            bias = na_bias_tables(rpb_c[i], s // GRID_W)
            y1 = neighborhood_attention(q, k, v, kc, vc, bias)
            params = (shift_mu[i][None, :], k_k[i][None, :], k_a[i][None, :], decay_w0[i], decay_w2[i],
                      iclr_a0[i], iclr_a2[i], gate_g2[i])
            y2, yc2 = _rwkv_mixer(zd, zdc, params, r_k[i].reshape(1, D_W), ln_x_w[i][None, :],
                                  ln_x_b[i][None, :], need_ctx)
            if need_ctx:
                yc1 = ctx_attention(qc, kc, vc, no_sink)
        x = out_proj(x, g1, y1, y2, w_out)
        x = ffn(x, gain_ffn, sh2, sc2, g2, w_ffn_in[layer], w_ffn_out[layer])
        if need_ctx:
            ctx = out_proj(ctx, cg1, yc1, yc2, w_out)
            ctx = ffn(ctx, gain_ffn, csh2, csc2, cg2, w_ffn_in[layer], w_ffn_out[layer])
    return final_norm(x, norm_out[None, :])
```

```python
import functools

import jax
import jax.numpy as jnp
import numpy as np
from jax import lax
from jax.experimental import pallas as pl
from jax.experimental.pallas import tpu as pltpu

F32 = jnp.float32
BF16 = jnp.bfloat16

D_MODEL = 1024
DEPTH = 4
GRID_W = 64
HEAD_DIM = 64
SCALE = HEAD_DIM ** -0.5
ROPE_HALF = HEAD_DIM // 2
ROPE_FREQS = HEAD_DIM // 4
ROPE_THETA = 10000.0
A_HEADS = 8
A_KV_HEADS = 2
B_HEADS = 8
B_KV_HEADS = 2
WINDOW = 128
C_HEADS = 8
NA_ROWS = 8
NA_COLS = 16
D_HEADS = 8
DECAY_LORA = 64
ICLR_LORA = 64
GATE_LORA = 128
GN_EPS = 64e-5
RMS_EPS = 1e-6
NEG_INF = -1e30
FFN_HIDDEN = -(-8 * D_MODEL // (3 * 256)) * 256

A_Q = A_HEADS * HEAD_DIM
A_KV = A_KV_HEADS * HEAD_DIM
B_Q = B_HEADS * HEAD_DIM
B_KV = B_KV_HEADS * HEAD_DIM
C_W = C_HEADS * HEAD_DIM
D_W = D_HEADS * HEAD_DIM
EVEN_WIDTHS = (A_Q, A_KV, A_KV, B_Q, B_KV, B_KV)
D_SHIFT_W = 3 * D_W + DECAY_LORA + ICLR_LORA + GATE_LORA

VMEM_LIMIT_BYTES = 56 * 1024 * 1024
RWKV_CHUNK = HEAD_DIM
RWKV_INV_BLOCK = 16
RWKV_LANES = 256
FLASH_SUB_KEYS = 512
LOG2E = 1.4426950408889634
VT_ROWS = HEAD_DIM + 16
NA_TILE_ROWS = 8
NA_KEY_ROWS = NA_TILE_ROWS + NA_ROWS


def _cparams(*sem):
    return pltpu.CompilerParams(dimension_semantics=sem, vmem_limit_bytes=VMEM_LIMIT_BYTES)


def _split(t, widths):
    return jnp.split(t, [int(o) for o in np.cumsum(widths)[:-1]], axis=-1)


def _pick_tile(n, target):
    t = min(n, target)
    while n % t:
        t //= 2
    return t


def _mod_kernel(a_ref, w_ref, b_ref, o_ref):
    w = w_ref[0].astype(BF16)
    o_ref[0] = jnp.dot(a_ref[...], w, preferred_element_type=F32) + b_ref[0]


def modulation(act, w_mod, b_mod):
    depth, d, n = w_mod.shape
    tn = 1536
    return pl.pallas_call(
        _mod_kernel,
        grid=(depth, n // tn),
        in_specs=[
            pl.BlockSpec((8, d), lambda l, j: (0, 0)),
            pl.BlockSpec((1, d, tn), lambda l, j: (l, 0, j)),
            pl.BlockSpec((1, 1, tn), lambda l, j: (l, 0, j)),
        ],
        out_specs=pl.BlockSpec((1, 8, tn), lambda l, j: (l, 0, j)),
        out_shape=jax.ShapeDtypeStruct((depth, 8, n), F32),
        compiler_params=_cparams("parallel", "parallel"),
        name="modulation",
    )(act, w_mod, b_mod.reshape(depth, 1, n))


def _norm_mod(x, gain, shift, scale):
    ms = jnp.mean(x * x, axis=-1, keepdims=True)
    h = x * lax.rsqrt(ms + RMS_EPS) * gain
    return h * (1.0 + scale) + shift


def _head_rms(z):
    return lax.rsqrt(jnp.mean(z * z, axis=-1, keepdims=True) + RMS_EPS)


def _nt_dot(a, b):
    return lax.dot_general(a, b, (((1,), (1,)), ((), ())), preferred_element_type=F32)


def _tn_dot(a, b):
    return lax.dot_general(a, b, (((0,), (0,)), ((), ())), preferred_element_type=F32)


def _transpose_heads(ot, o_ref, g, tq):
    tb = min(tq, 256)
    eye = (lax.broadcasted_iota(jnp.int32, (tb, tb), 0)
           == lax.broadcasted_iota(jnp.int32, (tb, tb), 1)).astype(BF16)
    for h in range(g):
        for j in range(tq // tb):
            lo = h * tq + j * tb
            o_ref[h, j * tb:(j + 1) * tb, :] = _nt_dot(eye, ot[:, lo:lo + tb]).astype(o_ref.dtype)


_EV_QA, _EV_KA, _EV_VA, _EV_QB, _EV_KB, _EV_VB = (int(o) for o in np.cumsum((0,) + EVEN_WIDTHS)[:-1])
_EV_QAR = sum(EVEN_WIDTHS)
_EV_KAR = _EV_QAR + A_Q
_EV_QBR = _EV_KAR + A_KV
_EV_KBR = _EV_QBR + B_Q
EVEN_EXT = _EV_KBR + B_KV


def _even_proj_kernel(x_ref, gain_ref, sh_ref, sc_ref, w_ref, wvt_ref, cos_ref, sin_ref, qg_ref, kg_ref,
                      qa_ref, ka_ref, vat_ref, qb_ref, kb_ref, vbt_ref):
    h = _norm_mod(x_ref[0], gain_ref[...], sh_ref[0], sc_ref[0]).astype(BF16)
    z = jnp.dot(h, w_ref[...], preferred_element_type=F32)
    vt = _nt_dot(wvt_ref[...], h)
    pad_rows = (lax.broadcasted_iota(jnp.int32, (VT_ROWS - HEAD_DIM, vt.shape[1]), 0) == 0).astype(BF16)
    cos = cos_ref[...]
    sin = sin_ref[...]
    qg = qg_ref[...]
    kg = kg_ref[...]

    def sl(off, i):
        return z[:, off + i * HEAD_DIM: off + (i + 1) * HEAD_DIM]

    for i in range(A_HEADS):
        zq = sl(_EV_QA, i)
        qa = _head_rms(zq) * (zq * qg * cos + sl(_EV_QAR, i) * sin)
        qa_ref[0, i] = (qa * (SCALE * LOG2E)).astype(BF16)
    for i in range(B_HEADS):
        qb_ref[0, i] = ((sl(_EV_QB, i) * cos + sl(_EV_QBR, i) * sin) * SCALE).astype(BF16)
    for i in range(A_KV_HEADS):
        zk = sl(_EV_KA, i)
        ka_ref[0, i] = (_head_rms(zk) * (zk * kg * cos + sl(_EV_KAR, i) * sin)).astype(BF16)
        vat_ref[0, i, :HEAD_DIM, :] = vt[i * HEAD_DIM:(i + 1) * HEAD_DIM].astype(BF16)
        vat_ref[0, i, HEAD_DIM:, :] = pad_rows
    for i in range(B_KV_HEADS):
        kb_ref[0, i] = (sl(_EV_KB, i) * cos + sl(_EV_KBR, i) * sin).astype(BF16)
        vbt_ref[0, i, :HEAD_DIM, :] = vt[A_KV + i * HEAD_DIM:A_KV + (i + 1) * HEAD_DIM].astype(BF16)
        vbt_ref[0, i, HEAD_DIM:, :] = pad_rows


def _rot_cols(w):
    d = w.shape[0]
    wh = w.reshape(d, -1, 2, ROPE_HALF)
    return jnp.stack([-wh[:, :, 1], wh[:, :, 0]], axis=2).reshape(w.shape)


def even_weight(w_in, q_gain, k_gain):
    qa, ka, va, qb, kb, vb = _split(w_in, EVEN_WIDTHS)
    qar = _rot_cols(qa * jnp.tile(q_gain, A_HEADS))
    kar = _rot_cols(ka * jnp.tile(k_gain, A_KV_HEADS))
    return jnp.concatenate([w_in, qar, kar, _rot_cols(qb), _rot_cols(kb)], axis=1).astype(BF16)


def even_proj(x, gain, shift, scale, w_ext, w_vt, cos2, sin2, q_gain, k_gain):
    b, s, d = x.shape
    tm = _pick_tile(s, 512)
    vec = pl.BlockSpec((1, 1, d), lambda bi, i: (bi, 0, 0))
    tab = pl.BlockSpec((tm, HEAD_DIM), lambda bi, i: (i, 0))
    hv = pl.BlockSpec((1, HEAD_DIM), lambda bi, i: (0, 0))

    def hm(n):
        return pl.BlockSpec((1, n, tm, HEAD_DIM), lambda bi, i: (bi, 0, i, 0))

    def shape(n):
        return jax.ShapeDtypeStruct((b, n, s, HEAD_DIM), BF16)

    def fm(n):
        return pl.BlockSpec((1, n, VT_ROWS, tm), lambda bi, i: (bi, 0, 0, i))

    def fshape(n):
        return jax.ShapeDtypeStruct((b, n, VT_ROWS, s), BF16)

    return pl.pallas_call(
        _even_proj_kernel,
        grid=(b, s // tm),
        in_specs=[
            pl.BlockSpec((1, tm, d), lambda bi, i: (bi, i, 0)),
            pl.BlockSpec((1, d), lambda bi, i: (0, 0)),
            vec, vec,
            pl.BlockSpec((d, EVEN_EXT), lambda bi, i: (0, 0)),
            pl.BlockSpec((A_KV + B_KV, d), lambda bi, i: (0, 0)),
            tab, tab, hv, hv,
        ],
        out_specs=[hm(A_HEADS), hm(A_KV_HEADS), fm(A_KV_HEADS), hm(B_HEADS), hm(B_KV_HEADS), fm(B_KV_HEADS)],
        out_shape=[shape(A_HEADS), shape(A_KV_HEADS), fshape(A_KV_HEADS),
                   shape(B_HEADS), shape(B_KV_HEADS), fshape(B_KV_HEADS)],
        compiler_params=_cparams("parallel", "parallel"),
        name="even_proj",
    )(x, gain, shift, scale, w_ext, w_vt, cos2, sin2, q_gain, k_gain)


def _odd_proj_kernel(x_ref, gain_ref, sh_ref, sc_ref, w_ref, q_ref, k_ref, v_ref, zd_ref):
    h = _norm_mod(x_ref[0], gain_ref[...], sh_ref[0], sc_ref[0])
    z = jnp.dot(h.astype(BF16), w_ref[...], preferred_element_type=F32)
    for i in range(C_HEADS):
        lo = i * HEAD_DIM
        q_ref[0, i] = (z[:, lo:lo + HEAD_DIM] * SCALE).astype(BF16)
        k_ref[0, i] = z[:, C_W + lo:C_W + lo + HEAD_DIM].astype(BF16)
        v_ref[0, i] = z[:, 2 * C_W + lo:2 * C_W + lo + HEAD_DIM].astype(BF16)
    zd_ref[0] = z[:, 3 * C_W:]


def odd_proj(x, gain, shift, scale, w):
    b, s, d = x.shape
    n = w.shape[1]
    tm = _pick_tile(s, 512)
    vec = pl.BlockSpec((1, 1, d), lambda bi, i: (bi, 0, 0))
    hm = pl.BlockSpec((1, C_HEADS, tm, HEAD_DIM), lambda bi, i: (bi, 0, i, 0))
    hshape = jax.ShapeDtypeStruct((b, C_HEADS, s, HEAD_DIM), BF16)
    return pl.pallas_call(
        _odd_proj_kernel,
        grid=(b, s // tm),
        in_specs=[
            pl.BlockSpec((1, tm, d), lambda bi, i: (bi, i, 0)),
            pl.BlockSpec((1, d), lambda bi, i: (0, 0)),
            vec, vec,
            pl.BlockSpec((d, n), lambda bi, i: (0, 0)),
        ],
        out_specs=[hm, hm, hm, pl.BlockSpec((1, tm, D_SHIFT_W), lambda bi, i: (bi, i, 0))],
        out_shape=[hshape, hshape, hshape, jax.ShapeDtypeStruct((b, s, D_SHIFT_W), F32)],
        compiler_params=_cparams("parallel", "parallel"),
        name="odd_proj",
    )(x, gain, shift, scale, w)


def _out_proj_kernel(x_ref, g_ref, y1_ref, y2_ref, w1_ref, w2_ref, o_ref, buf1_ref, buf2_ref, *, y2_heads):
    def gather_heads(y_ref, buf_ref):
        for h in range(y_ref.shape[1]):
            buf_ref[:, h * HEAD_DIM:(h + 1) * HEAD_DIM] = y_ref[0, h]
        return buf_ref[...]

    acc = jnp.dot(gather_heads(y1_ref, buf1_ref), w1_ref[...], preferred_element_type=F32)
    y2 = gather_heads(y2_ref, buf2_ref) if y2_heads else y2_ref[0]
    acc = acc + jnp.dot(y2, w2_ref[...], preferred_element_type=F32)
    o_ref[0] = x_ref[0] + g_ref[0] * acc


def out_proj(x, gate, y1, y2, w):
    b, s, d = x.shape
    nh = y1.shape[1]
    half = nh * HEAD_DIM
    tm = _pick_tile(s, 1024)
    y2_heads = y2.ndim == 4
    hm = pl.BlockSpec((1, nh, tm, HEAD_DIM), lambda bi, i: (bi, 0, i, 0))
    wspec = pl.BlockSpec((half, d), lambda bi, i: (0, 0))
    y2_spec = hm if y2_heads else pl.BlockSpec((1, tm, half), lambda bi, i: (bi, i, 0))
    return pl.pallas_call(
        functools.partial(_out_proj_kernel, y2_heads=y2_heads),
        grid=(b, s // tm),
        in_specs=[
            pl.BlockSpec((1, tm, d), lambda bi, i: (bi, i, 0)),
            pl.BlockSpec((1, 1, d), lambda bi, i: (bi, 0, 0)),
            hm, y2_spec, wspec, wspec,
        ],
        out_specs=pl.BlockSpec((1, tm, d), lambda bi, i: (bi, i, 0)),
        out_shape=jax.ShapeDtypeStruct((b, s, d), F32),
        scratch_shapes=[pltpu.VMEM((tm, half), BF16), pltpu.VMEM((tm, half), BF16)],
        compiler_params=_cparams("parallel", "parallel"),
        name="out_proj",
    )(x, gate, y1, y2, w[:half], w[half:])


def _ffn_kernel(x_ref, gain_ref, sh_ref, sc_ref, g_ref, wg_ref, wu_ref, wo_ref, o_ref, h_ref, acc_ref):
    j = pl.program_id(2)

    @pl.when(j == 0)
    def _():
        h_ref[...] = _norm_mod(x_ref[0], gain_ref[...], sh_ref[0], sc_ref[0]).astype(BF16)
        acc_ref[...] = jnp.zeros_like(acc_ref)

    h = h_ref[...]
    gate = jnp.dot(h, wg_ref[...], preferred_element_type=F32)
    up = jnp.dot(h, wu_ref[...], preferred_element_type=F32)
    act = gate * (1.0 / (1.0 + jnp.exp(-gate))) * up
    acc_ref[...] += jnp.dot(act.astype(BF16), wo_ref[...], preferred_element_type=F32)

    @pl.when(j == pl.num_programs(2) - 1)
    def _():
        o_ref[0] = x_ref[0] + g_ref[0] * acc_ref[...]


def ffn(x, gain, shift, scale, gate, w_in, w_out):
    b, s, d = x.shape
    f = w_out.shape[0]
    tm = _pick_tile(s, 512)
    th = 1408 if f % 1408 == 0 else _pick_tile(f, 256)
    nj = f // th
    vec = pl.BlockSpec((1, 1, d), lambda bi, i, j: (bi, 0, 0))
    return pl.pallas_call(
        _ffn_kernel,
        grid=(b, s // tm, nj),
        in_specs=[
            pl.BlockSpec((1, tm, d), lambda bi, i, j: (bi, i, 0)),
            pl.BlockSpec((1, d), lambda bi, i, j: (0, 0)),
            vec, vec, vec,
            pl.BlockSpec((d, th), lambda bi, i, j: (0, j)),
            pl.BlockSpec((d, th), lambda bi, i, j: (0, j + nj)),
            pl.BlockSpec((th, d), lambda bi, i, j: (j, 0)),
        ],
        out_specs=pl.BlockSpec((1, tm, d), lambda bi, i, j: (bi, i, 0)),
        out_shape=jax.ShapeDtypeStruct((b, s, d), F32),
        scratch_shapes=[pltpu.VMEM((tm, d), BF16), pltpu.VMEM((tm, d), F32)],
        compiler_params=_cparams("parallel", "parallel", "arbitrary"),
        name="ffn",
    )(x, gain, shift, scale, gate, w_in, w_in, w_out)


def _rms_kernel(x_ref, gain_ref, o_ref):
    x = x_ref[0]
    ms = jnp.mean(x * x, axis=-1, keepdims=True)
    o_ref[0] = x * lax.rsqrt(ms + RMS_EPS) * gain_ref[...]


def final_norm(x, gain):
    b, s, d = x.shape
    tm = _pick_tile(s, 1024)
    return pl.pallas_call(
        _rms_kernel,
        grid=(b, s // tm),
        in_specs=[pl.BlockSpec((1, tm, d), lambda bi, i: (bi, i, 0)),
                  pl.BlockSpec((1, d), lambda bi, i: (0, 0))],
        out_specs=pl.BlockSpec((1, tm, d), lambda bi, i: (bi, i, 0)),
        out_shape=jax.ShapeDtypeStruct((b, s, d), F32),
        compiler_params=_cparams("parallel", "parallel"),
        name="final_norm",
    )(x, gain)


def _flash_kernel(q_ref, k_ref, vt_ref, kx_ref, vxt_ref, o_ref, m_ref, acc_ref):
    kv = pl.program_id(3)
    g, tq, hd = q_ref.shape[1:]

    @pl.when(kv == 0)
    def _():
        m_ref[...] = jnp.full_like(m_ref, -jnp.inf)
        acc_ref[...] = jnp.zeros_like(acc_ref)

    q = q_ref[0].reshape(g * tq, hd)

    def scores(k):
        return _nt_dot(k, q)

    def accumulate(s, vt):
        m_prev = m_ref[...]
        m_new = jnp.maximum(m_prev, jnp.max(s, axis=0, keepdims=True))
        p = jnp.exp2(s - m_new).astype(BF16)
        acc_ref[...] = jnp.exp2(m_prev - m_new) * acc_ref[...] + jnp.dot(vt, p, preferred_element_type=F32)
        m_ref[...] = m_new

    tk = k_ref.shape[2]
    sub = min(tk, FLASH_SUB_KEYS)
    nsub = tk // sub
    s_next = scores(k_ref[0, 0, 0:sub, :])
    for c in range(nsub):
        s_cur = s_next
        if c + 1 < nsub:
            s_next = scores(k_ref[0, 0, (c + 1) * sub:(c + 2) * sub, :])
        accumulate(s_cur, vt_ref[0, 0, :, c * sub:(c + 1) * sub])

    @pl.when(kv == pl.num_programs(3) - 1)
    def _():
        accumulate(scores(kx_ref[0, 0]), vxt_ref[0, 0])
        acc = acc_ref[...]
        ot = (acc[:hd] / acc[hd:hd + 1]).astype(BF16)
        _transpose_heads(ot, o_ref.at[0], g, tq)


def global_attention(q, k, vt, kx, vxt):
    b, h, s, hd = q.shape
    hkv = k.shape[1]
    n_ctx = kx.shape[2]
    vr = vt.shape[2]
    g = h // hkv
    tq = _pick_tile(s, 1024)
    tk = _pick_tile(s, 4096)
    q_spec = pl.BlockSpec((1, g, tq, hd), lambda bi, gi, i, j: (bi, gi, i, 0))
    return pl.pallas_call(
        _flash_kernel,
        grid=(b, hkv, s // tq, s // tk),
        in_specs=[
            q_spec,
            pl.BlockSpec((1, 1, tk, hd), lambda bi, gi, i, j: (bi, gi, j, 0)),
            pl.BlockSpec((1, 1, vr, tk), lambda bi, gi, i, j: (bi, gi, 0, j)),
            pl.BlockSpec((1, 1, n_ctx, hd), lambda bi, gi, i, j: (bi, gi, 0, 0)),
            pl.BlockSpec((1, 1, vr, n_ctx), lambda bi, gi, i, j: (bi, gi, 0, 0)),
        ],
        out_specs=q_spec,
        out_shape=jax.ShapeDtypeStruct((b, h, s, hd), BF16),
        scratch_shapes=[pltpu.VMEM((1, g * tq), F32), pltpu.VMEM((vr, g * tq), F32)],
        compiler_params=_cparams("parallel", "parallel", "parallel", "arbitrary"),
        name="global_attention",
    )(q, k, vt, kx, vxt)


def _window_kernel(sink_ref, band_ref, q_ref, kp_ref, kc_ref, kn_ref, vp_ref, vc_ref, vn_ref, kx_ref, vxt_ref,
                   o_ref, *, seq):
    gi = pl.program_id(1)
    i = pl.program_id(2)
    g, tq, hd = q_ref.shape[1:]
    w = kp_ref.shape[2]
    nq = g * tq
    q = q_ref[0].reshape(nq, hd)
    k = jnp.concatenate([kp_ref[0, 0], kc_ref[0, 0], kn_ref[0, 0]], axis=0)
    vt = jnp.concatenate([vp_ref[0, 0], vc_ref[0, 0], vn_ref[0, 0]], axis=1)
    span = tq + 2 * w
    kpos = i * tq - w + lax.broadcasted_iota(jnp.int32, (span, 1), 0)
    outside = jnp.where((kpos >= 0) & (kpos < seq), 0.0, NEG_INF)
    s_loc = _nt_dot(k, q) + band_ref[...] + outside
    s_ctx = _nt_dot(kx_ref[0, 0], q)
    head = lax.broadcasted_iota(jnp.int32, (1, nq), 1) // tq
    sink = jnp.zeros((1, nq), F32)
    for hh in range(g):
        sink = jnp.where(head == hh, sink_ref[gi * g + hh], sink)
    m = jnp.maximum(jnp.maximum(jnp.max(s_loc, axis=0, keepdims=True),
                                jnp.max(s_ctx, axis=0, keepdims=True)), sink)
    acc = jnp.dot(vt, jnp.exp(s_loc - m).astype(BF16), preferred_element_type=F32)
    acc = acc + jnp.dot(vxt_ref[0, 0], jnp.exp(s_ctx - m).astype(BF16), preferred_element_type=F32)
    l = acc[hd:hd + 1] + jnp.exp(sink - m)
    _transpose_heads((acc[:hd] / l).astype(BF16), o_ref.at[0], g, tq)


def window_attention(q, k, vt, kx, vxt, sink):
    b, h, s, hd = q.shape
    hkv = k.shape[1]
    vr = vt.shape[2]
    g = h // hkv
    n_ctx = kx.shape[2]
    tq = _pick_tile(s, 256)
    r = tq // WINDOW
    nwb = s // WINDOW

    def before(i):
        return jnp.maximum(i * r - 1, 0)

    def after(i):
        return jnp.minimum((i + 1) * r, nwb - 1)

    span = tq + 2 * WINDOW
    krow = np.arange(span)[:, None] - WINDOW
    qcol = np.arange(g * tq)[None, :] % tq
    band = jnp.asarray(np.where(np.abs(krow - qcol) <= WINDOW, 0.0, NEG_INF), F32)
    return pl.pallas_call(
        functools.partial(_window_kernel, seq=s),
        grid=(b, hkv, s // tq),
        in_specs=[
            pl.BlockSpec(memory_space=pltpu.SMEM),
            pl.BlockSpec((span, g * tq), lambda bi, gi, i: (0, 0)),
            pl.BlockSpec((1, g, tq, hd), lambda bi, gi, i: (bi, gi, i, 0)),
            pl.BlockSpec((1, 1, WINDOW, hd), lambda bi, gi, i: (bi, gi, before(i), 0)),
            pl.BlockSpec((1, 1, tq, hd), lambda bi, gi, i: (bi, gi, i, 0)),
            pl.BlockSpec((1, 1, WINDOW, hd), lambda bi, gi, i: (bi, gi, after(i), 0)),
            pl.BlockSpec((1, 1, vr, WINDOW), lambda bi, gi, i: (bi, gi, 0, before(i))),
            pl.BlockSpec((1, 1, vr, tq), lambda bi, gi, i: (bi, gi, 0, i)),
            pl.BlockSpec((1, 1, vr, WINDOW), lambda bi, gi, i: (bi, gi, 0, after(i))),
            pl.BlockSpec((1, 1, n_ctx, hd), lambda bi, gi, i: (bi, gi, 0, 0)),
            pl.BlockSpec((1, 1, vr, n_ctx), lambda bi, gi, i: (bi, gi, 0, 0)),
        ],
        out_specs=pl.BlockSpec((1, g, tq, hd), lambda bi, gi, i: (bi, gi, i, 0)),
        out_shape=jax.ShapeDtypeStruct((b, h, s, hd), BF16),
        compiler_params=_cparams("parallel", "parallel", "parallel"),
        name="window_attention",
    )(sink, band, q, k, k, k, vt, vt, vt, kx, vxt)


def _ctx_attn_kernel(sink_ref, q_ref, k_ref, v_ref, o_ref, *, base2, feature_major):
    hi = pl.program_id(1)
    s = _nt_dot(q_ref[0, 0], k_ref[0, 0])
    sink = sink_ref[hi]
    m = jnp.maximum(jnp.max(s, axis=-1, keepdims=True), sink)
    ex = jnp.exp2 if base2 else jnp.exp
    p = ex(s - m)
    l = jnp.sum(p, axis=-1, keepdims=True) + ex(sink - m)
    if feature_major:
        o = _nt_dot(p.astype(BF16), v_ref[0, 0, :HEAD_DIM, :])
    else:
        o = jnp.dot(p.astype(BF16), v_ref[0, 0], preferred_element_type=F32)
    o_ref[0, 0] = (o / l).astype(o_ref.dtype)


def ctx_attention(q, k, v, sink, base2=False, feature_major=False):
    b, h, n, hd = q.shape
    g = h // k.shape[1]
    kv = pl.BlockSpec((1, 1, n, hd), lambda bi, hi: (bi, hi // g, 0, 0))
    vs = pl.BlockSpec((1, 1, v.shape[2], n), lambda bi, hi: (bi, hi // g, 0, 0)) if feature_major else kv
    qs = pl.BlockSpec((1, 1, n, hd), lambda bi, hi: (bi, hi, 0, 0))
    return pl.pallas_call(
        functools.partial(_ctx_attn_kernel, base2=base2, feature_major=feature_major),
        grid=(b, h),
        in_specs=[pl.BlockSpec(memory_space=pltpu.SMEM), qs, kv, vs],
        out_specs=qs,
        out_shape=jax.ShapeDtypeStruct((b, h, n, hd), BF16),
        compiler_params=_cparams("parallel", "parallel"),
        name="ctx_attention",
    )(sink, q, k, v)


def _na_kernel(q_ref, k_ref, v_ref, kx_ref, vx_ref, bias_ref, o_ref, *, rows):
    t = pl.program_id(2)
    nkeys = NA_KEY_ROWS * GRID_W
    kstart = jnp.clip(t * NA_TILE_ROWS - NA_ROWS // 2, 0, rows - NA_KEY_ROWS)
    off = pl.multiple_of(kstart * GRID_W, GRID_W)
    q = q_ref[0, 0]
    tq = q.shape[0]
    kw = k_ref[0, 0, pl.ds(off, nkeys), :]
    vw = v_ref[0, 0, pl.ds(off, nkeys), :]
    s_loc = _nt_dot(kw, q) + bias_ref[0, 0]
    s_ctx = _nt_dot(kx_ref[0, 0], q)
    m = jnp.maximum(jnp.max(s_loc, axis=0, keepdims=True), jnp.max(s_ctx, axis=0, keepdims=True))
    p_loc = jnp.exp(s_loc - m)
    p_ctx = jnp.exp(s_ctx - m)
    l = jnp.sum(p_loc, axis=0, keepdims=True) + jnp.sum(p_ctx, axis=0, keepdims=True)
    ot = _tn_dot(vw, p_loc.astype(BF16)) + _tn_dot(vx_ref[0, 0], p_ctx.astype(BF16))
    _transpose_heads((ot / l).astype(BF16), o_ref.at[0], 1, tq)


def na_bias_tables(rpb, rows):
    nh = rpb.shape[0]
    i = np.arange(NA_TILE_ROWS)
    j = np.arange(NA_KEY_ROWS)
    qc = np.arange(GRID_W)
    kc = np.arange(GRID_W)
    cs = np.clip(qc - NA_COLS // 2, 0, GRID_W - NA_COLS)
    col_valid = (kc[:, None] >= cs[None, :]) & (kc[:, None] < cs[None, :] + NA_COLS)
    dcol = np.clip(kc[:, None] - qc[None, :] + NA_COLS - 1, 0, 2 * NA_COLS - 2)
    per_drow = jnp.where(col_valid[None, None], rpb.astype(F32)[:, :, dcol], NEG_INF)
    masked = jnp.full((nh, 1, GRID_W, GRID_W), NEG_INF, F32)
    per_drow = jnp.concatenate([per_drow, masked], axis=1)
    out = []
    for r0 in (0, NA_TILE_ROWS, rows - NA_TILE_ROWS):
        r = r0 + i
        rs = np.clip(r - NA_ROWS // 2, 0, rows - NA_ROWS)
        kstart = int(np.clip(r0 - NA_ROWS // 2, 0, rows - NA_KEY_ROWS))
        krow = kstart + j
        row_valid = (krow[:, None] >= rs[None, :]) & (krow[:, None] < rs[None, :] + NA_ROWS)
        drow = np.where(row_valid, krow[:, None] - r[None, :] + NA_ROWS - 1, 2 * NA_ROWS - 1)
        cols = [per_drow[:, drow[:, ii]] for ii in range(NA_TILE_ROWS)]
        dense = jnp.stack(cols, axis=3)
        out.append(dense.reshape(nh, NA_KEY_ROWS * GRID_W, NA_TILE_ROWS * GRID_W))
    return jnp.stack(out)


def neighborhood_attention(q, k, v, kx, vx, bias):
    b, h, s, hd = q.shape
    rows = s // GRID_W
    nt = rows // NA_TILE_ROWS
    tq = NA_TILE_ROWS * GRID_W
    nkeys = NA_KEY_ROWS * GRID_W
    n_ctx = kx.shape[2]
    full = pl.BlockSpec((1, 1, s, hd), lambda bi, hi, t: (bi, hi, 0, 0))
    cx = pl.BlockSpec((1, 1, n_ctx, hd), lambda bi, hi, t: (bi, hi, 0, 0))

    def case(t):
        return jnp.where(t == 0, 0, jnp.where(t == nt - 1, 2, 1))

    return pl.pallas_call(
        functools.partial(_na_kernel, rows=rows),
        grid=(b, h, nt),
        in_specs=[
            pl.BlockSpec((1, 1, tq, hd), lambda bi, hi, t: (bi, hi, t, 0)),
            full, full, cx, cx,
            pl.BlockSpec((1, 1, nkeys, tq), lambda bi, hi, t: (case(t), hi, 0, 0)),
        ],
        out_specs=pl.BlockSpec((1, 1, tq, hd), lambda bi, hi, t: (bi, hi, t, 0)),
        out_shape=jax.ShapeDtypeStruct((b, h, s, hd), BF16),
        compiler_params=_cparams("parallel", "parallel", "arbitrary"),
        name="neighborhood_attention",
    )(q, k, v, kx, vx, bias)


def _mm(a, b):
    return jnp.dot(a.astype(BF16), b.astype(BF16), preferred_element_type=F32)


def _tri_cumsum(tri, x):
    t16 = tri.astype(BF16)
    hi = x.astype(BF16)
    r1 = x - hi.astype(F32)
    mid = r1.astype(BF16)
    lo = (r1 - mid.astype(F32)).astype(BF16)
    out = jnp.dot(t16, hi, preferred_element_type=F32)
    out = out + jnp.dot(t16, mid, preferred_element_type=F32)
    return out + jnp.dot(t16, lo, preferred_element_type=F32)


def _mm_nt(a, b):
    return lax.dot_general(a.astype(BF16), b.astype(BF16), (((1,), (1,)), ((), ())),
                           preferred_element_type=F32)


def _mm_tn(a, b):
    return lax.dot_general(a.astype(BF16), b.astype(BF16), (((0,), (0,)), ((), ())),
                           preferred_element_type=F32)


def _rwkv_chains(chains):
    t, w = chains[0][0].shape
    nh = w // HEAD_DIM
    n = range(len(chains))
    lw, kd, a, r, v, kap, st, rev = (list(col) for col in zip(*chains))
    row = lax.broadcasted_iota(jnp.int32, (t, w), 0)
    colj = lax.broadcasted_iota(jnp.int32, (t, w), 1) % t
    lane_head = lax.broadcasted_iota(jnp.int32, (1, w), 1) // HEAD_DIM
    strict_d = (colj < row, colj > row)
    incl_d = (colj <= row, colj >= row)
    tri_d = tuple(m[:, :t].astype(F32) for m in incl_d)
    strict = [strict_d[int(rev[i])] for i in n]
    incl = [incl_d[int(rev[i])] for i in n]

    def bd(y):
        return jnp.concatenate([jnp.where(lane_head == h, y, 0.0) for h in range(nh)], axis=0)

    def bdmm(xs, ys):
        return [_mm(xs[i], bd(ys[i])) for i in n]

    c_incl = [_tri_cumsum(tri_d[int(rev[i])], lw[i]) for i in n]
    c_excl = [c_incl[i] - lw[i] for i in n]
    c_mid = [c_incl[i][t // 2:t // 2 + 1, :] for i in n]
    c_end = [c_incl[i][0:1, :] if rev[i] else c_incl[i][t - 1:t, :] for i in n]
    bvec = [kap[i] * a[i] for i in n]
    e_neg = [jnp.exp(c_mid[i] - c_incl[i]) for i in n]
    left = [jnp.concatenate([kap[i] * jnp.exp(c_excl[i] - c_mid[i]), r[i] * jnp.exp(c_incl[i] - c_mid[i])],
                            axis=0) for i in n]
    wt = [jnp.concatenate([bd(bvec[i] * e_neg[i]), bd(kd[i] * e_neg[i])], axis=0) for i in n]
    m = [_mm_nt(left[i], wt[i]) for i in n]
    l_b = [jnp.where(strict[i], m[i][:t, :w], 0.0) for i in n]
    l_k = [jnp.where(strict[i], m[i][:t, w:], 0.0) for i in n]
    m_br = [jnp.where(incl[i], m[i][t:, :w], 0.0) for i in n]
    m_kr = [jnp.where(incl[i], m[i][t:, w:], 0.0) for i in n]

    blk = RWKV_INV_BLOCK
    same = row // blk == colj // blk
    eye = jnp.where(row == colj, 1.0, 0.0)
    pw = [-jnp.where(same, l_b[i], 0.0) for i in n]
    inv = [eye + pw[i] for i in n]
    span = 2
    while span < blk:
        pw = bdmm(pw, pw)
        step = bdmm(inv, pw)
        inv = [inv[i] + step[i] for i in n]
        span *= 2
    while blk < t:
        pair = (row // (2 * blk) == colj // (2 * blk)) & (row // blk != colj // blk)
        off = [jnp.where(pair, l_b[i], 0.0) for i in n]
        step = bdmm(bdmm(inv, off), inv)
        inv = [inv[i] - step[i] for i in n]
        blk *= 2

    kt = [kap[i] * jnp.exp(c_excl[i]) for i in n]
    rt = [r[i] * jnp.exp(c_incl[i]) for i in n]
    both = bdmm([jnp.concatenate([l_k[i], m_kr[i]], axis=0) for i in n], v)
    lkv = [both[i][:t] for i in n]
    mkv = [both[i][t:] for i in n]
    wk = bdmm(inv, kt)
    u0 = bdmm(inv, lkv)
    mbw = bdmm(m_br, wk)
    mbu = bdmm(m_br, u0)
    rq = [rt[i] - mbw[i] for i in n]
    y0 = [mkv[i] - mbu[i] for i in n]
    e_end = [jnp.exp(c_end[i] - c_incl[i]) for i in n]
    b_hat = [bvec[i] * e_end[i] for i in n]
    k_hat = [kd[i] * e_end[i] for i in n]
    ri = lax.broadcasted_iota(jnp.int32, (w, w), 0)
    ci = lax.broadcasted_iota(jnp.int32, (w, w), 1)
    same_head = ri // HEAD_DIM == ci // HEAD_DIM
    diag = ri == ci
    bwu = [_mm_tn(b_hat[i], jnp.concatenate([wk[i], u0[i]], axis=1)) for i in n]
    bw = [bwu[i][:, :w] for i in n]
    bu = [bwu[i][:, w:] for i in n]
    kv = [_mm_tn(k_hat[i], v[i]) for i in n]
    gt = [jnp.where(diag, jnp.exp(c_end[i]), 0.0) - jnp.where(same_head, bw[i], 0.0) for i in n]
    c0t = [jnp.where(same_head, kv[i] - bu[i], 0.0) for i in n]
    ys = [_mm(rq[i], st[i]) + y0[i] for i in n]
    sts = [_mm(gt[i], st[i]) + c0t[i] for i in n]
    return ys, sts


def _rwkv_kernel(lwf_ref, kdf_ref, af_ref, lwb_ref, kdb_ref, ab_ref, rf_ref, vf_ref, kkf_ref,
                 rb_ref, vb_ref, kkb_ref, sin_ref, yf_ref, yb_ref, sout_ref, st_ref):
    c = pl.program_id(0)

    @pl.when(c == 0)
    def _():
        st_ref[...] = sin_ref[...]

    bsz = lwf_ref.shape[0]
    groups = lwf_ref.shape[2] // RWKV_LANES
    chains, slots = [], []
    for bi in range(bsz):
        for d, (lw_ref, kd_ref, a_ref, r_ref, v_ref, kk_ref, y_ref) in enumerate((
                (lwf_ref, kdf_ref, af_ref, rf_ref, vf_ref, kkf_ref, yf_ref),
                (lwb_ref, kdb_ref, ab_ref, rb_ref, vb_ref, kkb_ref, yb_ref))):
            for hg in range(groups):
                lanes = slice(hg * RWKV_LANES, (hg + 1) * RWKV_LANES)
                chain = d * groups + hg
                chains.append((lw_ref[bi, :, lanes], kd_ref[bi, :, lanes], a_ref[bi, :, lanes],
                               r_ref[bi, :, lanes], v_ref[bi, :, lanes], kk_ref[bi, :, lanes],
                               st_ref[bi, chain], bool(d)))
                slots.append((y_ref, bi, lanes, chain))
    ys, sts = _rwkv_chains(chains)
    for (y_ref, bi, lanes, chain), y, st in zip(slots, ys, sts):
        y_ref[bi, :, lanes] = y
        st_ref[bi, chain] = st

    @pl.when(c == pl.num_programs(0) - 1)
    def _():
        sout_ref[...] = st_ref[...]


def rwkv_scan(lw, kd, a, r, v, kk, state):
    b, length, w = r.shape
    t = RWKV_CHUNK
    nc = length // t
    fwd = pl.BlockSpec((b, t, w), lambda c: (0, c, 0))
    bwd = pl.BlockSpec((b, t, w), lambda c: (0, nc - 1 - c, 0))
    st_spec = pl.BlockSpec(state.shape, lambda c: (0, 0, 0, 0))
    y_shape = jax.ShapeDtypeStruct((b, length, w), F32)
    return pl.pallas_call(
        _rwkv_kernel,
        grid=(nc,),
        in_specs=[fwd, fwd, fwd, bwd, bwd, bwd, fwd, fwd, fwd, bwd, bwd, bwd, st_spec],
        out_specs=[fwd, bwd, st_spec],
        out_shape=[y_shape, y_shape, jax.ShapeDtypeStruct(state.shape, F32)],
        scratch_shapes=[pltpu.VMEM(state.shape, F32)],
        compiler_params=_cparams("arbitrary"),
        name="rwkv_scan",
    )(lw[0], kd[0], a[0], lw[1], kd[1], a[1], r, v, kk, r, v, kk, state)


def _exact_dot(a, b):
    return jnp.dot(a, b, preferred_element_type=F32, precision=lax.Precision.HIGHEST)


def _head_sum_matrix(width, value):
    r = lax.broadcasted_iota(jnp.int32, (width, width), 0) // HEAD_DIM
    c = lax.broadcasted_iota(jnp.int32, (width, width), 1) // HEAD_DIM
    return jnp.where(r == c, value, 0.0).astype(BF16)


def _head_sum(x, mat):
    hi = x.astype(BF16)
    lo = (x - hi.astype(F32)).astype(BF16)
    return jnp.dot(hi, mat, preferred_element_type=F32) + jnp.dot(lo, mat, preferred_element_type=F32)


def _sigmoid(x):
    return 1.0 / (1.0 + jnp.exp(-x))


def _rwkv_prep_kernel(z_ref, zp_ref, zn_ref, mu_ref, kk_ref, ka_ref, w0_ref, w2_ref, a0_ref, a2_ref, g2_ref,
                      r_ref, v_ref, kn_ref, g_ref, lw0_ref, kd0_ref, av0_ref, lw1_ref, kd1_ref, av1_ref):
    i = pl.program_id(1)
    z = z_ref[0]
    tm = z.shape[0]
    halo = zp_ref.shape[1]
    row = lax.broadcasted_iota(jnp.int32, (tm, 1), 0)
    before = jnp.where(i > 0, zp_ref[0, halo - 1:halo, :], 0.0)
    after = jnp.where(i < pl.num_programs(1) - 1, zn_ref[0, 0:1, :], 0.0)
    z_prev = jnp.where(row == 0, before, pltpu.roll(z, 1, 0))
    z_next = jnp.where(row == tm - 1, after, pltpu.roll(z, tm - 1, 0))
    zs = z + (0.5 * (z_prev + z_next) - z) * mu_ref[...]

    r = zs[:, :D_W]
    k = zs[:, D_W:2 * D_W]
    v = zs[:, 2 * D_W:3 * D_W]
    o = 3 * D_W
    wd = zs[:, o:o + DECAY_LORA]
    ad = zs[:, o + DECAY_LORA:o + DECAY_LORA + ICLR_LORA]
    gd = zs[:, o + DECAY_LORA + ICLR_LORA:]
    r_ref[0] = r
    v_ref[0] = v
    g_ref[0] = _exact_dot(_sigmoid(gd), g2_ref[...])
    kraw = k * kk_ref[...]
    ss = _head_sum(kraw * kraw, _head_sum_matrix(D_W, 1.0))
    kn_ref[0] = kraw / jnp.maximum(jnp.sqrt(ss), 1e-12)
    twd = jnp.tanh(wd)
    for d, (lw_ref, kd_ref, av_ref) in enumerate(((lw0_ref, kd0_ref, av0_ref), (lw1_ref, kd1_ref, av1_ref))):
        x = -(w0_ref[d:d + 1, :] + _exact_dot(twd, w2_ref[d]))
        softplus = jnp.maximum(x, 0.0) + jnp.log(1.0 + jnp.exp(-jnp.abs(x)))
        lw_ref[0] = -jnp.exp(-softplus - 0.5)
        a = _sigmoid(a0_ref[d:d + 1, :] + _exact_dot(ad, a2_ref[d]))
        av_ref[0] = a
        kd_ref[0] = k * (1.0 + (a - 1.0) * ka_ref[...])


def rwkv_prep(zd, mu, k_k, k_a, w0, w2, a0, a2, g2):
    b, length, width = zd.shape
    tm = _pick_tile(length, 512)
    halo = 8
    nb = length // halo
    per = tm // halo
    row = pl.BlockSpec((1, tm, D_W), lambda bi, i: (bi, i, 0))
    shape = jax.ShapeDtypeStruct((b, length, D_W), F32)

    def full(a):
        return pl.BlockSpec(a.shape, lambda bi, i: (0,) * a.ndim)

    params = (mu, k_k, k_a, w0, w2, a0, a2, g2)
    return pl.pallas_call(
        _rwkv_prep_kernel,
        grid=(b, length // tm),
        in_specs=[
            pl.BlockSpec((1, tm, width), lambda bi, i: (bi, i, 0)),
            pl.BlockSpec((1, halo, width), lambda bi, i: (bi, jnp.maximum(i * per - 1, 0), 0)),
            pl.BlockSpec((1, halo, width), lambda bi, i: (bi, jnp.minimum((i + 1) * per, nb - 1), 0)),
        ] + [full(p) for p in params],
        out_specs=[row] * 10,
        out_shape=[shape] * 10,
        compiler_params=_cparams("parallel", "parallel"),
        name="rwkv_prep",
    )(zd, zd, zd, *params)


def _rwkv_readout_kernel(yf_ref, yb_ref, r_ref, v_ref, kd0_ref, kd1_ref, g_ref, rk_ref, lnw_ref, lnb_ref, o_ref):
    y = yf_ref[0] + yb_ref[0]
    avg = _head_sum_matrix(D_W, 1.0 / HEAD_DIM)
    mean = _head_sum(y, avg)
    yc = y - mean
    var = _head_sum(yc * yc, avg)
    yn = yc * lax.rsqrt(var + GN_EPS) * lnw_ref[...] + lnb_ref[...]
    bonus = _head_sum(r_ref[0] * (kd0_ref[0] + kd1_ref[0]) * rk_ref[...], _head_sum_matrix(D_W, 1.0))
    o_ref[0] = ((yn + bonus * v_ref[0]) * g_ref[0]).astype(o_ref.dtype)


def rwkv_readout(yf, yb, r, v, kd0, kd1, g, r_k, ln_w, ln_b):
    b, length, w = yf.shape
    tm = _pick_tile(length, 512)
    row = pl.BlockSpec((1, tm, w), lambda bi, i: (bi, i, 0))
    vec = pl.BlockSpec((1, w), lambda bi, i: (0, 0))
    return pl.pallas_call(
        _rwkv_readout_kernel,
        grid=(b, length // tm),
        in_specs=[row] * 7 + [vec] * 3,
        out_specs=row,
        out_shape=jax.ShapeDtypeStruct((b, length, w), BF16),
        compiler_params=_cparams("parallel", "parallel"),
        name="rwkv_readout",
    )(yf, yb, r, v, kd0, kd1, g, r_k, ln_w, ln_b)


def _rope_tables(n_tokens):
    t = jnp.arange(n_tokens, dtype=jnp.int32)
    row = (t // GRID_W).astype(F32)
    col = (t % GRID_W).astype(F32)
    inv = ROPE_THETA ** (-jnp.arange(ROPE_FREQS, dtype=F32) / ROPE_FREQS)
    ang = jnp.concatenate([row[:, None] * inv, col[:, None] * inv], axis=-1)
    cos, sin = jnp.cos(ang), jnp.sin(ang)
    return jnp.concatenate([cos, cos], axis=-1), jnp.concatenate([sin, sin], axis=-1)


def _rwkv_mixer(zd, zdc, params, r_k, ln_w, ln_b, need_ctx):
    lat = rwkv_prep(zd, *params)
    cx = rwkv_prep(zdc, *params)
    bsz = zd.shape[0]
    chains = 2 * (D_W // RWKV_LANES)
    zero = jnp.zeros((bsz, chains, RWKV_LANES, RWKV_LANES), F32)

    def scan(p, state):
        r, v, kk, _, lw0, kd0, a0, lw1, kd1, a1 = p
        return rwkv_scan((lw0, lw1), (kd0, kd1), (a0, a1), r, v, kk, state)

    def readout(p, yf, yb):
        r, v, _, g, _, kd0, _, _, kd1, _ = p
        return rwkv_readout(yf, yb, r, v, kd0, kd1, g, r_k, ln_w, ln_b)

    ycf, ycb, state = scan(cx, zero)
    yf, yb, _ = scan(lat, state)
    y_d = readout(lat, yf, yb)
    yc_d = readout(cx, ycf, ycb) if need_ctx else None
    return y_d, yc_d


def kernel(x, c, ctx, c_ctx, w_mod, b_mod, norm_mix, norm_ffn, w_in_even, w_out_even, q_norm_a, k_norm_a,
           sink_b, w_in_odd, w_out_odd, rpb_c, shift_mu, decay_w0, decay_w2, iclr_a0, iclr_a2, gate_g2,
           k_k, k_a, r_k, ln_x_w, ln_x_b, w_ffn_in, w_ffn_out, norm_out):
    bsz, s, d = x.shape
    n_ctx = ctx.shape[1]
    cos2, sin2 = _rope_tables(s)
    cos_ctx = jnp.ones((n_ctx, HEAD_DIM), F32)
    sin_ctx = jnp.zeros((n_ctx, HEAD_DIM), F32)
    act = jnp.zeros((8, d), F32).at[:bsz].set(jax.nn.silu(c)).at[bsz].set(jax.nn.silu(c_ctx))
    mods = modulation(act.astype(BF16), w_mod, b_mod)
    w_out_even = w_out_even.astype(BF16)
    w_in_odd = w_in_odd.astype(BF16)
    w_out_odd = w_out_odd.astype(BF16)
    w_ffn_in = w_ffn_in.astype(BF16)
    w_ffn_out = w_ffn_out.astype(BF16)
    no_sink = jnp.full((A_HEADS,), NEG_INF, F32)
    for layer in range(DEPTH):
        need_ctx = layer < DEPTH - 1
        mod = mods[layer, :bsz][:, None, :]
        modc = jnp.broadcast_to(mods[layer, bsz][None, None, :], (bsz, 1, 6 * d))
        sh1, sc1, g1, sh2, sc2, g2 = jnp.split(mod, 6, axis=-1)
        csh1, csc1, cg1, csh2, csc2, cg2 = jnp.split(modc, 6, axis=-1)
        gain_mix = norm_mix[layer][None, :]
        gain_ffn = norm_ffn[layer][None, :]
        i = layer // 2
        if layer % 2 == 0:
            w_out = w_out_even[i]
            qg, kg = q_norm_a[i][None, :], k_norm_a[i][None, :]
            w_ext = even_weight(w_in_even[i], q_norm_a[i], k_norm_a[i])
            w_vt = jnp.concatenate([w_in_even[i][:, _EV_VA:_EV_VA + A_KV],
                                    w_in_even[i][:, _EV_VB:_EV_VB + B_KV]], axis=1).T.astype(BF16)
            qa, ka, vat, qb, kb, vbt = even_proj(x, gain_mix, sh1, sc1, w_ext, w_vt, cos2, sin2, qg, kg)
            qca, kca, vcat, qcb, kcb, vcbt = even_proj(ctx, gain_mix, csh1, csc1, w_ext, w_vt,
                                                        cos_ctx, sin_ctx, qg, kg)
            y1 = global_attention(qa, ka, vat, kca, vcat)
            y2 = window_attention(qb, kb, vbt, kcb, vcbt, sink_b[i])
            if need_ctx:
                yc1 = ctx_attention(qca, kca, vcat, no_sink, base2=True, feature_major=True)
                yc2 = ctx_attention(qcb, kcb, vcbt, sink_b[i], feature_major=True)
        else:
            w_out = w_out_odd[i]
            q, k, v, zd = odd_proj(x, gain_mix, sh1, sc1, w_in_odd[i])
            qc, kc, vc, zdc = odd_proj(ctx, gain_mix, csh1, csc1, w_in_odd[i])
            bias = na_bias_tables(rpb_c[i], s // GRID_W)
            y1 = neighborhood_attention(q, k, v, kc, vc, bias)
            params = (shift_mu[i][None, :], k_k[i][None, :], k_a[i][None, :], decay_w0[i], decay_w2[i],
                      iclr_a0[i], iclr_a2[i], gate_g2[i])
            y2, yc2 = _rwkv_mixer(zd, zdc, params, r_k[i].reshape(1, D_W), ln_x_w[i][None, :],
                                  ln_x_b[i][None, :], need_ctx)
            if need_ctx:
                yc1 = ctx_attention(qc, kc, vc, no_sink)
        x = out_proj(x, g1, y1, y2, w_out)
        x = ffn(x, gain_ffn, sh2, sc2, g2, w_ffn_in[layer], w_ffn_out[layer])
        if need_ctx:
            ctx = out_proj(ctx, cg1, yc1, yc2, w_out)
            ctx = ffn(ctx, gain_ffn, csh2, csc2, cg2, w_ffn_in[layer], w_ffn_out[layer])
    return final_norm(x, norm_out[None, :])
```

```python
import functools

import jax
import jax.numpy as jnp
import numpy as np
from jax import lax
from jax.experimental import pallas as pl
from jax.experimental.pallas import tpu as pltpu

F32 = jnp.float32
BF16 = jnp.bfloat16

D_MODEL = 1024
DEPTH = 4
GRID_W = 64
HEAD_DIM = 64
SCALE = HEAD_DIM ** -0.5
ROPE_HALF = HEAD_DIM // 2
ROPE_FREQS = HEAD_DIM // 4
ROPE_THETA = 10000.0
A_HEADS = 8
A_KV_HEADS = 2
B_HEADS = 8
B_KV_HEADS = 2
WINDOW = 128
C_HEADS = 8
NA_ROWS = 8
NA_COLS = 16
D_HEADS = 8
DECAY_LORA = 64
ICLR_LORA = 64
GATE_LORA = 128
GN_EPS = 64e-5
RMS_EPS = 1e-6
NEG_INF = -1e30
FFN_HIDDEN = -(-8 * D_MODEL // (3 * 256)) * 256

A_Q = A_HEADS * HEAD_DIM
A_KV = A_KV_HEADS * HEAD_DIM
B_Q = B_HEADS * HEAD_DIM
B_KV = B_KV_HEADS * HEAD_DIM
C_W = C_HEADS * HEAD_DIM
D_W = D_HEADS * HEAD_DIM
EVEN_WIDTHS = (A_Q, A_KV, A_KV, B_Q, B_KV, B_KV)
D_SHIFT_W = 3 * D_W + DECAY_LORA + ICLR_LORA + GATE_LORA

VMEM_LIMIT_BYTES = 56 * 1024 * 1024
RWKV_CHUNK = HEAD_DIM
RWKV_INV_BLOCK = 16
RWKV_LANES = 256
FLASH_SUB_KEYS = 512
FLASH_TQ = 1024
FLASH_TK = 4096
LOG2E = 1.4426950408889634
VT_ROWS = HEAD_DIM + 16
NA_TILE_ROWS = 8
NA_HEADS_PER_STEP = 2
NA_KEY_ROWS = NA_TILE_ROWS + NA_ROWS


def _cparams(*sem):
    return pltpu.CompilerParams(dimension_semantics=sem, vmem_limit_bytes=VMEM_LIMIT_BYTES)


def _split(t, widths):
    return jnp.split(t, [int(o) for o in np.cumsum(widths)[:-1]], axis=-1)


def _pick_tile(n, target):
    t = min(n, target)
    while n % t:
        t //= 2
    return t


def _mod_kernel(a_ref, w_ref, b_ref, o_ref):
    w = w_ref[0].astype(BF16)
    o_ref[0] = jnp.dot(a_ref[...], w, preferred_element_type=F32) + b_ref[0]


def modulation(act, w_mod, b_mod):
    depth, d, n = w_mod.shape
    tn = 1536
    return pl.pallas_call(
        _mod_kernel,
        grid=(depth, n // tn),
        in_specs=[
            pl.BlockSpec((8, d), lambda l, j: (0, 0)),
            pl.BlockSpec((1, d, tn), lambda l, j: (l, 0, j)),
            pl.BlockSpec((1, 1, tn), lambda l, j: (l, 0, j)),
        ],
        out_specs=pl.BlockSpec((1, 8, tn), lambda l, j: (l, 0, j)),
        out_shape=jax.ShapeDtypeStruct((depth, 8, n), F32),
        compiler_params=_cparams("parallel", "parallel"),
        name="modulation",
    )(act, w_mod, b_mod.reshape(depth, 1, n))


def _norm_mod(x, gain, shift, scale):
    ms = jnp.mean(x * x, axis=-1, keepdims=True)
    h = x * lax.rsqrt(ms + RMS_EPS) * gain
    return h * (1.0 + scale) + shift


def _head_rms(z):
    return lax.rsqrt(jnp.mean(z * z, axis=-1, keepdims=True) + RMS_EPS)


def _nt_dot(a, b):
    return lax.dot_general(a, b, (((1,), (1,)), ((), ())), preferred_element_type=F32)


def _tn_dot(a, b):
    return lax.dot_general(a, b, (((0,), (0,)), ((), ())), preferred_element_type=F32)


def _transpose_heads(ot, o_ref, g, tq):
    tb = min(tq, 256)
    eye = (lax.broadcasted_iota(jnp.int32, (tb, tb), 0)
           == lax.broadcasted_iota(jnp.int32, (tb, tb), 1)).astype(BF16)
    for h in range(g):
        for j in range(tq // tb):
            lo = h * tq + j * tb
            o_ref[h, j * tb:(j + 1) * tb, :] = _nt_dot(eye, ot[:, lo:lo + tb]).astype(o_ref.dtype)


_EV_QA, _EV_KA, _EV_VA, _EV_QB, _EV_KB, _EV_VB = (int(o) for o in np.cumsum((0,) + EVEN_WIDTHS)[:-1])
_EV_QAR = sum(EVEN_WIDTHS)
_EV_KAR = _EV_QAR + A_Q
_EV_QBR = _EV_KAR + A_KV
_EV_KBR = _EV_QBR + B_Q
EVEN_EXT = _EV_KBR + B_KV


def _even_proj_kernel(x_ref, gain_ref, sh_ref, sc_ref, w_ref, wvt_ref, cos_ref, sin_ref, qg_ref, kg_ref,
                      qa_ref, ka_ref, vat_ref, qb_ref, kb_ref, vbt_ref):
    h = _norm_mod(x_ref[0], gain_ref[...], sh_ref[0], sc_ref[0]).astype(BF16)
    z = jnp.dot(h, w_ref[...], preferred_element_type=F32)
    vt = _nt_dot(wvt_ref[...], h)
    pad_rows = (lax.broadcasted_iota(jnp.int32, (VT_ROWS - HEAD_DIM, vt.shape[1]), 0) == 0).astype(BF16)
    cos = cos_ref[...]
    sin = sin_ref[...]
    qg = qg_ref[...]
    kg = kg_ref[...]

    def sl(off, i):
        return z[:, off + i * HEAD_DIM: off + (i + 1) * HEAD_DIM]

    for i in range(A_HEADS):
        zq = sl(_EV_QA, i)
        qa = _head_rms(zq) * (zq * qg * cos + sl(_EV_QAR, i) * sin)
        qa_ref[0, i] = (qa * (SCALE * LOG2E)).astype(BF16)
    for i in range(B_HEADS):
        qb_ref[0, i] = ((sl(_EV_QB, i) * cos + sl(_EV_QBR, i) * sin) * SCALE).astype(BF16)
    for i in range(A_KV_HEADS):
        zk = sl(_EV_KA, i)
        ka_ref[0, i] = (_head_rms(zk) * (zk * kg * cos + sl(_EV_KAR, i) * sin)).astype(BF16)
        vat_ref[0, i, :HEAD_DIM, :] = vt[i * HEAD_DIM:(i + 1) * HEAD_DIM].astype(BF16)
        vat_ref[0, i, HEAD_DIM:, :] = pad_rows
    for i in range(B_KV_HEADS):
        kb_ref[0, i] = (sl(_EV_KB, i) * cos + sl(_EV_KBR, i) * sin).astype(BF16)
        vbt_ref[0, i, :HEAD_DIM, :] = vt[A_KV + i * HEAD_DIM:A_KV + (i + 1) * HEAD_DIM].astype(BF16)
        vbt_ref[0, i, HEAD_DIM:, :] = pad_rows


def _rot_cols(w):
    d = w.shape[0]
    wh = w.reshape(d, -1, 2, ROPE_HALF)
    return jnp.stack([-wh[:, :, 1], wh[:, :, 0]], axis=2).reshape(w.shape)


def even_weight(w_in, q_gain, k_gain):
    qa, ka, va, qb, kb, vb = _split(w_in, EVEN_WIDTHS)
    qar = _rot_cols(qa * jnp.tile(q_gain, A_HEADS))
    kar = _rot_cols(ka * jnp.tile(k_gain, A_KV_HEADS))
    return jnp.concatenate([w_in, qar, kar, _rot_cols(qb), _rot_cols(kb)], axis=1).astype(BF16)


def even_proj(x, gain, shift, scale, w_ext, w_vt, cos2, sin2, q_gain, k_gain):
    b, s, d = x.shape
    tm = _pick_tile(s, 512)
    vec = pl.BlockSpec((1, 1, d), lambda bi, i: (bi, 0, 0))
    tab = pl.BlockSpec((tm, HEAD_DIM), lambda bi, i: (i, 0))
    hv = pl.BlockSpec((1, HEAD_DIM), lambda bi, i: (0, 0))

    def hm(n):
        return pl.BlockSpec((1, n, tm, HEAD_DIM), lambda bi, i: (bi, 0, i, 0))

    def shape(n):
        return jax.ShapeDtypeStruct((b, n, s, HEAD_DIM), BF16)

    def fm(n):
        return pl.BlockSpec((1, n, VT_ROWS, tm), lambda bi, i: (bi, 0, 0, i))

    def fshape(n):
        return jax.ShapeDtypeStruct((b, n, VT_ROWS, s), BF16)

    return pl.pallas_call(
        _even_proj_kernel,
        grid=(b, s // tm),
        in_specs=[
            pl.BlockSpec((1, tm, d), lambda bi, i: (bi, i, 0)),
            pl.BlockSpec((1, d), lambda bi, i: (0, 0)),
            vec, vec,
            pl.BlockSpec((d, EVEN_EXT), lambda bi, i: (0, 0)),
            pl.BlockSpec((A_KV + B_KV, d), lambda bi, i: (0, 0)),
            tab, tab, hv, hv,
        ],
        out_specs=[hm(A_HEADS), hm(A_KV_HEADS), fm(A_KV_HEADS), hm(B_HEADS), hm(B_KV_HEADS), fm(B_KV_HEADS)],
        out_shape=[shape(A_HEADS), shape(A_KV_HEADS), fshape(A_KV_HEADS),
                   shape(B_HEADS), shape(B_KV_HEADS), fshape(B_KV_HEADS)],
        compiler_params=_cparams("parallel", "parallel"),
        name="even_proj",
    )(x, gain, shift, scale, w_ext, w_vt, cos2, sin2, q_gain, k_gain)


def _odd_proj_kernel(x_ref, gain_ref, sh_ref, sc_ref, w_ref, q_ref, k_ref, v_ref, zd_ref):
    h = _norm_mod(x_ref[0], gain_ref[...], sh_ref[0], sc_ref[0])
    z = jnp.dot(h.astype(BF16), w_ref[...], preferred_element_type=F32)
    for i in range(C_HEADS):
        lo = i * HEAD_DIM
        q_ref[0, i] = (z[:, lo:lo + HEAD_DIM] * SCALE).astype(BF16)
        k_ref[0, i] = z[:, C_W + lo:C_W + lo + HEAD_DIM].astype(BF16)
        v_ref[0, i] = z[:, 2 * C_W + lo:2 * C_W + lo + HEAD_DIM].astype(BF16)
    zd_ref[0] = z[:, 3 * C_W:]


def odd_proj(x, gain, shift, scale, w):
    b, s, d = x.shape
    n = w.shape[1]
    tm = _pick_tile(s, 512)
    vec = pl.BlockSpec((1, 1, d), lambda bi, i: (bi, 0, 0))
    hm = pl.BlockSpec((1, C_HEADS, tm, HEAD_DIM), lambda bi, i: (bi, 0, i, 0))
    hshape = jax.ShapeDtypeStruct((b, C_HEADS, s, HEAD_DIM), BF16)
    return pl.pallas_call(
        _odd_proj_kernel,
        grid=(b, s // tm),
        in_specs=[
            pl.BlockSpec((1, tm, d), lambda bi, i: (bi, i, 0)),
            pl.BlockSpec((1, d), lambda bi, i: (0, 0)),
            vec, vec,
            pl.BlockSpec((d, n), lambda bi, i: (0, 0)),
        ],
        out_specs=[hm, hm, hm, pl.BlockSpec((1, tm, D_SHIFT_W), lambda bi, i: (bi, i, 0))],
        out_shape=[hshape, hshape, hshape, jax.ShapeDtypeStruct((b, s, D_SHIFT_W), F32)],
        compiler_params=_cparams("parallel", "parallel"),
        name="odd_proj",
    )(x, gain, shift, scale, w)


def _out_proj_kernel(x_ref, g_ref, y1_ref, y2_ref, w1_ref, w2_ref, o_ref, buf1_ref, buf2_ref, *, y2_heads):
    def gather_heads(y_ref, buf_ref):
        for h in range(y_ref.shape[1]):
            buf_ref[:, h * HEAD_DIM:(h + 1) * HEAD_DIM] = y_ref[0, h]
        return buf_ref[...]

    acc = jnp.dot(gather_heads(y1_ref, buf1_ref), w1_ref[...], preferred_element_type=F32)
    y2 = gather_heads(y2_ref, buf2_ref) if y2_heads else y2_ref[0]
    acc = acc + jnp.dot(y2, w2_ref[...], preferred_element_type=F32)
    o_ref[0] = x_ref[0] + g_ref[0] * acc


def out_proj(x, gate, y1, y2, w):
    b, s, d = x.shape
    nh = y1.shape[1]
    half = nh * HEAD_DIM
    tm = _pick_tile(s, 1024)
    y2_heads = y2.ndim == 4
    hm = pl.BlockSpec((1, nh, tm, HEAD_DIM), lambda bi, i: (bi, 0, i, 0))
    wspec = pl.BlockSpec((half, d), lambda bi, i: (0, 0))
    y2_spec = hm if y2_heads else pl.BlockSpec((1, tm, half), lambda bi, i: (bi, i, 0))
    return pl.pallas_call(
        functools.partial(_out_proj_kernel, y2_heads=y2_heads),
        grid=(b, s // tm),
        in_specs=[
            pl.BlockSpec((1, tm, d), lambda bi, i: (bi, i, 0)),
            pl.BlockSpec((1, 1, d), lambda bi, i: (bi, 0, 0)),
            hm, y2_spec, wspec, wspec,
        ],
        out_specs=pl.BlockSpec((1, tm, d), lambda bi, i: (bi, i, 0)),
        out_shape=jax.ShapeDtypeStruct((b, s, d), F32),
        scratch_shapes=[pltpu.VMEM((tm, half), BF16), pltpu.VMEM((tm, half), BF16)],
        compiler_params=_cparams("parallel", "parallel"),
        name="out_proj",
    )(x, gate, y1, y2, w[:half], w[half:])


def _ffn_kernel(x_ref, gain_ref, sh_ref, sc_ref, g_ref, wg_ref, wu_ref, wo_ref, o_ref, h_ref, acc_ref):
    j = pl.program_id(2)

    @pl.when(j == 0)
    def _():
        h_ref[...] = _norm_mod(x_ref[0], gain_ref[...], sh_ref[0], sc_ref[0]).astype(BF16)
        acc_ref[...] = jnp.zeros_like(acc_ref)

    h = h_ref[...]
    gate = jnp.dot(h, wg_ref[...], preferred_element_type=F32)
    up = jnp.dot(h, wu_ref[...], preferred_element_type=F32)
    act = gate * (1.0 / (1.0 + jnp.exp(-gate))) * up
    acc_ref[...] += jnp.dot(act.astype(BF16), wo_ref[...], preferred_element_type=F32)

    @pl.when(j == pl.num_programs(2) - 1)
    def _():
        o_ref[0] = x_ref[0] + g_ref[0] * acc_ref[...]


def ffn(x, gain, shift, scale, gate, w_in, w_out):
    b, s, d = x.shape
    f = w_out.shape[0]
    tm = _pick_tile(s, 512)
    th = 1408 if f % 1408 == 0 else _pick_tile(f, 256)
    nj = f // th
    vec = pl.BlockSpec((1, 1, d), lambda bi, i, j: (bi, 0, 0))
    return pl.pallas_call(
        _ffn_kernel,
        grid=(b, s // tm, nj),
        in_specs=[
            pl.BlockSpec((1, tm, d), lambda bi, i, j: (bi, i, 0)),
            pl.BlockSpec((1, d), lambda bi, i, j: (0, 0)),
            vec, vec, vec,
            pl.BlockSpec((d, th), lambda bi, i, j: (0, j)),
            pl.BlockSpec((d, th), lambda bi, i, j: (0, j + nj)),
            pl.BlockSpec((th, d), lambda bi, i, j: (j, 0)),
        ],
        out_specs=pl.BlockSpec((1, tm, d), lambda bi, i, j: (bi, i, 0)),
        out_shape=jax.ShapeDtypeStruct((b, s, d), F32),
        scratch_shapes=[pltpu.VMEM((tm, d), BF16), pltpu.VMEM((tm, d), F32)],
        compiler_params=_cparams("parallel", "parallel", "arbitrary"),
        name="ffn",
    )(x, gain, shift, scale, gate, w_in, w_in, w_out)


def _rms_kernel(x_ref, gain_ref, o_ref):
    x = x_ref[0]
    ms = jnp.mean(x * x, axis=-1, keepdims=True)
    o_ref[0] = x * lax.rsqrt(ms + RMS_EPS) * gain_ref[...]


def final_norm(x, gain):
    b, s, d = x.shape
    tm = _pick_tile(s, 1024)
    return pl.pallas_call(
        _rms_kernel,
        grid=(b, s // tm),
        in_specs=[pl.BlockSpec((1, tm, d), lambda bi, i: (bi, i, 0)),
                  pl.BlockSpec((1, d), lambda bi, i: (0, 0))],
        out_specs=pl.BlockSpec((1, tm, d), lambda bi, i: (bi, i, 0)),
        out_shape=jax.ShapeDtypeStruct((b, s, d), F32),
        compiler_params=_cparams("parallel", "parallel"),
        name="final_norm",
    )(x, gain)


def _flash_kernel(q_ref, k_ref, vt_ref, kx_ref, vxt_ref, o_ref, m_ref, acc_ref):
    kv = pl.program_id(3)
    g, tq, hd = q_ref.shape[1:]

    @pl.when(kv == 0)
    def _():
        m_ref[...] = jnp.full_like(m_ref, -jnp.inf)
        acc_ref[...] = jnp.zeros_like(acc_ref)

    q = q_ref[0].reshape(g * tq, hd)

    def scores(k):
        return _nt_dot(k, q)

    def accumulate(s, vt):
        m_prev = m_ref[...]
        m_new = jnp.maximum(m_prev, jnp.max(s, axis=0, keepdims=True))
        p = jnp.exp2(s - m_new).astype(BF16)
        acc_ref[...] = jnp.exp2(m_prev - m_new) * acc_ref[...] + jnp.dot(vt, p, preferred_element_type=F32)
        m_ref[...] = m_new

    tk = k_ref.shape[2]
    sub = min(tk, FLASH_SUB_KEYS)
    nsub = tk // sub
    s_next = scores(k_ref[0, 0, 0:sub, :])
    for c in range(nsub):
        s_cur = s_next
        if c + 1 < nsub:
            s_next = scores(k_ref[0, 0, (c + 1) * sub:(c + 2) * sub, :])
        accumulate(s_cur, vt_ref[0, 0, :, c * sub:(c + 1) * sub])

    @pl.when(kv == pl.num_programs(3) - 1)
    def _():
        accumulate(scores(kx_ref[0, 0]), vxt_ref[0, 0])
        acc = acc_ref[...]
        ot = (acc[:hd] / acc[hd:hd + 1]).astype(BF16)
        _transpose_heads(ot, o_ref.at[0], g, tq)


def global_attention(q, k, vt, kx, vxt):
    b, h, s, hd = q.shape
    hkv = k.shape[1]
    n_ctx = kx.shape[2]
    vr = vt.shape[2]
    g = h // hkv
    tq = _pick_tile(s, FLASH_TQ)
    tk = _pick_tile(s, FLASH_TK)
    q_spec = pl.BlockSpec((1, g, tq, hd), lambda bi, gi, i, j: (bi, gi, i, 0))
    return pl.pallas_call(
        _flash_kernel,
        grid=(b, hkv, s // tq, s // tk),
        in_specs=[
            q_spec,
            pl.BlockSpec((1, 1, tk, hd), lambda bi, gi, i, j: (bi, gi, j, 0)),
            pl.BlockSpec((1, 1, vr, tk), lambda bi, gi, i, j: (bi, gi, 0, j)),
            pl.BlockSpec((1, 1, n_ctx, hd), lambda bi, gi, i, j: (bi, gi, 0, 0)),
            pl.BlockSpec((1, 1, vr, n_ctx), lambda bi, gi, i, j: (bi, gi, 0, 0)),
        ],
        out_specs=q_spec,
        out_shape=jax.ShapeDtypeStruct((b, h, s, hd), BF16),
        scratch_shapes=[pltpu.VMEM((1, g * tq), F32), pltpu.VMEM((vr, g * tq), F32)],
        compiler_params=_cparams("parallel", "parallel", "parallel", "arbitrary"),
        name="global_attention",
    )(q, k, vt, kx, vxt)


def _window_kernel(sink_ref, band_ref, q_ref, kp_ref, kc_ref, kn_ref, vp_ref, vc_ref, vn_ref, kx_ref, vxt_ref,
                   o_ref, *, seq):
    i = pl.program_id(1)
    groups = range(kc_ref.shape[1])
    g = q_ref.shape[1] // kc_ref.shape[1]
    tq, hd = q_ref.shape[2:]
    w = kp_ref.shape[2]
    nq = g * tq
    span = tq + 2 * w
    kpos = i * tq - w + lax.broadcasted_iota(jnp.int32, (span, 1), 0)
    outside = jnp.where((kpos >= 0) & (kpos < seq), 0.0, NEG_INF)
    mask = band_ref[...] + outside
    head = lax.broadcasted_iota(jnp.int32, (1, nq), 1) // tq
    q = [q_ref[0, gi * g:(gi + 1) * g].reshape(nq, hd) for gi in groups]
    k = [jnp.concatenate([kp_ref[0, gi], kc_ref[0, gi], kn_ref[0, gi]], axis=0) for gi in groups]
    vt = [jnp.concatenate([vp_ref[0, gi], vc_ref[0, gi], vn_ref[0, gi]], axis=1) for gi in groups]
    s_loc = [_nt_dot(k[gi], q[gi]) + mask for gi in groups]
    s_ctx = [_nt_dot(kx_ref[0, gi], q[gi]) for gi in groups]
    sink = []
    for gi in groups:
        col = jnp.zeros((1, nq), F32)
        for hh in range(g):
            col = jnp.where(head == hh, sink_ref[gi * g + hh], col)
        sink.append(col)
    m = [jnp.maximum(jnp.maximum(jnp.max(s_loc[gi], axis=0, keepdims=True),
                                 jnp.max(s_ctx[gi], axis=0, keepdims=True)), sink[gi]) for gi in groups]
    acc = [jnp.dot(vt[gi], jnp.exp(s_loc[gi] - m[gi]).astype(BF16), preferred_element_type=F32)
           + jnp.dot(vxt_ref[0, gi], jnp.exp(s_ctx[gi] - m[gi]).astype(BF16), preferred_element_type=F32)
           for gi in groups]
    for gi in groups:
        l = acc[gi][hd:hd + 1] + jnp.exp(sink[gi] - m[gi])
        _transpose_heads((acc[gi][:hd] / l).astype(BF16), o_ref.at[0, gi * g:(gi + 1) * g], g, tq)


def window_attention(q, k, vt, kx, vxt, sink):
    b, h, s, hd = q.shape
    hkv = k.shape[1]
    vr = vt.shape[2]
    g = h // hkv
    n_ctx = kx.shape[2]
    tq = _pick_tile(s, 256)
    r = tq // WINDOW
    nwb = s // WINDOW

    def before(i):
        return jnp.maximum(i * r - 1, 0)

    def after(i):
        return jnp.minimum((i + 1) * r, nwb - 1)

    span = tq + 2 * WINDOW
    krow = np.arange(span)[:, None] - WINDOW
    qcol = np.arange(g * tq)[None, :] % tq
    band = jnp.asarray(np.where(np.abs(krow - qcol) <= WINDOW, 0.0, NEG_INF), F32)
    qspec = pl.BlockSpec((1, h, tq, hd), lambda bi, i: (bi, 0, i, 0))
    return pl.pallas_call(
        functools.partial(_window_kernel, seq=s),
        grid=(b, s // tq),
        in_specs=[
            pl.BlockSpec(memory_space=pltpu.SMEM),
            pl.BlockSpec((span, g * tq), lambda bi, i: (0, 0)),
            qspec,
            pl.BlockSpec((1, hkv, WINDOW, hd), lambda bi, i: (bi, 0, before(i), 0)),
            pl.BlockSpec((1, hkv, tq, hd), lambda bi, i: (bi, 0, i, 0)),
            pl.BlockSpec((1, hkv, WINDOW, hd), lambda bi, i: (bi, 0, after(i), 0)),
            pl.BlockSpec((1, hkv, vr, WINDOW), lambda bi, i: (bi, 0, 0, before(i))),
            pl.BlockSpec((1, hkv, vr, tq), lambda bi, i: (bi, 0, 0, i)),
            pl.BlockSpec((1, hkv, vr, WINDOW), lambda bi, i: (bi, 0, 0, after(i))),
            pl.BlockSpec((1, hkv, n_ctx, hd), lambda bi, i: (bi, 0, 0, 0)),
            pl.BlockSpec((1, hkv, vr, n_ctx), lambda bi, i: (bi, 0, 0, 0)),
        ],
        out_specs=qspec,
        out_shape=jax.ShapeDtypeStruct((b, h, s, hd), BF16),
        compiler_params=_cparams("parallel", "parallel"),
        name="window_attention",
    )(sink, band, q, k, k, k, vt, vt, vt, kx, vxt)


def _ctx_attn_kernel(sink_ref, q_ref, k_ref, v_ref, o_ref, *, base2, feature_major):
    hi = pl.program_id(1)
    s = _nt_dot(q_ref[0, 0], k_ref[0, 0])
    sink = sink_ref[hi]
    m = jnp.maximum(jnp.max(s, axis=-1, keepdims=True), sink)
    ex = jnp.exp2 if base2 else jnp.exp
    p = ex(s - m)
    l = jnp.sum(p, axis=-1, keepdims=True) + ex(sink - m)
    if feature_major:
        o = _nt_dot(p.astype(BF16), v_ref[0, 0, :HEAD_DIM, :])
    else:
        o = jnp.dot(p.astype(BF16), v_ref[0, 0], preferred_element_type=F32)
    o_ref[0, 0] = (o / l).astype(o_ref.dtype)


def ctx_attention(q, k, v, sink, base2=False, feature_major=False):
    b, h, n, hd = q.shape
    g = h // k.shape[1]
    kv = pl.BlockSpec((1, 1, n, hd), lambda bi, hi: (bi, hi // g, 0, 0))
    vs = pl.BlockSpec((1, 1, v.shape[2], n), lambda bi, hi: (bi, hi // g, 0, 0)) if feature_major else kv
    qs = pl.BlockSpec((1, 1, n, hd), lambda bi, hi: (bi, hi, 0, 0))
    return pl.pallas_call(
        functools.partial(_ctx_attn_kernel, base2=base2, feature_major=feature_major),
        grid=(b, h),
        in_specs=[pl.BlockSpec(memory_space=pltpu.SMEM), qs, kv, vs],
        out_specs=qs,
        out_shape=jax.ShapeDtypeStruct((b, h, n, hd), BF16),
        compiler_params=_cparams("parallel", "parallel"),
        name="ctx_attention",
    )(sink, q, k, v)


def _na_kernel(q_ref, k_ref, v_ref, kx_ref, vx_ref, bias_ref, o_ref, *, rows):
    t = pl.program_id(2)
    nkeys = NA_KEY_ROWS * GRID_W
    kstart = jnp.clip(t * NA_TILE_ROWS - NA_ROWS // 2, 0, rows - NA_KEY_ROWS)
    off = pl.multiple_of(kstart * GRID_W, GRID_W)
    heads = range(q_ref.shape[1])
    tq = q_ref.shape[2]
    q = [q_ref[0, h] for h in heads]
    s_loc = [_nt_dot(k_ref[0, h, pl.ds(off, nkeys), :], q[h]) + bias_ref[0, h] for h in heads]
    s_ctx = [_nt_dot(kx_ref[0, h], q[h]) for h in heads]
    m = [jnp.maximum(jnp.max(s_loc[h], axis=0, keepdims=True), jnp.max(s_ctx[h], axis=0, keepdims=True))
         for h in heads]
    p_loc = [jnp.exp(s_loc[h] - m[h]) for h in heads]
    p_ctx = [jnp.exp(s_ctx[h] - m[h]) for h in heads]
    l = [jnp.sum(p_loc[h], axis=0, keepdims=True) + jnp.sum(p_ctx[h], axis=0, keepdims=True) for h in heads]
    ot = [_tn_dot(v_ref[0, h, pl.ds(off, nkeys), :], p_loc[h].astype(BF16))
          + _tn_dot(vx_ref[0, h], p_ctx[h].astype(BF16)) for h in heads]
    for h in heads:
        _transpose_heads((ot[h] / l[h]).astype(BF16), o_ref.at[0, h:h + 1], 1, tq)


def na_bias_tables(rpb, rows):
    nh = rpb.shape[0]
    i = np.arange(NA_TILE_ROWS)
    j = np.arange(NA_KEY_ROWS)
    qc = np.arange(GRID_W)
    kc = np.arange(GRID_W)
    cs = np.clip(qc - NA_COLS // 2, 0, GRID_W - NA_COLS)
    col_valid = (kc[:, None] >= cs[None, :]) & (kc[:, None] < cs[None, :] + NA_COLS)
    dcol = np.clip(kc[:, None] - qc[None, :] + NA_COLS - 1, 0, 2 * NA_COLS - 2)
    per_drow = jnp.where(col_valid[None, None], rpb.astype(F32)[:, :, dcol], NEG_INF)
    masked = jnp.full((nh, 1, GRID_W, GRID_W), NEG_INF, F32)
    per_drow = jnp.concatenate([per_drow, masked], axis=1)
    out = []
    for r0 in (0, NA_TILE_ROWS, rows - NA_TILE_ROWS):
        r = r0 + i
        rs = np.clip(r - NA_ROWS // 2, 0, rows - NA_ROWS)
        kstart = int(np.clip(r0 - NA_ROWS // 2, 0, rows - NA_KEY_ROWS))
        krow = kstart + j
        row_valid = (krow[:, None] >= rs[None, :]) & (krow[:, None] < rs[None, :] + NA_ROWS)
        drow = np.where(row_valid, krow[:, None] - r[None, :] + NA_ROWS - 1, 2 * NA_ROWS - 1)
        cols = [per_drow[:, drow[:, ii]] for ii in range(NA_TILE_ROWS)]
        dense = jnp.stack(cols, axis=3)
        out.append(dense.reshape(nh, NA_KEY_ROWS * GRID_W, NA_TILE_ROWS * GRID_W))
    return jnp.stack(out)


def neighborhood_attention(q, k, v, kx, vx, bias):
    b, h, s, hd = q.shape
    rows = s // GRID_W
    nt = rows // NA_TILE_ROWS
    tq = NA_TILE_ROWS * GRID_W
    nkeys = NA_KEY_ROWS * GRID_W
    n_ctx = kx.shape[2]
    hs = NA_HEADS_PER_STEP
    full = pl.BlockSpec((1, hs, s, hd), lambda bi, hi, t: (bi, hi, 0, 0))
    cx = pl.BlockSpec((1, hs, n_ctx, hd), lambda bi, hi, t: (bi, hi, 0, 0))
    tile = pl.BlockSpec((1, hs, tq, hd), lambda bi, hi, t: (bi, hi, t, 0))

    def case(t):
        return jnp.where(t == 0, 0, jnp.where(t == nt - 1, 2, 1))

    return pl.pallas_call(
        functools.partial(_na_kernel, rows=rows),
        grid=(b, h // hs, nt),
        in_specs=[tile, full, full, cx, cx,
                  pl.BlockSpec((1, hs, nkeys, tq), lambda bi, hi, t: (case(t), hi, 0, 0))],
        out_specs=tile,
        out_shape=jax.ShapeDtypeStruct((b, h, s, hd), BF16),
        compiler_params=_cparams("parallel", "parallel", "arbitrary"),
        name="neighborhood_attention",
    )(q, k, v, kx, vx, bias)


def _mm(a, b):
    return jnp.dot(a.astype(BF16), b.astype(BF16), preferred_element_type=F32)


def _tri_cumsum(tri, x):
    t16 = tri.astype(BF16)
    hi = x.astype(BF16)
    r1 = x - hi.astype(F32)
    mid = r1.astype(BF16)
    lo = (r1 - mid.astype(F32)).astype(BF16)
    out = jnp.dot(t16, hi, preferred_element_type=F32)
    out = out + jnp.dot(t16, mid, preferred_element_type=F32)
    return out + jnp.dot(t16, lo, preferred_element_type=F32)


def _mm_nt(a, b):
    return lax.dot_general(a.astype(BF16), b.astype(BF16), (((1,), (1,)), ((), ())),
                           preferred_element_type=F32)


def _mm_tn(a, b):
    return lax.dot_general(a.astype(BF16), b.astype(BF16), (((0,), (0,)), ((), ())),
                           preferred_element_type=F32)


def _rwkv_chains(chains):
    t, w = chains[0][0].shape
    nh = w // HEAD_DIM
    n = range(len(chains))
    lw, kd, a, r, v, kap, st, rev = (list(col) for col in zip(*chains))
    row = lax.broadcasted_iota(jnp.int32, (t, w), 0)
    colj = lax.broadcasted_iota(jnp.int32, (t, w), 1) % t
    lane_head = lax.broadcasted_iota(jnp.int32, (1, w), 1) // HEAD_DIM
    strict_d = (colj < row, colj > row)
    incl_d = (colj <= row, colj >= row)
    tri_d = tuple(m[:, :t].astype(F32) for m in incl_d)
    strict = [strict_d[int(rev[i])] for i in n]
    incl = [incl_d[int(rev[i])] for i in n]

    def bd(y):
        return jnp.concatenate([jnp.where(lane_head == h, y, 0.0) for h in range(nh)], axis=0)

    def bdmm(xs, ys):
        return [_mm(xs[i], bd(ys[i])) for i in n]

    c_incl = [_tri_cumsum(tri_d[int(rev[i])], lw[i]) for i in n]
    c_excl = [c_incl[i] - lw[i] for i in n]
    c_mid = [c_incl[i][t // 2:t // 2 + 1, :] for i in n]
    c_end = [c_incl[i][0:1, :] if rev[i] else c_incl[i][t - 1:t, :] for i in n]
    bvec = [kap[i] * a[i] for i in n]
    e_neg = [jnp.exp(c_mid[i] - c_incl[i]) for i in n]
    left = [jnp.concatenate([kap[i] * jnp.exp(c_excl[i] - c_mid[i]), r[i] * jnp.exp(c_incl[i] - c_mid[i])],
                            axis=0) for i in n]
    wt = [jnp.concatenate([bd(bvec[i] * e_neg[i]), bd(kd[i] * e_neg[i])], axis=0) for i in n]
    m = [_mm_nt(left[i], wt[i]) for i in n]
    l_b = [jnp.where(strict[i], m[i][:t, :w], 0.0) for i in n]
    l_k = [jnp.where(strict[i], m[i][:t, w:], 0.0) for i in n]
    m_br = [jnp.where(incl[i], m[i][t:, :w], 0.0) for i in n]
    m_kr = [jnp.where(incl[i], m[i][t:, w:], 0.0) for i in n]

    blk = RWKV_INV_BLOCK
    same = row // blk == colj // blk
    eye = jnp.where(row == colj, 1.0, 0.0)
    pw = [-jnp.where(same, l_b[i], 0.0) for i in n]
    inv = [eye + pw[i] for i in n]
    span = 2
    while span < blk:
        pw = bdmm(pw, pw)
        step = bdmm(inv, pw)
        inv = [inv[i] + step[i] for i in n]
        span *= 2
    while blk < t:
        pair = (row // (2 * blk) == colj // (2 * blk)) & (row // blk != colj // blk)
        off = [jnp.where(pair, l_b[i], 0.0) for i in n]
        step = bdmm(bdmm(inv, off), inv)
        inv = [inv[i] - step[i] for i in n]
        blk *= 2

    kt = [kap[i] * jnp.exp(c_excl[i]) for i in n]
    rt = [r[i] * jnp.exp(c_incl[i]) for i in n]
    both = bdmm([jnp.concatenate([l_k[i], m_kr[i]], axis=0) for i in n], v)
    lkv = [both[i][:t] for i in n]
    mkv = [both[i][t:] for i in n]
    wk = bdmm(inv, kt)
    u0 = bdmm(inv, lkv)
    mbw = bdmm(m_br, wk)
    mbu = bdmm(m_br, u0)
    rq = [rt[i] - mbw[i] for i in n]
    y0 = [mkv[i] - mbu[i] for i in n]
    e_end = [jnp.exp(c_end[i] - c_incl[i]) for i in n]
    b_hat = [bvec[i] * e_end[i] for i in n]
    k_hat = [kd[i] * e_end[i] for i in n]
    ri = lax.broadcasted_iota(jnp.int32, (w, w), 0)
    ci = lax.broadcasted_iota(jnp.int32, (w, w), 1)
    same_head = ri // HEAD_DIM == ci // HEAD_DIM
    diag = ri == ci
    bwu = [_mm_tn(b_hat[i], jnp.concatenate([wk[i], u0[i]], axis=1)) for i in n]
    bw = [bwu[i][:, :w] for i in n]
    bu = [bwu[i][:, w:] for i in n]
    kv = [_mm_tn(k_hat[i], v[i]) for i in n]
    gt = [jnp.where(diag, jnp.exp(c_end[i]), 0.0) - jnp.where(same_head, bw[i], 0.0) for i in n]
    c0t = [jnp.where(same_head, kv[i] - bu[i], 0.0) for i in n]
    ys = [_mm(rq[i], st[i]) + y0[i] for i in n]
    sts = [_mm(gt[i], st[i]) + c0t[i] for i in n]
    return ys, sts


def _rwkv_kernel(lwf_ref, kdf_ref, af_ref, lwb_ref, kdb_ref, ab_ref, rf_ref, vf_ref, kkf_ref,
                 rb_ref, vb_ref, kkb_ref, sin_ref, yf_ref, yb_ref, sout_ref, st_ref):
    c = pl.program_id(0)

    @pl.when(c == 0)
    def _():
        st_ref[...] = sin_ref[...]

    bsz = lwf_ref.shape[0]
    groups = lwf_ref.shape[2] // RWKV_LANES
    chains, slots = [], []
    for bi in range(bsz):
        for d, (lw_ref, kd_ref, a_ref, r_ref, v_ref, kk_ref, y_ref) in enumerate((
                (lwf_ref, kdf_ref, af_ref, rf_ref, vf_ref, kkf_ref, yf_ref),
                (lwb_ref, kdb_ref, ab_ref, rb_ref, vb_ref, kkb_ref, yb_ref))):
            for hg in range(groups):
                lanes = slice(hg * RWKV_LANES, (hg + 1) * RWKV_LANES)
                chain = d * groups + hg
                chains.append((lw_ref[bi, :, lanes], kd_ref[bi, :, lanes], a_ref[bi, :, lanes],
                               r_ref[bi, :, lanes], v_ref[bi, :, lanes], kk_ref[bi, :, lanes],
                               st_ref[bi, chain], bool(d)))
                slots.append((y_ref, bi, lanes, chain))
    ys, sts = _rwkv_chains(chains)
    for (y_ref, bi, lanes, chain), y, st in zip(slots, ys, sts):
        y_ref[bi, :, lanes] = y
        st_ref[bi, chain] = st

    @pl.when(c == pl.num_programs(0) - 1)
    def _():
        sout_ref[...] = st_ref[...]


def rwkv_scan(lw, kd, a, r, v, kk, state):
    b, length, w = r.shape
    t = RWKV_CHUNK
    nc = length // t
    fwd = pl.BlockSpec((b, t, w), lambda c: (0, c, 0))
    bwd = pl.BlockSpec((b, t, w), lambda c: (0, nc - 1 - c, 0))
    st_spec = pl.BlockSpec(state.shape, lambda c: (0, 0, 0, 0))
    y_shape = jax.ShapeDtypeStruct((b, length, w), F32)
    return pl.pallas_call(
        _rwkv_kernel,
        grid=(nc,),
        in_specs=[fwd, fwd, fwd, bwd, bwd, bwd, fwd, fwd, fwd, bwd, bwd, bwd, st_spec],
        out_specs=[fwd, bwd, st_spec],
        out_shape=[y_shape, y_shape, jax.ShapeDtypeStruct(state.shape, F32)],
        scratch_shapes=[pltpu.VMEM(state.shape, F32)],
        compiler_params=_cparams("arbitrary"),
        name="rwkv_scan",
    )(lw[0], kd[0], a[0], lw[1], kd[1], a[1], r, v, kk, r, v, kk, state)


def _exact_dot(a, b):
    return jnp.dot(a, b, preferred_element_type=F32, precision=lax.Precision.HIGHEST)


def _head_sum_matrix(width, value):
    r = lax.broadcasted_iota(jnp.int32, (width, width), 0) // HEAD_DIM
    c = lax.broadcasted_iota(jnp.int32, (width, width), 1) // HEAD_DIM
    return jnp.where(r == c, value, 0.0).astype(BF16)


def _head_sum(x, mat):
    hi = x.astype(BF16)
    lo = (x - hi.astype(F32)).astype(BF16)
    return jnp.dot(hi, mat, preferred_element_type=F32) + jnp.dot(lo, mat, preferred_element_type=F32)


def _sigmoid(x):
    return 1.0 / (1.0 + jnp.exp(-x))


def _rwkv_prep_kernel(z_ref, zp_ref, zn_ref, mu_ref, kk_ref, ka_ref, w0_ref, w2_ref, a0_ref, a2_ref, g2_ref,
                      r_ref, v_ref, kn_ref, g_ref, lw0_ref, kd0_ref, av0_ref, lw1_ref, kd1_ref, av1_ref):
    i = pl.program_id(1)
    z = z_ref[0]
    tm = z.shape[0]
    halo = zp_ref.shape[1]
    row = lax.broadcasted_iota(jnp.int32, (tm, 1), 0)
    before = jnp.where(i > 0, zp_ref[0, halo - 1:halo, :], 0.0)
    after = jnp.where(i < pl.num_programs(1) - 1, zn_ref[0, 0:1, :], 0.0)
    z_prev = jnp.where(row == 0, before, pltpu.roll(z, 1, 0))
    z_next = jnp.where(row == tm - 1, after, pltpu.roll(z, tm - 1, 0))
    zs = z + (0.5 * (z_prev + z_next) - z) * mu_ref[...]

    r = zs[:, :D_W]
    k = zs[:, D_W:2 * D_W]
    v = zs[:, 2 * D_W:3 * D_W]
    o = 3 * D_W
    wd = zs[:, o:o + DECAY_LORA]
    ad = zs[:, o + DECAY_LORA:o + DECAY_LORA + ICLR_LORA]
    gd = zs[:, o + DECAY_LORA + ICLR_LORA:]
    r_ref[0] = r
    v_ref[0] = v
    g_ref[0] = _exact_dot(_sigmoid(gd), g2_ref[...])
    kraw = k * kk_ref[...]
    ss = _head_sum(kraw * kraw, _head_sum_matrix(D_W, 1.0))
    kn_ref[0] = kraw / jnp.maximum(jnp.sqrt(ss), 1e-12)
    twd = jnp.tanh(wd)
    for d, (lw_ref, kd_ref, av_ref) in enumerate(((lw0_ref, kd0_ref, av0_ref), (lw1_ref, kd1_ref, av1_ref))):
        x = -(w0_ref[d:d + 1, :] + _exact_dot(twd, w2_ref[d]))
        softplus = jnp.maximum(x, 0.0) + jnp.log(1.0 + jnp.exp(-jnp.abs(x)))
        lw_ref[0] = -jnp.exp(-softplus - 0.5)
        a = _sigmoid(a0_ref[d:d + 1, :] + _exact_dot(ad, a2_ref[d]))
        av_ref[0] = a
        kd_ref[0] = k * (1.0 + (a - 1.0) * ka_ref[...])


def rwkv_prep(zd, mu, k_k, k_a, w0, w2, a0, a2, g2):
    b, length, width = zd.shape
    tm = _pick_tile(length, 512)
    halo = 8
    nb = length // halo
    per = tm // halo
    row = pl.BlockSpec((1, tm, D_W), lambda bi, i: (bi, i, 0))
    shape = jax.ShapeDtypeStruct((b, length, D_W), F32)

    def full(a):
        return pl.BlockSpec(a.shape, lambda bi, i: (0,) * a.ndim)

    params = (mu, k_k, k_a, w0, w2, a0, a2, g2)
    return pl.pallas_call(
        _rwkv_prep_kernel,
        grid=(b, length // tm),
        in_specs=[
            pl.BlockSpec((1, tm, width), lambda bi, i: (bi, i, 0)),
            pl.BlockSpec((1, halo, width), lambda bi, i: (bi, jnp.maximum(i * per - 1, 0), 0)),
            pl.BlockSpec((1, halo, width), lambda bi, i: (bi, jnp.minimum((i + 1) * per, nb - 1), 0)),
        ] + [full(p) for p in params],
        out_specs=[row] * 10,
        out_shape=[shape] * 10,
        compiler_params=_cparams("parallel", "parallel"),
        name="rwkv_prep",
    )(zd, zd, zd, *params)


def _rwkv_readout_kernel(yf_ref, yb_ref, r_ref, v_ref, kd0_ref, kd1_ref, g_ref, rk_ref, lnw_ref, lnb_ref, o_ref):
    y = yf_ref[0] + yb_ref[0]
    avg = _head_sum_matrix(D_W, 1.0 / HEAD_DIM)
    mean = _head_sum(y, avg)
    yc = y - mean
    var = _head_sum(yc * yc, avg)
    yn = yc * lax.rsqrt(var + GN_EPS) * lnw_ref[...] + lnb_ref[...]
    bonus = _head_sum(r_ref[0] * (kd0_ref[0] + kd1_ref[0]) * rk_ref[...], _head_sum_matrix(D_W, 1.0))
    o_ref[0] = ((yn + bonus * v_ref[0]) * g_ref[0]).astype(o_ref.dtype)


def rwkv_readout(yf, yb, r, v, kd0, kd1, g, r_k, ln_w, ln_b):
    b, length, w = yf.shape
    tm = _pick_tile(length, 512)
    row = pl.BlockSpec((1, tm, w), lambda bi, i: (bi, i, 0))
    vec = pl.BlockSpec((1, w), lambda bi, i: (0, 0))
    return pl.pallas_call(
        _rwkv_readout_kernel,
        grid=(b, length // tm),
        in_specs=[row] * 7 + [vec] * 3,
        out_specs=row,
        out_shape=jax.ShapeDtypeStruct((b, length, w), BF16),
        compiler_params=_cparams("parallel", "parallel"),
        name="rwkv_readout",
    )(yf, yb, r, v, kd0, kd1, g, r_k, ln_w, ln_b)


def _rope_tables(n_tokens):
    t = jnp.arange(n_tokens, dtype=jnp.int32)
    row = (t // GRID_W).astype(F32)
    col = (t % GRID_W).astype(F32)
    inv = ROPE_THETA ** (-jnp.arange(ROPE_FREQS, dtype=F32) / ROPE_FREQS)
    ang = jnp.concatenate([row[:, None] * inv, col[:, None] * inv], axis=-1)
    cos, sin = jnp.cos(ang), jnp.sin(ang)
    return jnp.concatenate([cos, cos], axis=-1), jnp.concatenate([sin, sin], axis=-1)


def _rwkv_mixer(zd, zdc, params, r_k, ln_w, ln_b, need_ctx):
    lat = rwkv_prep(zd, *params)
    cx = rwkv_prep(zdc, *params)
    bsz = zd.shape[0]
    chains = 2 * (D_W // RWKV_LANES)
    zero = jnp.zeros((bsz, chains, RWKV_LANES, RWKV_LANES), F32)

    def scan(p, state):
        r, v, kk, _, lw0, kd0, a0, lw1, kd1, a1 = p
        return rwkv_scan((lw0, lw1), (kd0, kd1), (a0, a1), r, v, kk, state)

    def readout(p, yf, yb):
        r, v, _, g, _, kd0, _, _, kd1, _ = p
        return rwkv_readout(yf, yb, r, v, kd0, kd1, g, r_k, ln_w, ln_b)

    ycf, ycb, state = scan(cx, zero)
    yf, yb, _ = scan(lat, state)
    y_d = readout(lat, yf, yb)
    yc_d = readout(cx, ycf, ycb) if need_ctx else None
    return y_d, yc_d


def kernel(x, c, ctx, c_ctx, w_mod, b_mod, norm_mix, norm_ffn, w_in_even, w_out_even, q_norm_a, k_norm_a,
           sink_b, w_in_odd, w_out_odd, rpb_c, shift_mu, decay_w0, decay_w2, iclr_a0, iclr_a2, gate_g2,
           k_k, k_a, r_k, ln_x_w, ln_x_b, w_ffn_in, w_ffn_out, norm_out):
    bsz, s, d = x.shape
    n_ctx = ctx.shape[1]
    cos2, sin2 = _rope_tables(s)
    cos_ctx = jnp.ones((n_ctx, HEAD_DIM), F32)
    sin_ctx = jnp.zeros((n_ctx, HEAD_DIM), F32)
    act = jnp.zeros((8, d), F32).at[:bsz].set(jax.nn.silu(c)).at[bsz].set(jax.nn.silu(c_ctx))
    mods = modulation(act.astype(BF16), w_mod, b_mod)
    w_out_even = w_out_even.astype(BF16)
    w_in_odd = w_in_odd.astype(BF16)
    w_out_odd = w_out_odd.astype(BF16)
    w_ffn_in = w_ffn_in.astype(BF16)
    w_ffn_out = w_ffn_out.astype(BF16)
    no_sink = jnp.full((A_HEADS,), NEG_INF, F32)
    for layer in range(DEPTH):
        need_ctx = layer < DEPTH - 1
        mod = mods[layer, :bsz][:, None, :]
        modc = jnp.broadcast_to(mods[layer, bsz][None, None, :], (bsz, 1, 6 * d))
        sh1, sc1, g1, sh2, sc2, g2 = jnp.split(mod, 6, axis=-1)
        csh1, csc1, cg1, csh2, csc2, cg2 = jnp.split(modc, 6, axis=-1)
        gain_mix = norm_mix[layer][None, :]
        gain_ffn = norm_ffn[layer][None, :]
        i = layer // 2
        if layer % 2 == 0:
            w_out = w_out_even[i]
            qg, kg = q_norm_a[i][None, :], k_norm_a[i][None, :]
            w_ext = even_weight(w_in_even[i], q_norm_a[i], k_norm_a[i])
            w_vt = jnp.concatenate([w_in_even[i][:, _EV_VA:_EV_VA + A_KV],
                                    w_in_even[i][:, _EV_VB:_EV_VB + B_KV]], axis=1).T.astype(BF16)
            qa, ka, vat, qb, kb, vbt = even_proj(x, gain_mix, sh1, sc1, w_ext, w_vt, cos2, sin2, qg, kg)
            qca, kca, vcat, qcb, kcb, vcbt = even_proj(ctx, gain_mix, csh1, csc1, w_ext, w_vt,
                                                        cos_ctx, sin_ctx, qg, kg)
            y1 = global_attention(qa, ka, vat, kca, vcat)
            y2 = window_attention(qb, kb, vbt, kcb, vcbt, sink_b[i])
            if need_ctx:
                yc1 = ctx_attention(qca, kca, vcat, no_sink, base2=True, feature_major=True)
                yc2 = ctx_attention(qcb, kcb, vcbt, sink_b[i], feature_major=True)
        else:
            w_out = w_out_odd[i]
            q, k, v, zd = odd_proj(x, gain_mix, sh1, sc1, w_in_odd[i])
            qc, kc, vc, zdc = odd_proj(ctx, gain_mix, csh1, csc1, w_in_odd[i])
            bias = na_bias_tables(rpb_c[i], s // GRID_W)
            y1 = neighborhood_attention(q, k, v, kc, vc, bias)
            params = (shift_mu[i][None, :], k_k[i][None, :], k_a[i][None, :], decay_w0[i], decay_w2[i],
                      iclr_a0[i], iclr_a2[i], gate_g2[i])
            y2, yc2 = _rwkv_mixer(zd, zdc, params, r_k[i].reshape(1, D_W), ln_x_w[i][None, :],
                                  ln_x_b[i][None, :], need_ctx)
            if need_ctx:
                yc1 = ctx_attention(qc, kc, vc, no_sink)
        x = out_proj(x, g1, y1, y2, w_out)
        x = ffn(x, gain_ffn, sh2, sc2, g2, w_ffn_in[layer], w_ffn_out[layer])
        if need_ctx:
            ctx = out_proj(ctx, cg1, yc1, yc2, w_out)
            ctx = ffn(ctx, gain_ffn, csh2, csc2, cg2, w_ffn_in[layer], w_ffn_out[layer])
    return final_norm(x, norm_out[None, :])
```

```python
import functools

import jax
import jax.numpy as jnp
import numpy as np
from jax import lax
from jax.experimental import pallas as pl
from jax.experimental.pallas import tpu as pltpu

F32 = jnp.float32
BF16 = jnp.bfloat16

D_MODEL = 1024
DEPTH = 4
GRID_W = 64
HEAD_DIM = 64
SCALE = HEAD_DIM ** -0.5
ROPE_HALF = HEAD_DIM // 2
ROPE_FREQS = HEAD_DIM // 4
ROPE_THETA = 10000.0
A_HEADS = 8
A_KV_HEADS = 2
B_HEADS = 8
B_KV_HEADS = 2
WINDOW = 128
C_HEADS = 8
NA_ROWS = 8
NA_COLS = 16
D_HEADS = 8
DECAY_LORA = 64
ICLR_LORA = 64
GATE_LORA = 128
GN_EPS = 64e-5
RMS_EPS = 1e-6
NEG_INF = -1e30
FFN_HIDDEN = -(-8 * D_MODEL // (3 * 256)) * 256

A_Q = A_HEADS * HEAD_DIM
A_KV = A_KV_HEADS * HEAD_DIM
B_Q = B_HEADS * HEAD_DIM
B_KV = B_KV_HEADS * HEAD_DIM
C_W = C_HEADS * HEAD_DIM
D_W = D_HEADS * HEAD_DIM
EVEN_WIDTHS = (A_Q, A_KV, A_KV, B_Q, B_KV, B_KV)
D_SHIFT_W = 3 * D_W + DECAY_LORA + ICLR_LORA + GATE_LORA

VMEM_LIMIT_BYTES = 56 * 1024 * 1024
RWKV_CHUNK = HEAD_DIM
RWKV_INV_BLOCK = 16
RWKV_LANES = 256
FLASH_SUB_KEYS = 512
FLASH_TQ = 1024
FLASH_TK = 8192
LOG2E = 1.4426950408889634
VT_ROWS = HEAD_DIM + 16
NA_TILE_ROWS = 8
NA_HEADS_PER_STEP = 2
NA_KEY_ROWS = NA_TILE_ROWS + NA_ROWS


def _cparams(*sem):
    return pltpu.CompilerParams(dimension_semantics=sem, vmem_limit_bytes=VMEM_LIMIT_BYTES)


def _split(t, widths):
    return jnp.split(t, [int(o) for o in np.cumsum(widths)[:-1]], axis=-1)


def _pick_tile(n, target):
    t = min(n, target)
    while n % t:
        t //= 2
    return t


def _mod_kernel(a_ref, w_ref, b_ref, o_ref):
    w = w_ref[0].astype(BF16)
    o_ref[0] = jnp.dot(a_ref[...], w, preferred_element_type=F32) + b_ref[0]


def modulation(act, w_mod, b_mod):
    depth, d, n = w_mod.shape
    tn = 1536
    return pl.pallas_call(
        _mod_kernel,
        grid=(depth, n // tn),
        in_specs=[
            pl.BlockSpec((8, d), lambda l, j: (0, 0)),
            pl.BlockSpec((1, d, tn), lambda l, j: (l, 0, j)),
            pl.BlockSpec((1, 1, tn), lambda l, j: (l, 0, j)),
        ],
        out_specs=pl.BlockSpec((1, 8, tn), lambda l, j: (l, 0, j)),
        out_shape=jax.ShapeDtypeStruct((depth, 8, n), F32),
        compiler_params=_cparams("parallel", "parallel"),
        name="modulation",
    )(act, w_mod, b_mod.reshape(depth, 1, n))


def _norm_mod(x, gain, shift, scale):
    ms = jnp.mean(x * x, axis=-1, keepdims=True)
    h = x * lax.rsqrt(ms + RMS_EPS) * gain
    return h * (1.0 + scale) + shift


def _head_rms(z):
    return lax.rsqrt(jnp.mean(z * z, axis=-1, keepdims=True) + RMS_EPS)


def _nt_dot(a, b):
    return lax.dot_general(a, b, (((1,), (1,)), ((), ())), preferred_element_type=F32)


def _tn_dot(a, b):
    return lax.dot_general(a, b, (((0,), (0,)), ((), ())), preferred_element_type=F32)


def _transpose_heads(ot, o_ref, g, tq):
    tb = min(tq, 256)
    eye = (lax.broadcasted_iota(jnp.int32, (tb, tb), 0)
           == lax.broadcasted_iota(jnp.int32, (tb, tb), 1)).astype(BF16)
    for h in range(g):
        for j in range(tq // tb):
            lo = h * tq + j * tb
            o_ref[h, j * tb:(j + 1) * tb, :] = _nt_dot(eye, ot[:, lo:lo + tb]).astype(o_ref.dtype)


_EV_QA, _EV_KA, _EV_VA, _EV_QB, _EV_KB, _EV_VB = (int(o) for o in np.cumsum((0,) + EVEN_WIDTHS)[:-1])
_EV_QAR = sum(EVEN_WIDTHS)
_EV_KAR = _EV_QAR + A_Q
_EV_QBR = _EV_KAR + A_KV
_EV_KBR = _EV_QBR + B_Q
EVEN_EXT = _EV_KBR + B_KV


def _even_proj_kernel(x_ref, gain_ref, sh_ref, sc_ref, w_ref, wvt_ref, cos_ref, sin_ref, qg_ref, kg_ref,
                      qa_ref, ka_ref, vat_ref, qb_ref, kb_ref, vbt_ref):
    h = _norm_mod(x_ref[0], gain_ref[...], sh_ref[0], sc_ref[0]).astype(BF16)
    z = jnp.dot(h, w_ref[...], preferred_element_type=F32)
    vt = _nt_dot(wvt_ref[...], h)
    pad_rows = (lax.broadcasted_iota(jnp.int32, (VT_ROWS - HEAD_DIM, vt.shape[1]), 0) == 0).astype(BF16)
    cos = cos_ref[...]
    sin = sin_ref[...]
    qg = qg_ref[...]
    kg = kg_ref[...]

    def sl(off, i):
        return z[:, off + i * HEAD_DIM: off + (i + 1) * HEAD_DIM]

    for i in range(A_HEADS):
        zq = sl(_EV_QA, i)
        qa = _head_rms(zq) * (zq * qg * cos + sl(_EV_QAR, i) * sin)
        qa_ref[0, i] = (qa * (SCALE * LOG2E)).astype(BF16)
    for i in range(B_HEADS):
        qb_ref[0, i] = ((sl(_EV_QB, i) * cos + sl(_EV_QBR, i) * sin) * SCALE).astype(BF16)
    for i in range(A_KV_HEADS):
        zk = sl(_EV_KA, i)
        ka_ref[0, i] = (_head_rms(zk) * (zk * kg * cos + sl(_EV_KAR, i) * sin)).astype(BF16)
        vat_ref[0, i, :HEAD_DIM, :] = vt[i * HEAD_DIM:(i + 1) * HEAD_DIM].astype(BF16)
        vat_ref[0, i, HEAD_DIM:, :] = pad_rows
    for i in range(B_KV_HEADS):
        kb_ref[0, i] = (sl(_EV_KB, i) * cos + sl(_EV_KBR, i) * sin).astype(BF16)
        vbt_ref[0, i, :HEAD_DIM, :] = vt[A_KV + i * HEAD_DIM:A_KV + (i + 1) * HEAD_DIM].astype(BF16)
        vbt_ref[0, i, HEAD_DIM:, :] = pad_rows


def _rot_cols(w):
    d = w.shape[0]
    wh = w.reshape(d, -1, 2, ROPE_HALF)
    return jnp.stack([-wh[:, :, 1], wh[:, :, 0]], axis=2).reshape(w.shape)


def even_weight(w_in, q_gain, k_gain):
    qa, ka, va, qb, kb, vb = _split(w_in, EVEN_WIDTHS)
    qar = _rot_cols(qa * jnp.tile(q_gain, A_HEADS))
    kar = _rot_cols(ka * jnp.tile(k_gain, A_KV_HEADS))
    return jnp.concatenate([w_in, qar, kar, _rot_cols(qb), _rot_cols(kb)], axis=1).astype(BF16)


def even_proj(x, gain, shift, scale, w_ext, w_vt, cos2, sin2, q_gain, k_gain):
    b, s, d = x.shape
    tm = _pick_tile(s, 512)
    vec = pl.BlockSpec((1, 1, d), lambda bi, i: (bi, 0, 0))
    tab = pl.BlockSpec((tm, HEAD_DIM), lambda bi, i: (i, 0))
    hv = pl.BlockSpec((1, HEAD_DIM), lambda bi, i: (0, 0))

    def hm(n):
        return pl.BlockSpec((1, n, tm, HEAD_DIM), lambda bi, i: (bi, 0, i, 0))

    def shape(n):
        return jax.ShapeDtypeStruct((b, n, s, HEAD_DIM), BF16)

    def fm(n):
        return pl.BlockSpec((1, n, VT_ROWS, tm), lambda bi, i: (bi, 0, 0, i))

    def fshape(n):
        return jax.ShapeDtypeStruct((b, n, VT_ROWS, s), BF16)

    return pl.pallas_call(
        _even_proj_kernel,
        grid=(b, s // tm),
        in_specs=[
            pl.BlockSpec((1, tm, d), lambda bi, i: (bi, i, 0)),
            pl.BlockSpec((1, d), lambda bi, i: (0, 0)),
            vec, vec,
            pl.BlockSpec((d, EVEN_EXT), lambda bi, i: (0, 0)),
            pl.BlockSpec((A_KV + B_KV, d), lambda bi, i: (0, 0)),
            tab, tab, hv, hv,
        ],
        out_specs=[hm(A_HEADS), hm(A_KV_HEADS), fm(A_KV_HEADS), hm(B_HEADS), hm(B_KV_HEADS), fm(B_KV_HEADS)],
        out_shape=[shape(A_HEADS), shape(A_KV_HEADS), fshape(A_KV_HEADS),
                   shape(B_HEADS), shape(B_KV_HEADS), fshape(B_KV_HEADS)],
        compiler_params=_cparams("parallel", "parallel"),
        name="even_proj",
    )(x, gain, shift, scale, w_ext, w_vt, cos2, sin2, q_gain, k_gain)


def _odd_proj_kernel(x_ref, gain_ref, sh_ref, sc_ref, w_ref, q_ref, k_ref, v_ref, zd_ref):
    h = _norm_mod(x_ref[0], gain_ref[...], sh_ref[0], sc_ref[0])
    z = jnp.dot(h.astype(BF16), w_ref[...], preferred_element_type=F32)
    for i in range(C_HEADS):
        lo = i * HEAD_DIM
        q_ref[0, i] = (z[:, lo:lo + HEAD_DIM] * SCALE).astype(BF16)
        k_ref[0, i] = z[:, C_W + lo:C_W + lo + HEAD_DIM].astype(BF16)
        v_ref[0, i] = z[:, 2 * C_W + lo:2 * C_W + lo + HEAD_DIM].astype(BF16)
    zd_ref[0] = z[:, 3 * C_W:]


def odd_proj(x, gain, shift, scale, w):
    b, s, d = x.shape
    n = w.shape[1]
    tm = _pick_tile(s, 512)
    vec = pl.BlockSpec((1, 1, d), lambda bi, i: (bi, 0, 0))
    hm = pl.BlockSpec((1, C_HEADS, tm, HEAD_DIM), lambda bi, i: (bi, 0, i, 0))
    hshape = jax.ShapeDtypeStruct((b, C_HEADS, s, HEAD_DIM), BF16)
    return pl.pallas_call(
        _odd_proj_kernel,
        grid=(b, s // tm),
        in_specs=[
            pl.BlockSpec((1, tm, d), lambda bi, i: (bi, i, 0)),
            pl.BlockSpec((1, d), lambda bi, i: (0, 0)),
            vec, vec,
            pl.BlockSpec((d, n), lambda bi, i: (0, 0)),
        ],
        out_specs=[hm, hm, hm, pl.BlockSpec((1, tm, D_SHIFT_W), lambda bi, i: (bi, i, 0))],
        out_shape=[hshape, hshape, hshape, jax.ShapeDtypeStruct((b, s, D_SHIFT_W), F32)],
        compiler_params=_cparams("parallel", "parallel"),
        name="odd_proj",
    )(x, gain, shift, scale, w)


def _out_proj_kernel(x_ref, g_ref, y1_ref, y2_ref, w1_ref, w2_ref, o_ref, buf1_ref, buf2_ref, *, y2_heads):
    def gather_heads(y_ref, buf_ref):
        for h in range(y_ref.shape[1]):
            buf_ref[:, h * HEAD_DIM:(h + 1) * HEAD_DIM] = y_ref[0, h]
        return buf_ref[...]

    acc = jnp.dot(gather_heads(y1_ref, buf1_ref), w1_ref[...], preferred_element_type=F32)
    y2 = gather_heads(y2_ref, buf2_ref) if y2_heads else y2_ref[0]
    acc = acc + jnp.dot(y2, w2_ref[...], preferred_element_type=F32)
    o_ref[0] = x_ref[0] + g_ref[0] * acc


def out_proj(x, gate, y1, y2, w):
    b, s, d = x.shape
    nh = y1.shape[1]
    half = nh * HEAD_DIM
    tm = _pick_tile(s, 1024)
    y2_heads = y2.ndim == 4
    hm = pl.BlockSpec((1, nh, tm, HEAD_DIM), lambda bi, i: (bi, 0, i, 0))
    wspec = pl.BlockSpec((half, d), lambda bi, i: (0, 0))
    y2_spec = hm if y2_heads else pl.BlockSpec((1, tm, half), lambda bi, i: (bi, i, 0))
    return pl.pallas_call(
        functools.partial(_out_proj_kernel, y2_heads=y2_heads),
        grid=(b, s // tm),
        in_specs=[
            pl.BlockSpec((1, tm, d), lambda bi, i: (bi, i, 0)),
            pl.BlockSpec((1, 1, d), lambda bi, i: (bi, 0, 0)),
            hm, y2_spec, wspec, wspec,
        ],
        out_specs=pl.BlockSpec((1, tm, d), lambda bi, i: (bi, i, 0)),
        out_shape=jax.ShapeDtypeStruct((b, s, d), F32),
        scratch_shapes=[pltpu.VMEM((tm, half), BF16), pltpu.VMEM((tm, half), BF16)],
        compiler_params=_cparams("parallel", "parallel"),
        name="out_proj",
    )(x, gate, y1, y2, w[:half], w[half:])


def _ffn_kernel(x_ref, gain_ref, sh_ref, sc_ref, g_ref, wg_ref, wu_ref, wo_ref, o_ref, h_ref, acc_ref):
    j = pl.program_id(2)

    @pl.when(j == 0)
    def _():
        h_ref[...] = _norm_mod(x_ref[0], gain_ref[...], sh_ref[0], sc_ref[0]).astype(BF16)
        acc_ref[...] = jnp.zeros_like(acc_ref)

    h = h_ref[...]
    gate = jnp.dot(h, wg_ref[...], preferred_element_type=F32)
    up = jnp.dot(h, wu_ref[...], preferred_element_type=F32)
    act = gate * (1.0 / (1.0 + jnp.exp(-gate))) * up
    acc_ref[...] += jnp.dot(act.astype(BF16), wo_ref[...], preferred_element_type=F32)

    @pl.when(j == pl.num_programs(2) - 1)
    def _():
        o_ref[0] = x_ref[0] + g_ref[0] * acc_ref[...]


def ffn(x, gain, shift, scale, gate, w_in, w_out):
    b, s, d = x.shape
    f = w_out.shape[0]
    tm = _pick_tile(s, 512)
    th = 1408 if f % 1408 == 0 else _pick_tile(f, 256)
    nj = f // th
    vec = pl.BlockSpec((1, 1, d), lambda bi, i, j: (bi, 0, 0))
    return pl.pallas_call(
        _ffn_kernel,
        grid=(b, s // tm, nj),
        in_specs=[
            pl.BlockSpec((1, tm, d), lambda bi, i, j: (bi, i, 0)),
            pl.BlockSpec((1, d), lambda bi, i, j: (0, 0)),
            vec, vec, vec,
            pl.BlockSpec((d, th), lambda bi, i, j: (0, j)),
            pl.BlockSpec((d, th), lambda bi, i, j: (0, j + nj)),
            pl.BlockSpec((th, d), lambda bi, i, j: (j, 0)),
        ],
        out_specs=pl.BlockSpec((1, tm, d), lambda bi, i, j: (bi, i, 0)),
        out_shape=jax.ShapeDtypeStruct((b, s, d), F32),
        scratch_shapes=[pltpu.VMEM((tm, d), BF16), pltpu.VMEM((tm, d), F32)],
        compiler_params=_cparams("parallel", "parallel", "arbitrary"),
        name="ffn",
    )(x, gain, shift, scale, gate, w_in, w_in, w_out)


def _rms_kernel(x_ref, gain_ref, o_ref):
    x = x_ref[0]
    ms = jnp.mean(x * x, axis=-1, keepdims=True)
    o_ref[0] = x * lax.rsqrt(ms + RMS_EPS) * gain_ref[...]


def final_norm(x, gain):
    b, s, d = x.shape
    tm = _pick_tile(s, 1024)
    return pl.pallas_call(
        _rms_kernel,
        grid=(b, s // tm),
        in_specs=[pl.BlockSpec((1, tm, d), lambda bi, i: (bi, i, 0)),
                  pl.BlockSpec((1, d), lambda bi, i: (0, 0))],
        out_specs=pl.BlockSpec((1, tm, d), lambda bi, i: (bi, i, 0)),
        out_shape=jax.ShapeDtypeStruct((b, s, d), F32),
        compiler_params=_cparams("parallel", "parallel"),
        name="final_norm",
    )(x, gain)


def _flash_kernel(q_ref, k_ref, vt_ref, kx_ref, vxt_ref, o_ref, m_ref, acc_ref):
    kv = pl.program_id(3)
    g, tq, hd = q_ref.shape[1:]

    @pl.when(kv == 0)
    def _():
        m_ref[...] = jnp.full_like(m_ref, -jnp.inf)
        acc_ref[...] = jnp.zeros_like(acc_ref)

    q = q_ref[0].reshape(g * tq, hd)

    def scores(k):
        return _nt_dot(k, q)

    def accumulate(s, vt):
        m_prev = m_ref[...]
        m_new = jnp.maximum(m_prev, jnp.max(s, axis=0, keepdims=True))
        p = jnp.exp2(s - m_new).astype(BF16)
        acc_ref[...] = jnp.exp2(m_prev - m_new) * acc_ref[...] + jnp.dot(vt, p, preferred_element_type=F32)
        m_ref[...] = m_new

    tk = k_ref.shape[2]
    sub = min(tk, FLASH_SUB_KEYS)
    nsub = tk // sub
    s_next = scores(k_ref[0, 0, 0:sub, :])
    for c in range(nsub):
        s_cur = s_next
        if c + 1 < nsub:
            s_next = scores(k_ref[0, 0, (c + 1) * sub:(c + 2) * sub, :])
        accumulate(s_cur, vt_ref[0, 0, :, c * sub:(c + 1) * sub])

    @pl.when(kv == pl.num_programs(3) - 1)
    def _():
        accumulate(scores(kx_ref[0, 0]), vxt_ref[0, 0])
        acc = acc_ref[...]
        ot = (acc[:hd] / acc[hd:hd + 1]).astype(BF16)
        _transpose_heads(ot, o_ref.at[0], g, tq)


def global_attention(q, k, vt, kx, vxt):
    b, h, s, hd = q.shape
    hkv = k.shape[1]
    n_ctx = kx.shape[2]
    vr = vt.shape[2]
    g = h // hkv
    tq = _pick_tile(s, FLASH_TQ)
    tk = _pick_tile(s, FLASH_TK)
    q_spec = pl.BlockSpec((1, g, tq, hd), lambda bi, gi, i, j: (bi, gi, i, 0))
    return pl.pallas_call(
        _flash_kernel,
        grid=(b, hkv, s // tq, s // tk),
        in_specs=[
            q_spec,
            pl.BlockSpec((1, 1, tk, hd), lambda bi, gi, i, j: (bi, gi, j, 0)),
            pl.BlockSpec((1, 1, vr, tk), lambda bi, gi, i, j: (bi, gi, 0, j)),
            pl.BlockSpec((1, 1, n_ctx, hd), lambda bi, gi, i, j: (bi, gi, 0, 0)),
            pl.BlockSpec((1, 1, vr, n_ctx), lambda bi, gi, i, j: (bi, gi, 0, 0)),
        ],
        out_specs=q_spec,
        out_shape=jax.ShapeDtypeStruct((b, h, s, hd), BF16),
        scratch_shapes=[pltpu.VMEM((1, g * tq), F32), pltpu.VMEM((vr, g * tq), F32)],
        compiler_params=_cparams("parallel", "parallel", "parallel", "arbitrary"),
        name="global_attention",
    )(q, k, vt, kx, vxt)


def _window_kernel(sink_ref, band_ref, q_ref, kp_ref, kc_ref, kn_ref, vp_ref, vc_ref, vn_ref, kx_ref, vxt_ref,
                   o_ref, *, seq):
    i = pl.program_id(1)
    groups = range(kc_ref.shape[1])
    g = q_ref.shape[1] // kc_ref.shape[1]
    tq, hd = q_ref.shape[2:]
    w = kp_ref.shape[2]
    nq = g * tq
    span = tq + 2 * w
    kpos = i * tq - w + lax.broadcasted_iota(jnp.int32, (span, 1), 0)
    outside = jnp.where((kpos >= 0) & (kpos < seq), 0.0, NEG_INF)
    mask = band_ref[...] + outside
    head = lax.broadcasted_iota(jnp.int32, (1, nq), 1) // tq
    q = [q_ref[0, gi * g:(gi + 1) * g].reshape(nq, hd) for gi in groups]
    k = [jnp.concatenate([kp_ref[0, gi], kc_ref[0, gi], kn_ref[0, gi]], axis=0) for gi in groups]
    vt = [jnp.concatenate([vp_ref[0, gi], vc_ref[0, gi], vn_ref[0, gi]], axis=1) for gi in groups]
    s_loc = [_nt_dot(k[gi], q[gi]) + mask for gi in groups]
    s_ctx = [_nt_dot(kx_ref[0, gi], q[gi]) for gi in groups]
    sink = []
    for gi in groups:
        col = jnp.zeros((1, nq), F32)
        for hh in range(g):
            col = jnp.where(head == hh, sink_ref[gi * g + hh], col)
        sink.append(col)
    m = [jnp.maximum(jnp.maximum(jnp.max(s_loc[gi], axis=0, keepdims=True),
                                 jnp.max(s_ctx[gi], axis=0, keepdims=True)), sink[gi]) for gi in groups]
    acc = [jnp.dot(vt[gi], jnp.exp(s_loc[gi] - m[gi]).astype(BF16), preferred_element_type=F32)
           + jnp.dot(vxt_ref[0, gi], jnp.exp(s_ctx[gi] - m[gi]).astype(BF16), preferred_element_type=F32)
           for gi in groups]
    for gi in groups:
        l = acc[gi][hd:hd + 1] + jnp.exp(sink[gi] - m[gi])
        _transpose_heads((acc[gi][:hd] / l).astype(BF16), o_ref.at[0, gi * g:(gi + 1) * g], g, tq)


def window_attention(q, k, vt, kx, vxt, sink):
    b, h, s, hd = q.shape
    hkv = k.shape[1]
    vr = vt.shape[2]
    g = h // hkv
    n_ctx = kx.shape[2]
    tq = _pick_tile(s, 256)
    r = tq // WINDOW
    nwb = s // WINDOW

    def before(i):
        return jnp.maximum(i * r - 1, 0)

    def after(i):
        return jnp.minimum((i + 1) * r, nwb - 1)

    span = tq + 2 * WINDOW
    krow = np.arange(span)[:, None] - WINDOW
    qcol = np.arange(g * tq)[None, :] % tq
    band = jnp.asarray(np.where(np.abs(krow - qcol) <= WINDOW, 0.0, NEG_INF), F32)
    qspec = pl.BlockSpec((1, h, tq, hd), lambda bi, i: (bi, 0, i, 0))
    return pl.pallas_call(
        functools.partial(_window_kernel, seq=s),
        grid=(b, s // tq),
        in_specs=[
            pl.BlockSpec(memory_space=pltpu.SMEM),
            pl.BlockSpec((span, g * tq), lambda bi, i: (0, 0)),
            qspec,
            pl.BlockSpec((1, hkv, WINDOW, hd), lambda bi, i: (bi, 0, before(i), 0)),
            pl.BlockSpec((1, hkv, tq, hd), lambda bi, i: (bi, 0, i, 0)),
            pl.BlockSpec((1, hkv, WINDOW, hd), lambda bi, i: (bi, 0, after(i), 0)),
            pl.BlockSpec((1, hkv, vr, WINDOW), lambda bi, i: (bi, 0, 0, before(i))),
            pl.BlockSpec((1, hkv, vr, tq), lambda bi, i: (bi, 0, 0, i)),
            pl.BlockSpec((1, hkv, vr, WINDOW), lambda bi, i: (bi, 0, 0, after(i))),
            pl.BlockSpec((1, hkv, n_ctx, hd), lambda bi, i: (bi, 0, 0, 0)),
            pl.BlockSpec((1, hkv, vr, n_ctx), lambda bi, i: (bi, 0, 0, 0)),
        ],
        out_specs=qspec,
        out_shape=jax.ShapeDtypeStruct((b, h, s, hd), BF16),
        compiler_params=_cparams("parallel", "parallel"),
        name="window_attention",
    )(sink, band, q, k, k, k, vt, vt, vt, kx, vxt)


def _ctx_attn_kernel(sink_ref, q_ref, k_ref, v_ref, o_ref, *, base2, feature_major):
    hi = pl.program_id(1)
    s = _nt_dot(q_ref[0, 0], k_ref[0, 0])
    sink = sink_ref[hi]
    m = jnp.maximum(jnp.max(s, axis=-1, keepdims=True), sink)
    ex = jnp.exp2 if base2 else jnp.exp
    p = ex(s - m)
    l = jnp.sum(p, axis=-1, keepdims=True) + ex(sink - m)
    if feature_major:
        o = _nt_dot(p.astype(BF16), v_ref[0, 0, :HEAD_DIM, :])
    else:
        o = jnp.dot(p.astype(BF16), v_ref[0, 0], preferred_element_type=F32)
    o_ref[0, 0] = (o / l).astype(o_ref.dtype)


def ctx_attention(q, k, v, sink, base2=False, feature_major=False):
    b, h, n, hd = q.shape
    g = h // k.shape[1]
    kv = pl.BlockSpec((1, 1, n, hd), lambda bi, hi: (bi, hi // g, 0, 0))
    vs = pl.BlockSpec((1, 1, v.shape[2], n), lambda bi, hi: (bi, hi // g, 0, 0)) if feature_major else kv
    qs = pl.BlockSpec((1, 1, n, hd), lambda bi, hi: (bi, hi, 0, 0))
    return pl.pallas_call(
        functools.partial(_ctx_attn_kernel, base2=base2, feature_major=feature_major),
        grid=(b, h),
        in_specs=[pl.BlockSpec(memory_space=pltpu.SMEM), qs, kv, vs],
        out_specs=qs,
        out_shape=jax.ShapeDtypeStruct((b, h, n, hd), BF16),
        compiler_params=_cparams("parallel", "parallel"),
        name="ctx_attention",
    )(sink, q, k, v)


def _na_kernel(q_ref, k_ref, v_ref, kx_ref, vx_ref, bias_ref, o_ref, *, rows):
    t = pl.program_id(2)
    nkeys = NA_KEY_ROWS * GRID_W
    kstart = jnp.clip(t * NA_TILE_ROWS - NA_ROWS // 2, 0, rows - NA_KEY_ROWS)
    off = pl.multiple_of(kstart * GRID_W, GRID_W)
    heads = range(q_ref.shape[1])
    tq = q_ref.shape[2]
    q = [q_ref[0, h] for h in heads]
    s_loc = [_nt_dot(k_ref[0, h, pl.ds(off, nkeys), :], q[h]) + bias_ref[0, h] for h in heads]
    s_ctx = [_nt_dot(kx_ref[0, h], q[h]) for h in heads]
    m = [jnp.maximum(jnp.max(s_loc[h], axis=0, keepdims=True), jnp.max(s_ctx[h], axis=0, keepdims=True))
         for h in heads]
    p_loc = [jnp.exp(s_loc[h] - m[h]) for h in heads]
    p_ctx = [jnp.exp(s_ctx[h] - m[h]) for h in heads]
    l = [jnp.sum(p_loc[h], axis=0, keepdims=True) + jnp.sum(p_ctx[h], axis=0, keepdims=True) for h in heads]
    ot = [_tn_dot(v_ref[0, h, pl.ds(off, nkeys), :], p_loc[h].astype(BF16))
          + _tn_dot(vx_ref[0, h], p_ctx[h].astype(BF16)) for h in heads]
    for h in heads:
        _transpose_heads((ot[h] / l[h]).astype(BF16), o_ref.at[0, h:h + 1], 1, tq)


def na_bias_tables(rpb, rows):
    nh = rpb.shape[0]
    i = np.arange(NA_TILE_ROWS)
    j = np.arange(NA_KEY_ROWS)
    qc = np.arange(GRID_W)
    kc = np.arange(GRID_W)
    cs = np.clip(qc - NA_COLS // 2, 0, GRID_W - NA_COLS)
    col_valid = (kc[:, None] >= cs[None, :]) & (kc[:, None] < cs[None, :] + NA_COLS)
    dcol = np.clip(kc[:, None] - qc[None, :] + NA_COLS - 1, 0, 2 * NA_COLS - 2)
    per_drow = jnp.where(col_valid[None, None], rpb.astype(F32)[:, :, dcol], NEG_INF)
    masked = jnp.full((nh, 1, GRID_W, GRID_W), NEG_INF, F32)
    per_drow = jnp.concatenate([per_drow, masked], axis=1)
    out = []
    for r0 in (0, NA_TILE_ROWS, rows - NA_TILE_ROWS):
        r = r0 + i
        rs = np.clip(r - NA_ROWS // 2, 0, rows - NA_ROWS)
        kstart = int(np.clip(r0 - NA_ROWS // 2, 0, rows - NA_KEY_ROWS))
        krow = kstart + j
        row_valid = (krow[:, None] >= rs[None, :]) & (krow[:, None] < rs[None, :] + NA_ROWS)
        drow = np.where(row_valid, krow[:, None] - r[None, :] + NA_ROWS - 1, 2 * NA_ROWS - 1)
        cols = [per_drow[:, drow[:, ii]] for ii in range(NA_TILE_ROWS)]
        dense = jnp.stack(cols, axis=3)
        out.append(dense.reshape(nh, NA_KEY_ROWS * GRID_W, NA_TILE_ROWS * GRID_W))
    return jnp.stack(out)


def neighborhood_attention(q, k, v, kx, vx, bias):
    b, h, s, hd = q.shape
    rows = s // GRID_W
    nt = rows // NA_TILE_ROWS
    tq = NA_TILE_ROWS * GRID_W
    nkeys = NA_KEY_ROWS * GRID_W
    n_ctx = kx.shape[2]
    hs = NA_HEADS_PER_STEP
    full = pl.BlockSpec((1, hs, s, hd), lambda bi, hi, t: (bi, hi, 0, 0))
    cx = pl.BlockSpec((1, hs, n_ctx, hd), lambda bi, hi, t: (bi, hi, 0, 0))
    tile = pl.BlockSpec((1, hs, tq, hd), lambda bi, hi, t: (bi, hi, t, 0))

    def case(t):
        return jnp.where(t == 0, 0, jnp.where(t == nt - 1, 2, 1))

    return pl.pallas_call(
        functools.partial(_na_kernel, rows=rows),
        grid=(b, h // hs, nt),
        in_specs=[tile, full, full, cx, cx,
                  pl.BlockSpec((1, hs, nkeys, tq), lambda bi, hi, t: (case(t), hi, 0, 0))],
        out_specs=tile,
        out_shape=jax.ShapeDtypeStruct((b, h, s, hd), BF16),
        compiler_params=_cparams("parallel", "parallel", "arbitrary"),
        name="neighborhood_attention",
    )(q, k, v, kx, vx, bias)


def _mm(a, b):
    return jnp.dot(a.astype(BF16), b.astype(BF16), preferred_element_type=F32)


def _tri_cumsum(tri, x):
    t16 = tri.astype(BF16)
    hi = x.astype(BF16)
    r1 = x - hi.astype(F32)
    mid = r1.astype(BF16)
    lo = (r1 - mid.astype(F32)).astype(BF16)
    out = jnp.dot(t16, hi, preferred_element_type=F32)
    out = out + jnp.dot(t16, mid, preferred_element_type=F32)
    return out + jnp.dot(t16, lo, preferred_element_type=F32)


def _mm_nt(a, b):
    return lax.dot_general(a.astype(BF16), b.astype(BF16), (((1,), (1,)), ((), ())),
                           preferred_element_type=F32)


def _mm_tn(a, b):
    return lax.dot_general(a.astype(BF16), b.astype(BF16), (((0,), (0,)), ((), ())),
                           preferred_element_type=F32)


def _rwkv_chains(chains):
    t, w = chains[0][0].shape
    nh = w // HEAD_DIM
    n = range(len(chains))
    lw, kd, a, r, v, kap, st, rev = (list(col) for col in zip(*chains))
    row = lax.broadcasted_iota(jnp.int32, (t, w), 0)
    colj = lax.broadcasted_iota(jnp.int32, (t, w), 1) % t
    lane_head = lax.broadcasted_iota(jnp.int32, (1, w), 1) // HEAD_DIM
    strict_d = (colj < row, colj > row)
    incl_d = (colj <= row, colj >= row)
    tri_d = tuple(m[:, :t].astype(F32) for m in incl_d)
    strict = [strict_d[int(rev[i])] for i in n]
    incl = [incl_d[int(rev[i])] for i in n]

    def bd(y):
        return jnp.concatenate([jnp.where(lane_head == h, y, 0.0) for h in range(nh)], axis=0)

    def bdmm(xs, ys):
        return [_mm(xs[i], bd(ys[i])) for i in n]

    c_incl = [_tri_cumsum(tri_d[int(rev[i])], lw[i]) for i in n]
    c_excl = [c_incl[i] - lw[i] for i in n]
    c_mid = [c_incl[i][t // 2:t // 2 + 1, :] for i in n]
    c_end = [c_incl[i][0:1, :] if rev[i] else c_incl[i][t - 1:t, :] for i in n]
    bvec = [kap[i] * a[i] for i in n]
    e_neg = [jnp.exp(c_mid[i] - c_incl[i]) for i in n]
    left = [jnp.concatenate([kap[i] * jnp.exp(c_excl[i] - c_mid[i]), r[i] * jnp.exp(c_incl[i] - c_mid[i])],
                            axis=0) for i in n]
    wt = [jnp.concatenate([bd(bvec[i] * e_neg[i]), bd(kd[i] * e_neg[i])], axis=0) for i in n]
    m = [_mm_nt(left[i], wt[i]) for i in n]
    l_b = [jnp.where(strict[i], m[i][:t, :w], 0.0) for i in n]
    l_k = [jnp.where(strict[i], m[i][:t, w:], 0.0) for i in n]
    m_br = [jnp.where(incl[i], m[i][t:, :w], 0.0) for i in n]
    m_kr = [jnp.where(incl[i], m[i][t:, w:], 0.0) for i in n]

    blk = RWKV_INV_BLOCK
    same = row // blk == colj // blk
    eye = jnp.where(row == colj, 1.0, 0.0)
    pw = [-jnp.where(same, l_b[i], 0.0) for i in n]
    inv = [eye + pw[i] for i in n]
    span = 2
    while span < blk:
        pw = bdmm(pw, pw)
        step = bdmm(inv, pw)
        inv = [inv[i] + step[i] for i in n]
        span *= 2
    while blk < t:
        pair = (row // (2 * blk) == colj // (2 * blk)) & (row // blk != colj // blk)
        off = [jnp.where(pair, l_b[i], 0.0) for i in n]
        step = bdmm(bdmm(inv, off), inv)
        inv = [inv[i] - step[i] for i in n]
        blk *= 2

    kt = [kap[i] * jnp.exp(c_excl[i]) for i in n]
    rt = [r[i] * jnp.exp(c_incl[i]) for i in n]
    both = bdmm([jnp.concatenate([l_k[i], m_kr[i]], axis=0) for i in n], v)
    lkv = [both[i][:t] for i in n]
    mkv = [both[i][t:] for i in n]
    wk = bdmm(inv, kt)
    u0 = bdmm(inv, lkv)
    mbw = bdmm(m_br, wk)
    mbu = bdmm(m_br, u0)
    rq = [rt[i] - mbw[i] for i in n]
    y0 = [mkv[i] - mbu[i] for i in n]
    e_end = [jnp.exp(c_end[i] - c_incl[i]) for i in n]
    b_hat = [bvec[i] * e_end[i] for i in n]
    k_hat = [kd[i] * e_end[i] for i in n]
    ri = lax.broadcasted_iota(jnp.int32, (w, w), 0)
    ci = lax.broadcasted_iota(jnp.int32, (w, w), 1)
    same_head = ri // HEAD_DIM == ci // HEAD_DIM
    diag = ri == ci
    bwu = [_mm_tn(b_hat[i], jnp.concatenate([wk[i], u0[i]], axis=1)) for i in n]
    bw = [bwu[i][:, :w] for i in n]
    bu = [bwu[i][:, w:] for i in n]
    kv = [_mm_tn(k_hat[i], v[i]) for i in n]
    gt = [jnp.where(diag, jnp.exp(c_end[i]), 0.0) - jnp.where(same_head, bw[i], 0.0) for i in n]
    c0t = [jnp.where(same_head, kv[i] - bu[i], 0.0) for i in n]
    ys = [_mm(rq[i], st[i]) + y0[i] for i in n]
    sts = [_mm(gt[i], st[i]) + c0t[i] for i in n]
    return ys, sts


def _rwkv_kernel(lwf_ref, kdf_ref, af_ref, lwb_ref, kdb_ref, ab_ref, rf_ref, vf_ref, kkf_ref,
                 rb_ref, vb_ref, kkb_ref, sin_ref, yf_ref, yb_ref, sout_ref, st_ref):
    c = pl.program_id(0)

    @pl.when(c == 0)
    def _():
        st_ref[...] = sin_ref[...]

    bsz = lwf_ref.shape[0]
    groups = lwf_ref.shape[2] // RWKV_LANES
    chains, slots = [], []
    for bi in range(bsz):
        for d, (lw_ref, kd_ref, a_ref, r_ref, v_ref, kk_ref, y_ref) in enumerate((
                (lwf_ref, kdf_ref, af_ref, rf_ref, vf_ref, kkf_ref, yf_ref),
                (lwb_ref, kdb_ref, ab_ref, rb_ref, vb_ref, kkb_ref, yb_ref))):
            for hg in range(groups):
                lanes = slice(hg * RWKV_LANES, (hg + 1) * RWKV_LANES)
                chain = d * groups + hg
                chains.append((lw_ref[bi, :, lanes], kd_ref[bi, :, lanes], a_ref[bi, :, lanes],
                               r_ref[bi, :, lanes], v_ref[bi, :, lanes], kk_ref[bi, :, lanes],
                               st_ref[bi, chain], bool(d)))
                slots.append((y_ref, bi, lanes, chain))
    ys, sts = _rwkv_chains(chains)
    for (y_ref, bi, lanes, chain), y, st in zip(slots, ys, sts):
        y_ref[bi, :, lanes] = y
        st_ref[bi, chain] = st

    @pl.when(c == pl.num_programs(0) - 1)
    def _():
        sout_ref[...] = st_ref[...]


def rwkv_scan(lw, kd, a, r, v, kk, state):
    b, length, w = r.shape
    t = RWKV_CHUNK
    nc = length // t
    fwd = pl.BlockSpec((b, t, w), lambda c: (0, c, 0))
    bwd = pl.BlockSpec((b, t, w), lambda c: (0, nc - 1 - c, 0))
    st_spec = pl.BlockSpec(state.shape, lambda c: (0, 0, 0, 0))
    y_shape = jax.ShapeDtypeStruct((b, length, w), F32)
    return pl.pallas_call(
        _rwkv_kernel,
        grid=(nc,),
        in_specs=[fwd, fwd, fwd, bwd, bwd, bwd, fwd, fwd, fwd, bwd, bwd, bwd, st_spec],
        out_specs=[fwd, bwd, st_spec],
        out_shape=[y_shape, y_shape, jax.ShapeDtypeStruct(state.shape, F32)],
        scratch_shapes=[pltpu.VMEM(state.shape, F32)],
        compiler_params=_cparams("arbitrary"),
        name="rwkv_scan",
    )(lw[0], kd[0], a[0], lw[1], kd[1], a[1], r, v, kk, r, v, kk, state)


def _exact_dot(a, b):
    return jnp.dot(a, b, preferred_element_type=F32, precision=lax.Precision.HIGHEST)


def _head_sum_matrix(width, value):
    r = lax.broadcasted_iota(jnp.int32, (width, width), 0) // HEAD_DIM
    c = lax.broadcasted_iota(jnp.int32, (width, width), 1) // HEAD_DIM
    return jnp.where(r == c, value, 0.0).astype(BF16)


def _head_sum(x, mat):
    hi = x.astype(BF16)
    lo = (x - hi.astype(F32)).astype(BF16)
    return jnp.dot(hi, mat, preferred_element_type=F32) + jnp.dot(lo, mat, preferred_element_type=F32)


def _sigmoid(x):
    return 1.0 / (1.0 + jnp.exp(-x))


def _rwkv_prep_kernel(z_ref, zp_ref, zn_ref, mu_ref, kk_ref, ka_ref, w0_ref, w2_ref, a0_ref, a2_ref, g2_ref,
                      r_ref, v_ref, kn_ref, g_ref, lw0_ref, kd0_ref, av0_ref, lw1_ref, kd1_ref, av1_ref):
    i = pl.program_id(1)
    z = z_ref[0]
    tm = z.shape[0]
    halo = zp_ref.shape[1]
    row = lax.broadcasted_iota(jnp.int32, (tm, 1), 0)
    before = jnp.where(i > 0, zp_ref[0, halo - 1:halo, :], 0.0)
    after = jnp.where(i < pl.num_programs(1) - 1, zn_ref[0, 0:1, :], 0.0)
    z_prev = jnp.where(row == 0, before, pltpu.roll(z, 1, 0))
    z_next = jnp.where(row == tm - 1, after, pltpu.roll(z, tm - 1, 0))
    zs = z + (0.5 * (z_prev + z_next) - z) * mu_ref[...]

    r = zs[:, :D_W]
    k = zs[:, D_W:2 * D_W]
    v = zs[:, 2 * D_W:3 * D_W]
    o = 3 * D_W
    wd = zs[:, o:o + DECAY_LORA]
    ad = zs[:, o + DECAY_LORA:o + DECAY_LORA + ICLR_LORA]
    gd = zs[:, o + DECAY_LORA + ICLR_LORA:]
    r_ref[0] = r
    v_ref[0] = v
    g_ref[0] = _exact_dot(_sigmoid(gd), g2_ref[...])
    kraw = k * kk_ref[...]
    ss = _head_sum(kraw * kraw, _head_sum_matrix(D_W, 1.0))
    kn_ref[0] = kraw / jnp.maximum(jnp.sqrt(ss), 1e-12)
    twd = jnp.tanh(wd)
    for d, (lw_ref, kd_ref, av_ref) in enumerate(((lw0_ref, kd0_ref, av0_ref), (lw1_ref, kd1_ref, av1_ref))):
        x = -(w0_ref[d:d + 1, :] + _exact_dot(twd, w2_ref[d]))
        softplus = jnp.maximum(x, 0.0) + jnp.log(1.0 + jnp.exp(-jnp.abs(x)))
        lw_ref[0] = -jnp.exp(-softplus - 0.5)
        a = _sigmoid(a0_ref[d:d + 1, :] + _exact_dot(ad, a2_ref[d]))
        av_ref[0] = a
        kd_ref[0] = k * (1.0 + (a - 1.0) * ka_ref[...])


def rwkv_prep(zd, mu, k_k, k_a, w0, w2, a0, a2, g2):
    b, length, width = zd.shape
    tm = _pick_tile(length, 512)
    halo = 8
    nb = length // halo
    per = tm // halo
    row = pl.BlockSpec((1, tm, D_W), lambda bi, i: (bi, i, 0))
    shape = jax.ShapeDtypeStruct((b, length, D_W), F32)

    def full(a):
        return pl.BlockSpec(a.shape, lambda bi, i: (0,) * a.ndim)

    params = (mu, k_k, k_a, w0, w2, a0, a2, g2)
    return pl.pallas_call(
        _rwkv_prep_kernel,
        grid=(b, length // tm),
        in_specs=[
            pl.BlockSpec((1, tm, width), lambda bi, i: (bi, i, 0)),
            pl.BlockSpec((1, halo, width), lambda bi, i: (bi, jnp.maximum(i * per - 1, 0), 0)),
            pl.BlockSpec((1, halo, width), lambda bi, i: (bi, jnp.minimum((i + 1) * per, nb - 1), 0)),
        ] + [full(p) for p in params],
        out_specs=[row] * 10,
        out_shape=[shape] * 10,
        compiler_params=_cparams("parallel", "parallel"),
        name="rwkv_prep",
    )(zd, zd, zd, *params)


def _rwkv_readout_kernel(yf_ref, yb_ref, r_ref, v_ref, kd0_ref, kd1_ref, g_ref, rk_ref, lnw_ref, lnb_ref, o_ref):
    y = yf_ref[0] + yb_ref[0]
    avg = _head_sum_matrix(D_W, 1.0 / HEAD_DIM)
    mean = _head_sum(y, avg)
    yc = y - mean
    var = _head_sum(yc * yc, avg)
    yn = yc * lax.rsqrt(var + GN_EPS) * lnw_ref[...] + lnb_ref[...]
    bonus = _head_sum(r_ref[0] * (kd0_ref[0] + kd1_ref[0]) * rk_ref[...], _head_sum_matrix(D_W, 1.0))
    o_ref[0] = ((yn + bonus * v_ref[0]) * g_ref[0]).astype(o_ref.dtype)


def rwkv_readout(yf, yb, r, v, kd0, kd1, g, r_k, ln_w, ln_b):
    b, length, w = yf.shape
    tm = _pick_tile(length, 512)
    row = pl.BlockSpec((1, tm, w), lambda bi, i: (bi, i, 0))
    vec = pl.BlockSpec((1, w), lambda bi, i: (0, 0))
    return pl.pallas_call(
        _rwkv_readout_kernel,
        grid=(b, length // tm),
        in_specs=[row] * 7 + [vec] * 3,
        out_specs=row,
        out_shape=jax.ShapeDtypeStruct((b, length, w), BF16),
        compiler_params=_cparams("parallel", "parallel"),
        name="rwkv_readout",
    )(yf, yb, r, v, kd0, kd1, g, r_k, ln_w, ln_b)


def _rope_tables(n_tokens):
    t = jnp.arange(n_tokens, dtype=jnp.int32)
    row = (t // GRID_W).astype(F32)
    col = (t % GRID_W).astype(F32)
    inv = ROPE_THETA ** (-jnp.arange(ROPE_FREQS, dtype=F32) / ROPE_FREQS)
    ang = jnp.concatenate([row[:, None] * inv, col[:, None] * inv], axis=-1)
    cos, sin = jnp.cos(ang), jnp.sin(ang)
    return jnp.concatenate([cos, cos], axis=-1), jnp.concatenate([sin, sin], axis=-1)


def _rwkv_mixer(zd, zdc, params, r_k, ln_w, ln_b, need_ctx):
    lat = rwkv_prep(zd, *params)
    cx = rwkv_prep(zdc, *params)
    bsz = zd.shape[0]
    chains = 2 * (D_W // RWKV_LANES)
    zero = jnp.zeros((bsz, chains, RWKV_LANES, RWKV_LANES), F32)

    def scan(p, state):
        r, v, kk, _, lw0, kd0, a0, lw1, kd1, a1 = p
        return rwkv_scan((lw0, lw1), (kd0, kd1), (a0, a1), r, v, kk, state)

    def readout(p, yf, yb):
        r, v, _, g, _, kd0, _, _, kd1, _ = p
        return rwkv_readout(yf, yb, r, v, kd0, kd1, g, r_k, ln_w, ln_b)

    ycf, ycb, state = scan(cx, zero)
    yf, yb, _ = scan(lat, state)
    y_d = readout(lat, yf, yb)
    yc_d = readout(cx, ycf, ycb) if need_ctx else None
    return y_d, yc_d


def kernel(x, c, ctx, c_ctx, w_mod, b_mod, norm_mix, norm_ffn, w_in_even, w_out_even, q_norm_a, k_norm_a,
           sink_b, w_in_odd, w_out_odd, rpb_c, shift_mu, decay_w0, decay_w2, iclr_a0, iclr_a2, gate_g2,
           k_k, k_a, r_k, ln_x_w, ln_x_b, w_ffn_in, w_ffn_out, norm_out):
    bsz, s, d = x.shape
    n_ctx = ctx.shape[1]
    cos2, sin2 = _rope_tables(s)
    cos_ctx = jnp.ones((n_ctx, HEAD_DIM), F32)
    sin_ctx = jnp.zeros((n_ctx, HEAD_DIM), F32)
    act = jnp.zeros((8, d), F32).at[:bsz].set(jax.nn.silu(c)).at[bsz].set(jax.nn.silu(c_ctx))
    mods = modulation(act.astype(BF16), w_mod, b_mod)
    w_out_even = w_out_even.astype(BF16)
    w_in_odd = w_in_odd.astype(BF16)
    w_out_odd = w_out_odd.astype(BF16)
    w_ffn_in = w_ffn_in.astype(BF16)
    w_ffn_out = w_ffn_out.astype(BF16)
    no_sink = jnp.full((A_HEADS,), NEG_INF, F32)
    for layer in range(DEPTH):
        need_ctx = layer < DEPTH - 1
        mod = mods[layer, :bsz][:, None, :]
        modc = jnp.broadcast_to(mods[layer, bsz][None, None, :], (bsz, 1, 6 * d))
        sh1, sc1, g1, sh2, sc2, g2 = jnp.split(mod, 6, axis=-1)
        csh1, csc1, cg1, csh2, csc2, cg2 = jnp.split(modc, 6, axis=-1)
        gain_mix = norm_mix[layer][None, :]
        gain_ffn = norm_ffn[layer][None, :]
        i = layer // 2
        if layer % 2 == 0:
            w_out = w_out_even[i]
            qg, kg = q_norm_a[i][None, :], k_norm_a[i][None, :]
            w_ext = even_weight(w_in_even[i], q_norm_a[i], k_norm_a[i])
            w_vt = jnp.concatenate([w_in_even[i][:, _EV_VA:_EV_VA + A_KV],
                                    w_in_even[i][:, _EV_VB:_EV_VB + B_KV]], axis=1).T.astype(BF16)
            qa, ka, vat, qb, kb, vbt = even_proj(x, gain_mix, sh1, sc1, w_ext, w_vt, cos2, sin2, qg, kg)
            qca, kca, vcat, qcb, kcb, vcbt = even_proj(ctx, gain_mix, csh1, csc1, w_ext, w_vt,
                                                        cos_ctx, sin_ctx, qg, kg)
            y1 = global_attention(qa, ka, vat, kca, vcat)
            y2 = window_attention(qb, kb, vbt, kcb, vcbt, sink_b[i])
            if need_ctx:
                yc1 = ctx_attention(qca, kca, vcat, no_sink, base2=True, feature_major=True)
                yc2 = ctx_attention(qcb, kcb, vcbt, sink_b[i], feature_major=True)
        else:
            w_out = w_out_odd[i]
            q, k, v, zd = odd_proj(x, gain_mix, sh1, sc1, w_in_odd[i])
            qc, kc, vc, zdc = odd_proj(ctx, gain_mix, csh1, csc1, w_in_odd[i])
            bias = na_bias_tables(rpb_c[i], s // GRID_W)
            y1 = neighborhood_attention(q, k, v, kc, vc, bias)
            params = (shift_mu[i][None, :], k_k[i][None, :], k_a[i][None, :], decay_w0[i], decay_w2[i],
                      iclr_a0[i], iclr_a2[i], gate_g2[i])
            y2, yc2 = _rwkv_mixer(zd, zdc, params, r_k[i].reshape(1, D_W), ln_x_w[i][None, :],
                                  ln_x_b[i][None, :], need_ctx)
            if need_ctx:
                yc1 = ctx_attention(qc, kc, vc, no_sink)
        x = out_proj(x, g1, y1, y2, w_out)
        x = ffn(x, gain_ffn, sh2, sc2, g2, w_ffn_in[layer], w_ffn_out[layer])
        if need_ctx:
            ctx = out_proj(ctx, cg1, yc1, yc2, w_out)
            ctx = ffn(ctx, gain_ffn, csh2, csc2, cg2, w_ffn_in[layer], w_ffn_out[layer])
    return final_norm(x, norm_out[None, :])
```
